```python
import math
import jax, jax.numpy as jnp
from jax import lax
import numpy as np

D_MODEL = 1024
BATCH = 16
SEQ = 2048
DEPTH = 4
DEC_BATCH = 32
DEC_SEQ = 16
PAST_LEN = 4096

CHUNK = 64
N_MIXERS = 2
N_GLA_LAYERS = (DEPTH + 1) // 2
N_GDN_LAYERS = DEPTH // 2
DEEPNORM_ALPHA = (2.0 * DEPTH) ** 0.25
DEEPNORM_BETA = (8.0 * DEPTH) ** -0.25
LN_EPS = 1e-5
RMS_EPS = 1e-6
L2_EPS = 1e-6
FFN_RES = 0.5
GLA_HEADS = 4
GLA_DK = D_MODEL // 2 // GLA_HEADS
GLA_DV = D_MODEL // GLA_HEADS
GLA_GATE_RANK = 16
GLA_GATE_TAU = 16.0
GLA_QK_W = GLA_HEADS * GLA_DK
GLA_V_W = GLA_HEADS * GLA_DV
GLA_IN = 2 * GLA_QK_W + 2 * GLA_V_W
GDN_HEAD_DIM = 128
GDN_QK_HEADS = D_MODEL // GDN_HEAD_DIM
GDN_V_HEADS = 2 * GDN_QK_HEADS
GDN_KEY_W = GDN_QK_HEADS * GDN_HEAD_DIM
GDN_VAL_W = GDN_V_HEADS * GDN_HEAD_DIM
GDN_CONV = 4
GDN_CONV_DIM = 2 * GDN_KEY_W + GDN_VAL_W
GDN_IN = GDN_CONV_DIM + GDN_VAL_W + 2 * GDN_V_HEADS
D_FF = ((8 * D_MODEL // 3 + 127) // 128) * 128

kernel_name = 'hybrid_gla_gdn_macaron_stream_step'


def layer_norm(x, g, b):
    xf = x.astype(jnp.float32)
    mu = jnp.mean(xf, -1, keepdims=True)
    var = jnp.mean(jnp.square(xf - mu), -1, keepdims=True)
    return ((xf - mu) * lax.rsqrt(var + LN_EPS) * g + b).astype(x.dtype)


def rms_norm(x, g):
    xf = x.astype(jnp.float32)
    return xf * lax.rsqrt(jnp.mean(xf * xf, -1, keepdims=True) + RMS_EPS) * g


def l2_norm(x):
    xf = x.astype(jnp.float32)
    return xf * lax.rsqrt(jnp.sum(xf * xf, -1, keepdims=True) + L2_EPS)


def swiglu_ffn(x, w_gu, w_d):
    gate, up = jnp.split(x @ w_gu, 2, axis=-1)
    return (jax.nn.silu(gate) * up) @ w_d


def to_chunks(a, chunk):
    b, t = a.shape[:2]
    a = a.reshape((b, t // chunk, chunk) + a.shape[2:])
    return jnp.moveaxis(a, (1, 3), (0, 2))


def from_chunks(a):
    a = jnp.moveaxis(a, (0, 2), (1, 3))
    b, n, c, h, d = a.shape
    return a.reshape(b, n * c, h, d)


def gla_recurrence(q, k, v, log_a, s0, chunk):
    mask = jnp.tril(jnp.ones((chunk, chunk), bool))

    def step(s, inp):
        qc, kc, vc, gc = inp
        bcum = jnp.cumsum(gc, axis=2)
        b_last = bcum[:, :, -1:, :]
        q_dec = qc * jnp.exp(bcum)
        k_inv = kc * jnp.exp(-bcum)
        att = jnp.where(mask, jnp.einsum('bhid,bhjd->bhij', q_dec, k_inv), 0.0)
        o = jnp.einsum('bhij,bhjv->bhiv', att, vc) + jnp.einsum('bhid,bhdv->bhiv', q_dec, s)
        k_end = kc * jnp.exp(b_last - bcum)
        s = jnp.exp(b_last[:, :, 0, :])[..., None] * s + jnp.einsum('bhjd,bhjv->bhdv', k_end, vc)
        return s, o

    f32 = jnp.float32
    xs = (to_chunks(q.astype(f32), chunk), to_chunks(k.astype(f32), chunk),
          to_chunks(v.astype(f32), chunk), to_chunks(log_a.astype(f32), chunk))
    s, o = lax.scan(step, s0.astype(f32), xs)
    return from_chunks(o), s


def gated_delta_recurrence(q, k, v, g, beta, s0, chunk):
    dk = q.shape[-1]
    tril_incl = jnp.tril(jnp.ones((chunk, chunk), bool))
    strict = jnp.tril(jnp.ones((chunk, chunk), bool), -1)
    eye = jnp.eye(chunk, dtype=jnp.float32)

    def step(s, inp):
        qc, kc, vc, gc, bc = inp
        gcum = jnp.cumsum(gc, axis=-1)
        diff = gcum[..., :, None] - gcum[..., None, :]
        decay = jnp.where(tril_incl, jnp.exp(jnp.where(tril_incl, diff, 0.0)), 0.0)
        kb = kc * bc[..., None]
        m = jnp.where(strict, jnp.einsum('bhid,bhjd->bhij', kb, kc) * decay, 0.0)
        rhs = jnp.concatenate([kb * jnp.exp(gcum)[..., None], vc * bc[..., None]], axis=-1)
        sol = lax.linalg.triangular_solve(eye + m, rhs, left_side=True, lower=True, unit_diagonal=True)
        w, u = sol[..., :dk], sol[..., dk:]
        v_new = u - jnp.einsum('bhid,bhdv->bhiv', w, s)
        att = jnp.einsum('bhid,bhjd->bhij', qc, kc) * decay
        o = (jnp.einsum('bhid,bhdv->bhiv', qc * jnp.exp(gcum)[..., None], s)
             + jnp.einsum('bhij,bhjv->bhiv', att, v_new))
        g_last = gcum[..., -1]
        k_end = kc * jnp.exp(g_last[..., None] - gcum)[..., None]
        s = jnp.exp(g_last)[..., None, None] * s + jnp.einsum('bhjd,bhjv->bhdv', k_end, v_new)
        return s, o

    f32 = jnp.float32
    xs = (to_chunks(q.astype(f32), chunk), to_chunks(k.astype(f32), chunk), to_chunks(v.astype(f32), chunk),
          to_chunks(g.astype(f32), chunk), to_chunks(beta.astype(f32), chunk))
    s, o = lax.scan(step, s0.astype(f32), xs)
    return from_chunks(o), s


def gla_mixer(x, s0, w_in, w_a1, w_a2, b_a, norm_g, w_o, chunk):
    b, t, _ = x.shape
    q, k, v, r = jnp.split(x @ w_in, [GLA_QK_W, 2 * GLA_QK_W, 2 * GLA_QK_W + GLA_V_W], axis=-1)
    q = q.reshape(b, t, GLA_HEADS, GLA_DK) * (GLA_DK ** -0.5)
    k = k.reshape(b, t, GLA_HEADS, GLA_DK)
    v = v.reshape(b, t, GLA_HEADS, GLA_DV)
    gate_logits = ((x @ w_a1) @ w_a2 + b_a).astype(jnp.float32)
    log_a = (jax.nn.log_sigmoid(gate_logits) / GLA_GATE_TAU).reshape(b, t, GLA_HEADS, GLA_DK)
    o, s = gla_recurrence(q, k, v, log_a, s0, chunk)
    o = rms_norm(o, norm_g).reshape(b, t, GLA_V_W) * jax.nn.silu(r.astype(jnp.float32))
    return o.astype(x.dtype) @ w_o, s


def gdn_mixer(x, s0, conv_hist, w_in, conv_w, a_log, dt_bias, norm_g, w_o, chunk):
    b, t, _ = x.shape
    qkv, z, a, bt = jnp.split(x @ w_in, [GDN_CONV_DIM, GDN_CONV_DIM + GDN_VAL_W,
                                         GDN_CONV_DIM + GDN_VAL_W + GDN_V_HEADS], axis=-1)
    xp = jnp.concatenate([conv_hist.astype(qkv.dtype), qkv], axis=1)
    conv = xp[:, 0:t] * conv_w[0]
    for j in range(1, GDN_CONV):
        conv = conv + xp[:, j:j + t] * conv_w[j]
    new_hist = xp[:, t:]
    qkv = jax.nn.silu(conv)
    q, k, v = jnp.split(qkv, [GDN_KEY_W, 2 * GDN_KEY_W], axis=-1)
    rep = GDN_V_HEADS // GDN_QK_HEADS
    q = jnp.repeat(l2_norm(q.reshape(b, t, GDN_QK_HEADS, GDN_HEAD_DIM)) * (GDN_HEAD_DIM ** -0.5), rep, axis=2)
    k = jnp.repeat(l2_norm(k.reshape(b, t, GDN_QK_HEADS, GDN_HEAD_DIM)), rep, axis=2)
    v = v.reshape(b, t, GDN_V_HEADS, GDN_HEAD_DIM)
    beta = jax.nn.sigmoid(bt.astype(jnp.float32))
    g = -jnp.exp(a_log) * jax.nn.softplus(a.astype(jnp.float32) + dt_bias)
    o, s = gated_delta_recurrence(q, k, v, g, beta, s0, chunk)
    zg = jax.nn.silu(z.astype(jnp.float32)).reshape(b, t, GDN_V_HEADS, GDN_HEAD_DIM)
    o = (rms_norm(o, norm_g) * zg).reshape(b, t, GDN_VAL_W)
    return o.astype(x.dtype) @ w_o, s, new_hist


def trunk(x, s_gla, s_gdn, conv_hist, ln_g, ln_b, ffn1_w_gu, ffn1_w_d, ffn2_w_gu, ffn2_w_d,
          gla_w_in, gla_w_a1, gla_w_a2, gla_b_a, gla_norm_g, gla_w_o,
          gdn_w_in, gdn_conv_w, gdn_a_log, gdn_dt_bias, gdn_norm_g, gdn_w_o):
    chunk = min(CHUNK, x.shape[1])
    new_gla, new_gdn, new_conv = [], [], []
    for i in range(DEPTH):
        x = layer_norm(DEEPNORM_ALPHA * x + FFN_RES * swiglu_ffn(x, ffn1_w_gu[i], ffn1_w_d[i]), ln_g[i, 0], ln_b[i, 0])
        j = i // N_MIXERS
        if i % N_MIXERS == 0:
            y, s = gla_mixer(x, s_gla[:, j], gla_w_in[j], gla_w_a1[j], gla_w_a2[j], gla_b_a[j],
                             gla_norm_g[j], gla_w_o[j], chunk)
            new_gla.append(s)
        else:
            y, s, c = gdn_mixer(x, s_gdn[:, j], conv_hist[:, j], gdn_w_in[j], gdn_conv_w[j], gdn_a_log[j],
                                gdn_dt_bias[j], gdn_norm_g[j], gdn_w_o[j], chunk)
            new_gdn.append(s)
            new_conv.append(c)
        x = layer_norm(DEEPNORM_ALPHA * x + y, ln_g[i, 1], ln_b[i, 1])
        x = layer_norm(DEEPNORM_ALPHA * x + FFN_RES * swiglu_ffn(x, ffn2_w_gu[i], ffn2_w_d[i]), ln_g[i, 2], ln_b[i, 2])
    return x, jnp.stack(new_gla, 1), jnp.stack(new_gdn, 1), jnp.stack(new_conv, 1)


def setup_inputs(seed: int = 0) -> dict:
    key = jax.random.key(seed)
    ks = jax.random.split(key, 24)
    f32 = jnp.float32

    def nrm(k, shape, scale):
        return jax.random.normal(k, shape, f32) * scale

    dt = jnp.exp(jax.random.uniform(ks[22], (N_GDN_LAYERS, GDN_V_HEADS), f32, math.log(1e-3), math.log(1e-1)))
    return {
        'x_prompt': nrm(ks[0], (BATCH, SEQ, D_MODEL), 1.0),
        'x_sample': nrm(ks[1], (DEC_BATCH, DEC_SEQ, D_MODEL), 1.0),
        'state_gla': nrm(ks[2], (DEC_BATCH, N_GLA_LAYERS, GLA_HEADS, GLA_DK, GLA_DV), 0.5),
        'state_gdn': nrm(ks[3], (DEC_BATCH, N_GDN_LAYERS, GDN_V_HEADS, GDN_HEAD_DIM, GDN_HEAD_DIM), 0.5),
        'state_gdn_conv': nrm(ks[4], (DEC_BATCH, N_GDN_LAYERS, GDN_CONV - 1, GDN_CONV_DIM), 1.0),
        'ln_g': 1.0 + nrm(ks[5], (DEPTH, 3, D_MODEL), 0.02),
        'ln_b': nrm(ks[6], (DEPTH, 3, D_MODEL), 0.02),
        'ffn1_w_gu': nrm(ks[7], (DEPTH, D_MODEL, 2 * D_FF), D_MODEL ** -0.5),
        'ffn1_w_d': nrm(ks[8], (DEPTH, D_FF, D_MODEL), D_FF ** -0.5 * DEEPNORM_BETA),
        'ffn2_w_gu': nrm(ks[9], (DEPTH, D_MODEL, 2 * D_FF), D_MODEL ** -0.5),
        'ffn2_w_d': nrm(ks[10], (DEPTH, D_FF, D_MODEL), D_FF ** -0.5 * DEEPNORM_BETA),
        'gla_w_in': nrm(ks[11], (N_GLA_LAYERS, D_MODEL, GLA_IN), D_MODEL ** -0.5),
        'gla_w_a1': nrm(ks[12], (N_GLA_LAYERS, D_MODEL, GLA_GATE_RANK), D_MODEL ** -0.5),
        'gla_w_a2': nrm(ks[13], (N_GLA_LAYERS, GLA_GATE_RANK, GLA_QK_W), GLA_GATE_RANK ** -0.5),
        'gla_b_a': nrm(ks[14], (N_GLA_LAYERS, GLA_QK_W), 0.1),
        'gla_norm_g': 1.0 + nrm(ks[15], (N_GLA_LAYERS, GLA_DV), 0.02),
        'gla_w_o': nrm(ks[16], (N_GLA_LAYERS, GLA_V_W, D_MODEL), GLA_V_W ** -0.5 * DEEPNORM_BETA),
        'gdn_w_in': nrm(ks[17], (N_GDN_LAYERS, D_MODEL, GDN_IN), D_MODEL ** -0.5),
        'gdn_conv_w': nrm(ks[18], (N_GDN_LAYERS, GDN_CONV, GDN_CONV_DIM), GDN_CONV ** -0.5),
        'gdn_a_log': jnp.log(jax.random.uniform(ks[19], (N_GDN_LAYERS, GDN_V_HEADS), f32, 1.0, 16.0)),
        'gdn_dt_bias': dt + jnp.log(-jnp.expm1(-dt)),
        'gdn_norm_g': 1.0 + nrm(ks[20], (N_GDN_LAYERS, GDN_HEAD_DIM), 0.02),
        'gdn_w_o': nrm(ks[21], (N_GDN_LAYERS, GDN_VAL_W, D_MODEL), GDN_VAL_W ** -0.5 * DEEPNORM_BETA),
    }


def reference(x_prompt, x_sample, state_gla, state_gdn, state_gdn_conv, ln_g, ln_b,
              ffn1_w_gu, ffn1_w_d, ffn2_w_gu, ffn2_w_d,
              gla_w_in, gla_w_a1, gla_w_a2, gla_b_a, gla_norm_g, gla_w_o,
              gdn_w_in, gdn_conv_w, gdn_a_log, gdn_dt_bias, gdn_norm_g, gdn_w_o):
    f32 = jnp.float32
    nb = x_prompt.shape[0]
    zero_gla = jnp.zeros((nb, N_GLA_LAYERS, GLA_HEADS, GLA_DK, GLA_DV), f32)
    zero_gdn = jnp.zeros((nb, N_GDN_LAYERS, GDN_V_HEADS, GDN_HEAD_DIM, GDN_HEAD_DIM), f32)
    zero_conv = jnp.zeros((nb, N_GDN_LAYERS, GDN_CONV - 1, GDN_CONV_DIM), x_prompt.dtype)
    y_prompt, gla_p, gdn_p, conv_p = trunk(
        x_prompt, zero_gla, zero_gdn, zero_conv, ln_g, ln_b, ffn1_w_gu, ffn1_w_d, ffn2_w_gu, ffn2_w_d,
        gla_w_in, gla_w_a1, gla_w_a2, gla_b_a, gla_norm_g, gla_w_o,
        gdn_w_in, gdn_conv_w, gdn_a_log, gdn_dt_bias, gdn_norm_g, gdn_w_o)
    y_sample, gla_s, gdn_s, conv_s = trunk(
        x_sample, state_gla, state_gdn, state_gdn_conv, ln_g, ln_b, ffn1_w_gu, ffn1_w_d, ffn2_w_gu, ffn2_w_d,
        gla_w_in, gla_w_a1, gla_w_a2, gla_b_a, gla_norm_g, gla_w_o,
        gdn_w_in, gdn_conv_w, gdn_a_log, gdn_dt_bias, gdn_norm_g, gdn_w_o)
    return (y_prompt, y_sample, gla_p, gdn_p, conv_p, gla_s, gdn_s, conv_s)
```

```python
import functools
import math

import jax
import jax.numpy as jnp
from jax import lax
from jax.experimental import pallas as pl
from jax.experimental.pallas import tpu as pltpu

F32 = jnp.float32
BF16 = jnp.bfloat16

LN_EPS = 1e-5
RMS_EPS = 1e-6
L2_EPS = 1e-6
FFN_RES = 0.5
GLA_GATE_TAU = 16.0
GLA_CHUNK = 64
CONV_TAPS = 4

LANES = 128
SUBLANES = 8
VMEM_LIMIT_BYTES = 56 * 1024 * 1024
TOKEN_TILE = 512
MATMUL_N_CHUNK = 1024


def _cparams(n_axes):
    return pltpu.CompilerParams(
        dimension_semantics=("arbitrary",) * n_axes, vmem_limit_bytes=VMEM_LIMIT_BYTES)


def _resident(shape):
    zeros = (0,) * len(shape)
    return pl.BlockSpec(shape, lambda *_: zeros, pipeline_mode=pl.Buffered(1))


def _dot(a, b):
    return jnp.dot(a.astype(BF16), b.astype(BF16), preferred_element_type=F32)


def _dot_nt(a, b):
    return lax.dot_general(a.astype(BF16), b.astype(BF16), (((1,), (1,)), ((), ())),
                           preferred_element_type=F32)


def _silu(x):
    return x * jax.nn.sigmoid(x)


def _softplus(x):
    return jnp.maximum(x, 0.0) + jnp.log1p(jnp.exp(-jnp.abs(x)))


def _layer_norm_rows(y, g, b):
    mu = jnp.mean(y, axis=-1, keepdims=True)
    d = y - mu
    var = jnp.mean(d * d, axis=-1, keepdims=True)
    return d * lax.rsqrt(var + LN_EPS) * g + b


def _rms_norm_rows(o, g):
    return o * lax.rsqrt(jnp.mean(o * o, axis=-1, keepdims=True) + RMS_EPS) * g


def _l2_norm_rows(x):
    return x * lax.rsqrt(jnp.sum(x * x, axis=-1, keepdims=True) + L2_EPS)


def _col_chunks(n, step=MATMUL_N_CHUNK):
    return [(c, min(c + step, n)) for c in range(0, n, step)]


def _ffn_ln_body(x_ref, wgu_ref, wd_ref, g_ref, b_ref, o_ref, h_ref, *, d_ff, alpha):
    x = x_ref[...]
    xb = x.astype(BF16)
    for c0, c1 in _col_chunks(d_ff):
        gate = jnp.dot(xb, wgu_ref[:, c0:c1], preferred_element_type=F32)
        up = jnp.dot(xb, wgu_ref[:, d_ff + c0:d_ff + c1], preferred_element_type=F32)
        h_ref[:, c0:c1] = (_silu(gate) * up).astype(BF16)
    y = jnp.dot(h_ref[...], wd_ref[...], preferred_element_type=F32)
    o_ref[...] = _layer_norm_rows(alpha * x + FFN_RES * y, g_ref[...], b_ref[...])


def _ffn_ln(x, w_gu, w_d, g, b, *, tm, alpha):
    n, d = x.shape
    d_ff = w_d.shape[0]
    tile = pl.BlockSpec((tm, d), lambda i: (i, 0))
    return pl.pallas_call(
        functools.partial(_ffn_ln_body, d_ff=d_ff, alpha=alpha),
        grid=(n // tm,),
        in_specs=[tile, _resident(w_gu.shape), _resident(w_d.shape),
                  _resident((1, d)), _resident((1, d))],
        out_specs=tile,
        out_shape=jax.ShapeDtypeStruct((n, d), F32),
        scratch_shapes=[pltpu.VMEM((tm, d_ff), BF16)],
        compiler_params=_cparams(1),
        name="ffn_ln",
    )(x, w_gu, w_d, g.reshape(1, d), b.reshape(1, d))


def _out_ln_body(x_ref, ogp_ref, ogs_ref, wo_ref, g_ref, b_ref, o_ref, *, n_prompt_tiles, alpha):
    is_prompt = pl.program_id(0) < n_prompt_tiles
    og = jnp.where(is_prompt, ogp_ref[...], ogs_ref[...])
    y = jnp.dot(og, wo_ref[...], preferred_element_type=F32)
    o_ref[...] = _layer_norm_rows(alpha * x_ref[...] + y, g_ref[...], b_ref[...])


def _out_ln(x, og_prompt, og_sample, w_o, g, b, *, tm, alpha):
    n, d = x.shape
    v = w_o.shape[0]
    ntp = og_prompt.shape[0] // tm
    tile = pl.BlockSpec((tm, d), lambda i: (i, 0))
    return pl.pallas_call(
        functools.partial(_out_ln_body, n_prompt_tiles=ntp, alpha=alpha),
        grid=(n // tm,),
        in_specs=[tile,
                  pl.BlockSpec((tm, v), lambda i: (jnp.minimum(i, ntp - 1), 0)),
                  pl.BlockSpec((tm, v), lambda i: (jnp.maximum(i - ntp, 0), 0)),
                  _resident(w_o.shape), _resident((1, d)), _resident((1, d))],
        out_specs=tile,
        out_shape=jax.ShapeDtypeStruct((n, d), F32),
        compiler_params=_cparams(1),
        name="out_ln",
    )(x, og_prompt, og_sample, w_o, g.reshape(1, d), b.reshape(1, d))


def _gla_proj_body(x_ref, win_ref, wa1_ref, wa2_ref, ba_ref, p_ref, la_ref, *, qk_w, q_scale):
    xb = x_ref[...].astype(BF16)
    n_out = win_ref.shape[1]
    p_ref[:, 0:qk_w] = jnp.dot(xb, win_ref[:, 0:qk_w], preferred_element_type=F32) * q_scale
    rest = [(qk_w, 2 * qk_w)] + [(a + 2 * qk_w, b + 2 * qk_w)
                                  for a, b in _col_chunks(n_out - 2 * qk_w)]
    for c0, c1 in rest:
        p_ref[:, c0:c1] = jnp.dot(xb, win_ref[:, c0:c1], preferred_element_type=F32)
    low = jnp.dot(xb, wa1_ref[...], preferred_element_type=F32)
    logits = _dot(low, wa2_ref[...]) + ba_ref[...]
    log_sig = jnp.minimum(logits, 0.0) - jnp.log1p(jnp.exp(-jnp.abs(logits)))
    la_ref[...] = log_sig / GLA_GATE_TAU


def _gla_proj(x, w_in, w_a1, w_a2, b_a, *, tm, qk_w, q_scale):
    n, d = x.shape
    n_out = w_in.shape[1]
    return pl.pallas_call(
        functools.partial(_gla_proj_body, qk_w=qk_w, q_scale=q_scale),
        grid=(n // tm,),
        in_specs=[pl.BlockSpec((tm, d), lambda i: (i, 0)), _resident(w_in.shape),
                  _resident(w_a1.shape), _resident(w_a2.shape), _resident((1, qk_w))],
        out_specs=[pl.BlockSpec((tm, n_out), lambda i: (i, 0)),
                   pl.BlockSpec((tm, qk_w), lambda i: (i, 0))],
        out_shape=[jax.ShapeDtypeStruct((n, n_out), F32), jax.ShapeDtypeStruct((n, qk_w), F32)],
        compiler_params=_cparams(1),
        name="gla_proj",
    )(x, w_in, w_a1, w_a2, b_a.reshape(1, qk_w))


def _gdn_proj_body(x_ref, wmain_ref, wab_ref, alog_ref, dtb_ref, p_ref, gb_ref):
    xb = x_ref[...].astype(BF16)
    for c0, c1 in _col_chunks(wmain_ref.shape[1]):
        p_ref[:, c0:c1] = jnp.dot(xb, wmain_ref[:, c0:c1], preferred_element_type=F32)
    ab = lax.dot_general(wab_ref[...], xb, (((1,), (1,)), ((), ())), preferred_element_type=F32)
    kind = _iota2(ab.shape, 0) & (SUBLANES - 1)
    g = -jnp.exp(alog_ref[...]) * _softplus(ab + dtb_ref[...])
    beta = jax.nn.sigmoid(ab)
    gb_ref[...] = jnp.where(kind < 2, g, jnp.where(kind < 4, beta, 0.0))


def _gdn_proj(x, w_main, w_ab_t, a_log_rows, dt_bias_rows, *, tm):
    n, d = x.shape
    n_out = w_main.shape[1]
    n_rows = w_ab_t.shape[0]
    return pl.pallas_call(
        _gdn_proj_body,
        grid=(n // tm,),
        in_specs=[pl.BlockSpec((tm, d), lambda i: (i, 0)), _resident(w_main.shape),
                  _resident(w_ab_t.shape), _resident((n_rows, 1)), _resident((n_rows, 1))],
        out_specs=[pl.BlockSpec((tm, n_out), lambda i: (i, 0)),
                   pl.BlockSpec((n_rows, tm), lambda i: (0, i))],
        out_shape=[jax.ShapeDtypeStruct((n, n_out), F32), jax.ShapeDtypeStruct((n_rows, n), F32)],
        compiler_params=_cparams(1),
        name="gdn_proj",
    )(x, w_main, w_ab_t, a_log_rows, dt_bias_rows)


def _iota2(shape, dim):
    return lax.broadcasted_iota(jnp.int32, shape, dim)


def _block_of(idx, block):
    return idx >> (block.bit_length() - 1)


def _block_cumsum(x, block, axis):
    pos = _iota2(x.shape, axis) & (block - 1)
    shift = 1
    while shift < block:
        x = x + jnp.where(pos >= shift, pltpu.roll(x, shift, axis), 0.0)
        shift *= 2
    return x


def _col_form(row):
    return jnp.transpose(jnp.broadcast_to(row, (LANES, LANES)))


def _segments(block):
    return [(r, r + block) for r in range(0, LANES, block)]


def _last_row_of_segments(x, block):
    return jnp.concatenate(
        [jnp.broadcast_to(x[r1 - 1:r1, :], (r1 - r0, x.shape[1])) for r0, r1 in _segments(block)],
        axis=0)


def _rows_only(x, r0, r1):
    rows = _iota2(x.shape, 0)
    return jnp.where((rows >= r0) & (rows < r1), x, 0.0)


def _gla_tile(q, k, v, la, r, ng, block, states, chained):
    row = _iota2((LANES, LANES), 0)
    col = _iota2((LANES, LANES), 1)
    causal = (_block_of(row, block) == _block_of(col, block)) & (row >= col)
    bcum = _block_cumsum(la, block, 0)
    q_dec = q * jnp.exp(bcum)
    k_inv = k * jnp.exp(-bcum)
    att = jnp.where(causal, _dot_nt(q_dec, k_inv), 0.0)
    intra = _dot(att, v)
    k_end_t = jnp.transpose(k * jnp.exp(_last_row_of_segments(bcum, block) - bcum))
    lane_reps = v.shape[1] // LANES
    outs, leaving = [], []
    state = states[0]
    for s, (r0, r1) in enumerate(_segments(block)):
        if not chained:
            state = states[s]
        outs.append(intra[r0:r1] + _dot(q_dec[r0:r1], state))
        decay = jnp.concatenate([_col_form(jnp.exp(bcum[r1 - 1:r1, :]))] * lane_reps, axis=1)
        state = decay * state + _dot(k_end_t, _rows_only(v, r0, r1))
        if not chained:
            leaving.append(state)
    if chained:
        leaving.append(state)
    o = _rms_norm_rows(jnp.concatenate(outs, axis=0), ng)
    return o * _silu(r), leaving


def _gla_prompt_body(q_ref, k_ref, v_ref, r_ref, la_ref, ng_ref, og_ref, so_ref, s_ref, *, n_tiles):
    s_ref[...] = jnp.zeros_like(s_ref)

    def tile(t, carry):
        rows = pl.ds(pl.multiple_of(t * LANES, LANES), LANES)
        o, leaving = _gla_tile(q_ref[rows, :], k_ref[rows, :], v_ref[rows, :], la_ref[rows, :],
                               r_ref[rows, :], ng_ref[...], GLA_CHUNK, [s_ref[...]], True)
        og_ref[rows, :] = o.astype(BF16)
        s_ref[...] = leaving[0]
        return carry

    lax.fori_loop(0, n_tiles, tile, 0)
    so_ref[...] = s_ref[...]


def _gla_sample_body(q_ref, k_ref, v_ref, r_ref, la_ref, ng_ref, s_ref, og_ref, so_ref, *, block):
    n_seq = LANES // block
    o, leaving = _gla_tile(q_ref[...], k_ref[...], v_ref[...], la_ref[...], r_ref[...], ng_ref[...],
                           block, [s_ref[i] for i in range(n_seq)], False)
    og_ref[...] = o.astype(BF16)
    for i in range(n_seq):
        so_ref[i] = leaving[i]


def _gla_recurrence(p, la, norm_g, state_in, layer, *, n_prompt, batch, heads, dk, dv, dec_seq):
    n = p.shape[0]
    t = n_prompt // batch
    qk_w, v_w = heads * dk, heads * dv
    v_blk0 = (2 * qk_w) // dv
    r_blk0 = (2 * qk_w + v_w) // dv
    ng = norm_g.reshape(1, dv)

    og_p, s_p = pl.pallas_call(
        functools.partial(_gla_prompt_body, n_tiles=t // LANES),
        grid=(batch, heads),
        in_specs=[pl.BlockSpec((t, dk), lambda b, h: (b, h)),
                  pl.BlockSpec((t, dk), lambda b, h: (b, heads + h)),
                  pl.BlockSpec((t, dv), lambda b, h: (b, v_blk0 + h)),
                  pl.BlockSpec((t, dv), lambda b, h: (b, r_blk0 + h)),
                  pl.BlockSpec((t, dk), lambda b, h: (b, h)),
                  _resident((1, dv))],
        out_specs=[pl.BlockSpec((t, dv), lambda b, h: (b, h)),
                   pl.BlockSpec((None, None, dk, dv), lambda b, h: (b, h, 0, 0))],
        out_shape=[jax.ShapeDtypeStruct((n_prompt, v_w), BF16),
                   jax.ShapeDtypeStruct((batch, heads, dk, dv), F32)],
        scratch_shapes=[pltpu.VMEM((dk, dv), F32)],
        compiler_params=_cparams(2),
        name="gla_prompt",
    )(p, p, p, p, la, ng)

    n_sample = n - n_prompt
    dec_batch = n_sample // dec_seq
    seqs = LANES // dec_seq
    off = n_prompt // LANES
    og_s, s_s = pl.pallas_call(
        functools.partial(_gla_sample_body, block=dec_seq),
        grid=(n_sample // LANES, heads),
        in_specs=[pl.BlockSpec((LANES, dk), lambda g, h: (off + g, h)),
                  pl.BlockSpec((LANES, dk), lambda g, h: (off + g, heads + h)),
                  pl.BlockSpec((LANES, dv), lambda g, h: (off + g, v_blk0 + h)),
                  pl.BlockSpec((LANES, dv), lambda g, h: (off + g, r_blk0 + h)),
                  pl.BlockSpec((LANES, dk), lambda g, h: (off + g, h)),
                  _resident((1, dv)),
                  pl.BlockSpec((seqs, None, None, dk, dv), lambda g, h: (g, layer, h, 0, 0))],
        out_specs=[pl.BlockSpec((LANES, dv), lambda g, h: (g, h)),
                   pl.BlockSpec((seqs, None, dk, dv), lambda g, h: (g, h, 0, 0))],
        out_shape=[jax.ShapeDtypeStruct((n_sample, v_w), BF16),
                   jax.ShapeDtypeStruct((dec_batch, heads, dk, dv), F32)],
        compiler_params=_cparams(2),
        name="gla_sample",
    )(p, p, p, p, la, ng, state_in)
    return og_p, og_s, s_p, s_s


def _causal_conv(prev8, x, w):
    n = x.shape[0]
    xs = jnp.concatenate([prev8, x], axis=0)
    acc = xs[SUBLANES:, :] * w[CONV_TAPS - 1:CONV_TAPS, :]
    for j in range(CONV_TAPS - 1):
        shifted = pltpu.roll(xs, CONV_TAPS - 1 - j, 0)
        acc = acc + shifted[SUBLANES:SUBLANES + n, :] * w[j:j + 1, :]
    return acc


def _unit_lower_inverse(a, block):
    row = _iota2(a.shape, 0)
    col = _iota2(a.shape, 1)
    x = jnp.where(row == col, 1.0, 0.0) - jnp.where(_block_of(row, 2) == _block_of(col, 2), a, 0.0)
    size = 2
    while size < block:
        pair = _block_of(row, 2 * size) == _block_of(col, 2 * size)
        off_diag = jnp.where(pair & (_block_of(row, size) != _block_of(col, size)), a, 0.0)
        x = x - _dot(_dot(x, off_diag), x)
        size *= 2
    return x


def _gdn_tile(q, k, v, z, gb, ng, block, states, hd):
    row = _iota2((LANES, LANES), 0)
    col = _iota2((LANES, LANES), 1)
    same = _block_of(row, block) == _block_of(col, block)
    incl = same & (row >= col)
    strict = same & (row > col)
    kk = _dot_nt(k, k)
    qk = _dot_nt(q, k)
    gcum = _block_cumsum(gb, block, 1)
    segs = _segments(block)
    outs = []
    leaving = [[None, None] for _ in segs]
    for e in range(2):
        g_row = gcum[e:e + 1, :]
        g_col = _col_form(g_row)
        b_col = _col_form(gb[2 + e:3 + e, :])
        decay = jnp.where(incl, jnp.exp(g_col - jnp.broadcast_to(g_row, (LANES, LANES))), 0.0)
        t_inv = _unit_lower_inverse(jnp.where(strict, b_col * kk * decay, 0.0), block)
        eg = jnp.exp(g_col)
        v_e = v[:, e * hd:(e + 1) * hd]
        wu = _dot(t_inv, jnp.concatenate([k * (b_col * eg), v_e * b_col], axis=1))
        w, u = wu[:, :hd], wu[:, hd:]
        q_dec = q * eg
        att = qk * decay
        g_last = _last_row_of_segments(g_col, block)
        k_end_t = jnp.transpose(k * jnp.exp(g_last - g_col))
        o_rows = []
        for s, (r0, r1) in enumerate(segs):
            state = states[s][e]
            v_new = u[r0:r1] - _dot(w[r0:r1], state)
            pads = [jnp.zeros((r0, hd), F32)] * (r0 > 0) + [v_new] + \
                   [jnp.zeros((LANES - r1, hd), F32)] * (r1 < LANES)
            v_new_full = jnp.concatenate(pads, axis=0) if len(pads) > 1 else v_new
            o_rows.append(_dot(q_dec[r0:r1], state) + _dot(att[r0:r1], v_new_full))
            leaving[s][e] = (jnp.exp(g_col[r1 - 1:r1, :]) * state + _dot(k_end_t, v_new_full))
        o = jnp.concatenate(o_rows, axis=0) if len(o_rows) > 1 else o_rows[0]
        outs.append(_rms_norm_rows(o, ng) * _silu(z[:, e * hd:(e + 1) * hd]))
    return jnp.concatenate(outs, axis=1), leaving


def _gdn_prompt_body(q_ref, k_ref, v_ref, z_ref, gb_ref, wq_ref, wk_ref, wv_ref, ng_ref,
                     og_ref, so_ref, s_ref, *, n_tiles, q_scale, hd):
    s_ref[...] = jnp.zeros_like(s_ref)

    def tile(t, carry):
        r0 = pl.multiple_of(t * LANES, LANES)
        rows = pl.ds(r0, LANES)
        prev = pl.ds(pl.multiple_of(jnp.maximum(r0 - SUBLANES, 0), SUBLANES), SUBLANES)

        def conv(ref, w_ref):
            prev8 = jnp.where(t > 0, ref[prev, :], 0.0)
            return _silu(_causal_conv(prev8, ref[rows, :], w_ref[...]))

        q = _l2_norm_rows(conv(q_ref, wq_ref)) * q_scale
        k = _l2_norm_rows(conv(k_ref, wk_ref))
        v = conv(v_ref, wv_ref)
        o, leaving = _gdn_tile(q, k, v, z_ref[rows, :], gb_ref[:, rows], ng_ref[...], LANES,
                               [[s_ref[0], s_ref[1]]], hd)
        og_ref[rows, :] = o.astype(BF16)
        s_ref[0] = leaving[0][0]
        s_ref[1] = leaving[0][1]
        return carry

    lax.fori_loop(0, n_tiles, tile, 0)
    so_ref[...] = s_ref[...]


def _gdn_sample_body(q_ref, k_ref, v_ref, z_ref, gb_ref, wq_ref, wk_ref, wv_ref, ng_ref,
                     hq_ref, hk_ref, hv_ref, s_ref, og_ref, so_ref, *, block, q_scale, hd):
    segs = _segments(block)

    def conv(ref, hist_ref, w_ref):
        x = ref[...]
        w = w_ref[...]
        return _silu(jnp.concatenate(
            [_causal_conv(hist_ref[s], x[r0:r1], w) for s, (r0, r1) in enumerate(segs)], axis=0))

    q = _l2_norm_rows(conv(q_ref, hq_ref, wq_ref)) * q_scale
    k = _l2_norm_rows(conv(k_ref, hk_ref, wk_ref))
    v = conv(v_ref, hv_ref, wv_ref)
    states = [[s_ref[s, 0], s_ref[s, 1]] for s in range(len(segs))]
    o, leaving = _gdn_tile(q, k, v, z_ref[...], gb_ref[...], ng_ref[...], block, states, hd)
    og_ref[...] = o.astype(BF16)
    for s in range(len(segs)):
        so_ref[s, 0] = leaving[s][0]
        so_ref[s, 1] = leaving[s][1]


def _gdn_recurrence(p, gb, conv_w, norm_g, hist8, state_in, layer, *,
                    n_prompt, batch, hq, hd, dec_seq):
    n = p.shape[0]
    t = n_prompt // batch
    key_w = hq * hd
    val_w = 2 * key_w
    q_scale = hd ** -0.5
    v_blk0 = (2 * key_w) // (2 * hd)
    z_blk0 = (2 * key_w + val_w) // (2 * hd)
    ng = norm_g.reshape(1, hd)
    w_specs = [pl.BlockSpec((CONV_TAPS, hd), lambda a, h: (0, h)),
               pl.BlockSpec((CONV_TAPS, hd), lambda a, h: (0, hq + h)),
               pl.BlockSpec((CONV_TAPS, 2 * hd), lambda a, h: (0, v_blk0 + h))]

    og_p, s_p = pl.pallas_call(
        functools.partial(_gdn_prompt_body, n_tiles=t // LANES, q_scale=q_scale, hd=hd),
        grid=(batch, hq),
        in_specs=[pl.BlockSpec((t, hd), lambda b, h: (b, h)),
                  pl.BlockSpec((t, hd), lambda b, h: (b, hq + h)),
                  pl.BlockSpec((t, 2 * hd), lambda b, h: (b, v_blk0 + h)),
                  pl.BlockSpec((t, 2 * hd), lambda b, h: (b, z_blk0 + h)),
                  pl.BlockSpec((SUBLANES, t), lambda b, h: (h, b))] + w_specs + [_resident((1, hd))],
        out_specs=[pl.BlockSpec((t, 2 * hd), lambda b, h: (b, h)),
                   pl.BlockSpec((None, 2, hd, hd), lambda b, h: (b, h, 0, 0))],
        out_shape=[jax.ShapeDtypeStruct((n_prompt, val_w), BF16),
                   jax.ShapeDtypeStruct((batch, 2 * hq, hd, hd), F32)],
        scratch_shapes=[pltpu.VMEM((2, hd, hd), F32)],
        compiler_params=_cparams(2),
        name="gdn_prompt",
    )(p, p, p, p, gb, conv_w, conv_w, conv_w, ng)

    n_sample = n - n_prompt
    dec_batch = n_sample // dec_seq
    seqs = LANES // dec_seq
    off = n_prompt // LANES
    og_s, s_s = pl.pallas_call(
        functools.partial(_gdn_sample_body, block=dec_seq, q_scale=q_scale, hd=hd),
        grid=(n_sample // LANES, hq),
        in_specs=[pl.BlockSpec((LANES, hd), lambda g, h: (off + g, h)),
                  pl.BlockSpec((LANES, hd), lambda g, h: (off + g, hq + h)),
                  pl.BlockSpec((LANES, 2 * hd), lambda g, h: (off + g, v_blk0 + h)),
                  pl.BlockSpec((LANES, 2 * hd), lambda g, h: (off + g, z_blk0 + h)),
                  pl.BlockSpec((SUBLANES, LANES), lambda g, h: (h, off + g))] + w_specs + [
                  _resident((1, hd)),
                  pl.BlockSpec((seqs, SUBLANES, hd), lambda g, h: (g, 0, h)),
                  pl.BlockSpec((seqs, SUBLANES, hd), lambda g, h: (g, 0, hq + h)),
                  pl.BlockSpec((seqs, SUBLANES, 2 * hd), lambda g, h: (g, 0, v_blk0 + h)),
                  pl.BlockSpec((seqs, None, 2, hd, hd), lambda g, h: (g, layer, h, 0, 0))],
        out_specs=[pl.BlockSpec((LANES, 2 * hd), lambda g, h: (g, h)),
                   pl.BlockSpec((seqs, 2, hd, hd), lambda g, h: (g, h, 0, 0))],
        out_shape=[jax.ShapeDtypeStruct((n_sample, val_w), BF16),
                   jax.ShapeDtypeStruct((dec_batch, 2 * hq, hd, hd), F32)],
        compiler_params=_cparams(2),
        name="gdn_sample",
    )(p, p, p, p, gb, conv_w, conv_w, conv_w, ng, hist8, hist8, hist8, state_in)
    return og_p, og_s, s_p, s_s


def _gdn_gate_rows(w_in, a_log, dt_bias, n_main, hq):
    d = w_in.shape[0]
    hv = a_log.shape[0]
    rep = hv // hq
    pad = SUBLANES - 2 * rep
    wa = w_in[:, n_main:n_main + hv].T.reshape(hq, rep, d)
    wb = w_in[:, n_main + hv:n_main + 2 * hv].T.reshape(hq, rep, d)
    w_ab_t = jnp.concatenate([wa, wb, jnp.zeros((hq, pad, d), w_in.dtype)], axis=1)
    zeros = jnp.zeros((hq, SUBLANES - rep), F32)
    alog = jnp.concatenate([a_log.reshape(hq, rep), zeros], axis=1).reshape(hq * SUBLANES, 1)
    dtb = jnp.concatenate([dt_bias.reshape(hq, rep), zeros], axis=1).reshape(hq * SUBLANES, 1)
    return w_ab_t.reshape(hq * SUBLANES, d).astype(BF16), alog, dtb


def kernel(x_prompt, x_sample, state_gla, state_gdn, state_gdn_conv, ln_g, ln_b, ffn1_w_gu, ffn1_w_d, ffn2_w_gu, ffn2_w_d, gla_w_in, gla_w_a1, gla_w_a2, gla_b_a, gla_norm_g, gla_w_o, gdn_w_in, gdn_conv_w, gdn_a_log, gdn_dt_bias, gdn_norm_g, gdn_w_o):
    batch, t, d = x_prompt.shape
    dec_batch, dec_seq, _ = x_sample.shape
    depth = ln_g.shape[0]
    alpha = (2.0 * depth) ** 0.25
    n_prompt, n_sample = batch * t, dec_batch * dec_seq
    tm = math.gcd(TOKEN_TILE, math.gcd(n_prompt, n_sample))
    assert t % LANES == 0 and t % GLA_CHUNK == 0 and n_sample % LANES == 0 and LANES % dec_seq == 0
    assert tm % LANES == 0

    gla_heads, gla_dk, gla_dv = state_gla.shape[2:]
    gla_qk_w = gla_heads * gla_dk
    hv, hd = state_gdn.shape[2:4]
    conv_dim = gdn_conv_w.shape[2]
    val_w = hv * hd
    hq = (conv_dim - val_w) // (2 * hd)
    assert hv == 2 * hq, "the gated-delta kernel pairs two v heads with each q/k head"
    n_main = conv_dim + val_w

    x = jnp.concatenate([x_prompt.reshape(n_prompt, d), x_sample.reshape(n_sample, d)], axis=0)
    new_gla_p, new_gla_s, new_gdn_p, new_gdn_s, new_conv_p, new_conv_s = [], [], [], [], [], []
    for i in range(depth):
        x = _ffn_ln(x, ffn1_w_gu[i].astype(BF16), ffn1_w_d[i].astype(BF16), ln_g[i, 0], ln_b[i, 0],
                    tm=tm, alpha=alpha)
        j = i // 2
        if i % 2 == 0:
            rank = gla_w_a1.shape[2]
            w_a1 = jnp.pad(gla_w_a1[j], ((0, 0), (0, LANES - rank))).astype(BF16)
            w_a2 = jnp.pad(gla_w_a2[j], ((0, LANES - rank), (0, 0))).astype(BF16)
            p, la = _gla_proj(x, gla_w_in[j].astype(BF16), w_a1, w_a2, gla_b_a[j],
                              tm=tm, qk_w=gla_qk_w, q_scale=gla_dk ** -0.5)
            og_p, og_s, s_p, s_s = _gla_recurrence(
                p, la, gla_norm_g[j], state_gla, j, n_prompt=n_prompt, batch=batch,
                heads=gla_heads, dk=gla_dk, dv=gla_dv, dec_seq=dec_seq)
            new_gla_p.append(s_p)
            new_gla_s.append(s_s)
            w_o = gla_w_o[j]
        else:
            w_ab_t, alog_rows, dtb_rows = _gdn_gate_rows(gdn_w_in[j], gdn_a_log[j], gdn_dt_bias[j],
                                                         n_main, hq)
            p, gb = _gdn_proj(x, gdn_w_in[j][:, :n_main].astype(BF16), w_ab_t, alog_rows, dtb_rows,
                              tm=tm)
            hist8 = jnp.pad(state_gdn_conv[:, j],
                            ((0, 0), (SUBLANES - (CONV_TAPS - 1), 0), (0, 0)))
            og_p, og_s, s_p, s_s = _gdn_recurrence(
                p, gb, gdn_conv_w[j], gdn_norm_g[j], hist8, state_gdn, j, n_prompt=n_prompt,
                batch=batch, hq=hq, hd=hd, dec_seq=dec_seq)
            new_gdn_p.append(s_p)
            new_gdn_s.append(s_s)
            keep = CONV_TAPS - 1
            new_conv_p.append(p[:n_prompt].reshape(batch, t, n_main)[:, t - keep:, :conv_dim])
            new_conv_s.append(p[n_prompt:].reshape(dec_batch, dec_seq, n_main)[:, dec_seq - keep:, :conv_dim])
            w_o = gdn_w_o[j]
        x = _out_ln(x, og_p, og_s, w_o.astype(BF16), ln_g[i, 1], ln_b[i, 1], tm=tm, alpha=alpha)
        x = _ffn_ln(x, ffn2_w_gu[i].astype(BF16), ffn2_w_d[i].astype(BF16), ln_g[i, 2], ln_b[i, 2],
                    tm=tm, alpha=alpha)
    return (x[:n_prompt].reshape(batch, t, d), x[n_prompt:].reshape(dec_batch, dec_seq, d),
            jnp.stack(new_gla_p, 1), jnp.stack(new_gdn_p, 1), jnp.stack(new_conv_p, 1),
            jnp.stack(new_gla_s, 1), jnp.stack(new_gdn_s, 1), jnp.stack(new_conv_s, 1))
```

```python
import functools
import math

import numpy as np

import jax
import jax.numpy as jnp
from jax import lax
from jax.experimental import pallas as pl
from jax.experimental.pallas import tpu as pltpu

F32 = jnp.float32
BF16 = jnp.bfloat16

LN_EPS = 1e-5
RMS_EPS = 1e-6
L2_EPS = 1e-6
FFN_RES = 0.5
GLA_GATE_TAU = 16.0
GLA_CHUNK = 64
CONV_TAPS = 4

LANES = 128
SUBLANES = 8
VMEM_LIMIT_BYTES = 56 * 1024 * 1024
TOKEN_TILE = 512
MATMUL_N_CHUNK = 1024
GDN_PAIRS_PER_STEP = 4
GDN_SEQ_BLOCK = 512


def _cparams(n_axes):
    return pltpu.CompilerParams(
        dimension_semantics=("arbitrary",) * n_axes, vmem_limit_bytes=VMEM_LIMIT_BYTES)


def _resident(shape):
    zeros = (0,) * len(shape)
    return pl.BlockSpec(shape, lambda *_: zeros, pipeline_mode=pl.Buffered(1))


def _dot(a, b):
    return jnp.dot(a.astype(BF16), b.astype(BF16), preferred_element_type=F32)


def _dot_nt(a, b):
    return lax.dot_general(a.astype(BF16), b.astype(BF16), (((1,), (1,)), ((), ())),
                           preferred_element_type=F32)


def _silu(x):
    return x * jax.nn.sigmoid(x)


def _softplus(x):
    return jnp.maximum(x, 0.0) + jnp.log1p(jnp.exp(-jnp.abs(x)))


def _layer_norm_rows(y, g, b):
    mu = jnp.mean(y, axis=-1, keepdims=True)
    d = y - mu
    var = jnp.mean(d * d, axis=-1, keepdims=True)
    return d * lax.rsqrt(var + LN_EPS) * g + b


def _rms_norm_rows(o, g):
    return o * lax.rsqrt(jnp.mean(o * o, axis=-1, keepdims=True) + RMS_EPS) * g


def _l2_norm_rows(x):
    return x * lax.rsqrt(jnp.sum(x * x, axis=-1, keepdims=True) + L2_EPS)


def _col_chunks(n, step=MATMUL_N_CHUNK):
    return [(c, min(c + step, n)) for c in range(0, n, step)]


def _ffn_ln_body(x_ref, wgu_ref, wd_ref, g_ref, b_ref, o_ref, h_ref, *, d_ff, alpha):
    x = x_ref[...]
    xb = x.astype(BF16)
    for c0, c1 in _col_chunks(d_ff):
        gate = jnp.dot(xb, wgu_ref[:, c0:c1], preferred_element_type=F32)
        up = jnp.dot(xb, wgu_ref[:, d_ff + c0:d_ff + c1], preferred_element_type=F32)
        h_ref[:, c0:c1] = (_silu(gate) * up).astype(BF16)
    y = jnp.dot(h_ref[...], wd_ref[...], preferred_element_type=F32)
    o_ref[...] = _layer_norm_rows(alpha * x + FFN_RES * y, g_ref[...], b_ref[...])


def _ffn_ln(x, w_gu, w_d, g, b, *, tm, alpha):
    n, d = x.shape
    d_ff = w_d.shape[0]
    tile = pl.BlockSpec((tm, d), lambda i: (i, 0))
    return pl.pallas_call(
        functools.partial(_ffn_ln_body, d_ff=d_ff, alpha=alpha),
        grid=(n // tm,),
        in_specs=[tile, _resident(w_gu.shape), _resident(w_d.shape),
                  _resident((1, d)), _resident((1, d))],
        out_specs=tile,
        out_shape=jax.ShapeDtypeStruct((n, d), F32),
        scratch_shapes=[pltpu.VMEM((tm, d_ff), BF16)],
        compiler_params=_cparams(1),
        name="ffn_ln",
    )(x, w_gu, w_d, g.reshape(1, d), b.reshape(1, d))


def _out_ln_body(x_ref, ogp_ref, ogs_ref, wo_ref, g_ref, b_ref, o_ref, *, n_prompt_tiles, alpha):
    is_prompt = pl.program_id(0) < n_prompt_tiles
    og = jnp.where(is_prompt, ogp_ref[...], ogs_ref[...])
    y = jnp.dot(og, wo_ref[...], preferred_element_type=F32)
    o_ref[...] = _layer_norm_rows(alpha * x_ref[...] + y, g_ref[...], b_ref[...])


def _out_ln(x, og_prompt, og_sample, w_o, g, b, *, tm, alpha):
    n, d = x.shape
    v = w_o.shape[0]
    ntp = og_prompt.shape[0] // tm
    tile = pl.BlockSpec((tm, d), lambda i: (i, 0))
    return pl.pallas_call(
        functools.partial(_out_ln_body, n_prompt_tiles=ntp, alpha=alpha),
        grid=(n // tm,),
        in_specs=[tile,
                  pl.BlockSpec((tm, v), lambda i: (jnp.minimum(i, ntp - 1), 0)),
                  pl.BlockSpec((tm, v), lambda i: (jnp.maximum(i - ntp, 0), 0)),
                  _resident(w_o.shape), _resident((1, d)), _resident((1, d))],
        out_specs=tile,
        out_shape=jax.ShapeDtypeStruct((n, d), F32),
        compiler_params=_cparams(1),
        name="out_ln",
    )(x, og_prompt, og_sample, w_o, g.reshape(1, d), b.reshape(1, d))


def _gla_proj_body(x_ref, win_ref, wa1_ref, wa2_ref, ba_ref, p_ref, la_ref, *, qk_w, q_scale):
    xb = x_ref[...].astype(BF16)
    n_out = win_ref.shape[1]
    p_ref[:, 0:qk_w] = jnp.dot(xb, win_ref[:, 0:qk_w], preferred_element_type=F32) * q_scale
    rest = [(qk_w, 2 * qk_w)] + [(a + 2 * qk_w, b + 2 * qk_w)
                                  for a, b in _col_chunks(n_out - 2 * qk_w)]
    for c0, c1 in rest:
        p_ref[:, c0:c1] = jnp.dot(xb, win_ref[:, c0:c1], preferred_element_type=F32)
    low = jnp.dot(xb, wa1_ref[...], preferred_element_type=F32)
    logits = _dot(low, wa2_ref[...]) + ba_ref[...]
    log_sig = jnp.minimum(logits, 0.0) - jnp.log1p(jnp.exp(-jnp.abs(logits)))
    la_ref[...] = log_sig / GLA_GATE_TAU


def _gla_proj(x, w_in, w_a1, w_a2, b_a, *, tm, qk_w, q_scale):
    n, d = x.shape
    n_out = w_in.shape[1]
    return pl.pallas_call(
        functools.partial(_gla_proj_body, qk_w=qk_w, q_scale=q_scale),
        grid=(n // tm,),
        in_specs=[pl.BlockSpec((tm, d), lambda i: (i, 0)), _resident(w_in.shape),
                  _resident(w_a1.shape), _resident(w_a2.shape), _resident((1, qk_w))],
        out_specs=[pl.BlockSpec((tm, n_out), lambda i: (i, 0)),
                   pl.BlockSpec((tm, qk_w), lambda i: (i, 0))],
        out_shape=[jax.ShapeDtypeStruct((n, n_out), F32), jax.ShapeDtypeStruct((n, qk_w), F32)],
        compiler_params=_cparams(1),
        name="gla_proj",
    )(x, w_in, w_a1, w_a2, b_a.reshape(1, qk_w))


def _gdn_proj_body(x_ref, wmain_ref, wab_ref, alog_ref, dtb_ref, p_ref, gb_ref):
    xb = x_ref[...].astype(BF16)
    for c0, c1 in _col_chunks(wmain_ref.shape[1]):
        p_ref[:, c0:c1] = jnp.dot(xb, wmain_ref[:, c0:c1], preferred_element_type=F32)
    ab = lax.dot_general(wab_ref[...], xb, (((1,), (1,)), ((), ())), preferred_element_type=F32)
    kind = _iota2(ab.shape, 0) & (SUBLANES - 1)
    g = -jnp.exp(alog_ref[...]) * _softplus(ab + dtb_ref[...])
    beta = jax.nn.sigmoid(ab)
    gb_ref[...] = jnp.where(kind < 2, g, jnp.where(kind < 4, beta, 0.0))


def _gdn_proj(x, w_main, w_ab_t, a_log_rows, dt_bias_rows, *, tm):
    n, d = x.shape
    n_out = w_main.shape[1]
    n_rows = w_ab_t.shape[0]
    return pl.pallas_call(
        _gdn_proj_body,
        grid=(n // tm,),
        in_specs=[pl.BlockSpec((tm, d), lambda i: (i, 0)), _resident(w_main.shape),
                  _resident(w_ab_t.shape), _resident((n_rows, 1)), _resident((n_rows, 1))],
        out_specs=[pl.BlockSpec((tm, n_out), lambda i: (i, 0)),
                   pl.BlockSpec((n_rows, tm), lambda i: (0, i))],
        out_shape=[jax.ShapeDtypeStruct((n, n_out), F32), jax.ShapeDtypeStruct((n_rows, n), F32)],
        compiler_params=_cparams(1),
        name="gdn_proj",
    )(x, w_main, w_ab_t, a_log_rows, dt_bias_rows)


def _iota2(shape, dim):
    return lax.broadcasted_iota(jnp.int32, shape, dim)


def _block_of(idx, block):
    return idx >> (block.bit_length() - 1)


def _block_cumsum(x, block, axis):
    pos = _iota2(x.shape, axis) & (block - 1)
    shift = 1
    while shift < block:
        x = x + jnp.where(pos >= shift, pltpu.roll(x, shift, axis), 0.0)
        shift *= 2
    return x


def _col_form(row):
    return jnp.transpose(jnp.broadcast_to(row, (LANES, LANES)))


def _segments(block):
    return [(r, r + block) for r in range(0, LANES, block)]


def _last_row_of_segments(x, block):
    return jnp.concatenate(
        [jnp.broadcast_to(x[r1 - 1:r1, :], (r1 - r0, x.shape[1])) for r0, r1 in _segments(block)],
        axis=0)


def _rows_only(x, r0, r1):
    rows = _iota2(x.shape, 0)
    return jnp.where((rows >= r0) & (rows < r1), x, 0.0)


def _gla_tile(q, k, v, la, r, ng, block, states, chained):
    row = _iota2((LANES, LANES), 0)
    col = _iota2((LANES, LANES), 1)
    causal = (_block_of(row, block) == _block_of(col, block)) & (row >= col)
    bcum = _block_cumsum(la, block, 0)
    q_dec = q * jnp.exp(bcum)
    k_inv = k * jnp.exp(-bcum)
    att = jnp.where(causal, _dot_nt(q_dec, k_inv), 0.0)
    intra = _dot(att, v)
    k_end_t = jnp.transpose(k * jnp.exp(_last_row_of_segments(bcum, block) - bcum))
    lane_reps = v.shape[1] // LANES
    outs, leaving = [], []
    state = states[0]
    for s, (r0, r1) in enumerate(_segments(block)):
        if not chained:
            state = states[s]
        outs.append(intra[r0:r1] + _dot(q_dec[r0:r1], state))
        decay = jnp.concatenate([_col_form(jnp.exp(bcum[r1 - 1:r1, :]))] * lane_reps, axis=1)
        state = decay * state + _dot(k_end_t, _rows_only(v, r0, r1))
        if not chained:
            leaving.append(state)
    if chained:
        leaving.append(state)
    o = _rms_norm_rows(jnp.concatenate(outs, axis=0), ng)
    return o * _silu(r), leaving


def _gla_prompt_body(q_ref, k_ref, v_ref, r_ref, la_ref, ng_ref, og_ref, so_ref, s_ref, *, n_tiles):
    s_ref[...] = jnp.zeros_like(s_ref)

    def tile(t, carry):
        rows = pl.ds(pl.multiple_of(t * LANES, LANES), LANES)
        o, leaving = _gla_tile(q_ref[rows, :], k_ref[rows, :], v_ref[rows, :], la_ref[rows, :],
                               r_ref[rows, :], ng_ref[...], GLA_CHUNK, [s_ref[...]], True)
        og_ref[rows, :] = o.astype(BF16)
        s_ref[...] = leaving[0]
        return carry

    lax.fori_loop(0, n_tiles, tile, 0)
    so_ref[...] = s_ref[...]


def _gla_sample_body(q_ref, k_ref, v_ref, r_ref, la_ref, ng_ref, s_ref, og_ref, so_ref, *, block):
    n_seq = LANES // block
    o, leaving = _gla_tile(q_ref[...], k_ref[...], v_ref[...], la_ref[...], r_ref[...], ng_ref[...],
                           block, [s_ref[i] for i in range(n_seq)], False)
    og_ref[...] = o.astype(BF16)
    for i in range(n_seq):
        so_ref[i] = leaving[i]


def _gla_recurrence(p, la, norm_g, state_in, layer, *, n_prompt, batch, heads, dk, dv, dec_seq):
    n = p.shape[0]
    t = n_prompt // batch
    qk_w, v_w = heads * dk, heads * dv
    v_blk0 = (2 * qk_w) // dv
    r_blk0 = (2 * qk_w + v_w) // dv
    ng = norm_g.reshape(1, dv)

    og_p, s_p = pl.pallas_call(
        functools.partial(_gla_prompt_body, n_tiles=t // LANES),
        grid=(batch, heads),
        in_specs=[pl.BlockSpec((t, dk), lambda b, h: (b, h)),
                  pl.BlockSpec((t, dk), lambda b, h: (b, heads + h)),
                  pl.BlockSpec((t, dv), lambda b, h: (b, v_blk0 + h)),
                  pl.BlockSpec((t, dv), lambda b, h: (b, r_blk0 + h)),
                  pl.BlockSpec((t, dk), lambda b, h: (b, h)),
                  _resident((1, dv))],
        out_specs=[pl.BlockSpec((t, dv), lambda b, h: (b, h)),
                   pl.BlockSpec((None, None, dk, dv), lambda b, h: (b, h, 0, 0))],
        out_shape=[jax.ShapeDtypeStruct((n_prompt, v_w), BF16),
                   jax.ShapeDtypeStruct((batch, heads, dk, dv), F32)],
        scratch_shapes=[pltpu.VMEM((dk, dv), F32)],
        compiler_params=_cparams(2),
        name="gla_prompt",
    )(p, p, p, p, la, ng)

    n_sample = n - n_prompt
    dec_batch = n_sample // dec_seq
    seqs = LANES // dec_seq
    off = n_prompt // LANES
    og_s, s_s = pl.pallas_call(
        functools.partial(_gla_sample_body, block=dec_seq),
        grid=(n_sample // LANES, heads),
        in_specs=[pl.BlockSpec((LANES, dk), lambda g, h: (off + g, h)),
                  pl.BlockSpec((LANES, dk), lambda g, h: (off + g, heads + h)),
                  pl.BlockSpec((LANES, dv), lambda g, h: (off + g, v_blk0 + h)),
                  pl.BlockSpec((LANES, dv), lambda g, h: (off + g, r_blk0 + h)),
                  pl.BlockSpec((LANES, dk), lambda g, h: (off + g, h)),
                  _resident((1, dv)),
                  pl.BlockSpec((seqs, None, None, dk, dv), lambda g, h: (g, layer, h, 0, 0))],
        out_specs=[pl.BlockSpec((LANES, dv), lambda g, h: (g, h)),
                   pl.BlockSpec((seqs, None, dk, dv), lambda g, h: (g, h, 0, 0))],
        out_shape=[jax.ShapeDtypeStruct((n_sample, v_w), BF16),
                   jax.ShapeDtypeStruct((dec_batch, heads, dk, dv), F32)],
        compiler_params=_cparams(2),
        name="gla_sample",
    )(p, p, p, p, la, ng, state_in)
    return og_p, og_s, s_p, s_s


def _causal_conv(prev8, x, w):
    n = x.shape[0]
    xs = jnp.concatenate([prev8, x], axis=0)
    acc = xs[SUBLANES:, :] * w[CONV_TAPS - 1:CONV_TAPS, :]
    for j in range(CONV_TAPS - 1):
        shifted = pltpu.roll(xs, CONV_TAPS - 1 - j, 0)
        acc = acc + shifted[SUBLANES:SUBLANES + n, :] * w[j:j + 1, :]
    return acc


def _inverse_level_masks(block):
    idx = np.arange(LANES)
    masks = []
    size = 1
    while size < block:
        pair = (idx[:, None] // (2 * size)) == (idx[None, :] // (2 * size))
        masks.append(pair & ((idx[:, None] // size) != (idx[None, :] // size)))
        size *= 2
    return jnp.asarray(np.stack(masks).astype(np.float32))


def _unit_lower_inverses(mats, masks_ref):
    row = _iota2((LANES, LANES), 0)
    col = _iota2((LANES, LANES), 1)
    eye = jnp.where(row == col, 1.0, 0.0)
    xs = [eye - a * masks_ref[0] for a in mats]
    for level in range(1, masks_ref.shape[0]):
        mask = masks_ref[level]
        ys = [_dot(x, a * mask) for x, a in zip(xs, mats)]
        xs = [x - _dot(y, x) for x, y in zip(xs, ys)]
    return xs


def _gdn_tiles(pairs, ng, block, hd, masks_ref):
    row = _iota2((LANES, LANES), 0)
    col = _iota2((LANES, LANES), 1)
    same = _block_of(row, block) == _block_of(col, block)
    incl = same & (row >= col)
    strict = same & (row > col)
    segs = _segments(block)
    kks = [_dot_nt(k, k) for (_, k, _, _, _, _) in pairs]
    qks = [_dot_nt(q, k) for (q, k, _, _, _, _) in pairs]
    heads = []
    for (q, k, v, z, gb, states), kk, qk in zip(pairs, kks, qks):
        gcum = _block_cumsum(gb, block, 1)
        for e in range(2):
            g_row = gcum[e:e + 1, :]
            g_col = _col_form(g_row)
            b_col = _col_form(gb[2 + e:3 + e, :])
            decay = jnp.where(incl, jnp.exp(g_col - jnp.broadcast_to(g_row, (LANES, LANES))), 0.0)
            eg = jnp.exp(g_col)
            heads.append(dict(
                q_dec=q * eg, att=qk * decay, g_col=g_col,
                a=jnp.where(strict, b_col * kk * decay, 0.0),
                rhs=jnp.concatenate([k * (b_col * eg), v[:, e * hd:(e + 1) * hd] * b_col], axis=1),
                k_end=k * jnp.exp(_last_row_of_segments(g_col, block) - g_col),
                z=z[:, e * hd:(e + 1) * hd], states=[st[e] for st in states]))
    t_invs = _unit_lower_inverses([h["a"] for h in heads], masks_ref)
    wus = [_dot(t_inv, h["rhs"]) for t_inv, h in zip(t_invs, heads)]
    k_end_ts = [jnp.transpose(h["k_end"]) for h in heads]
    ws_qs = [[_dot(jnp.concatenate([wu[r0:r1, :hd], h["q_dec"][r0:r1]], axis=0), h["states"][s])
              for s, (r0, r1) in enumerate(segs)] for wu, h in zip(wus, heads)]
    v_news = []
    for wu, per_seg in zip(wus, ws_qs):
        parts = [wu[r0:r1, hd:] - x[:r1 - r0] for (r0, r1), x in zip(segs, per_seg)]
        v_news.append(jnp.concatenate(parts, axis=0) if len(parts) > 1 else parts[0])
    o_intra = [_dot(h["att"], v_new) for h, v_new in zip(heads, v_news)]
    outs, leaving = [], []
    for h, per_seg, v_new, oi, k_end_t in zip(heads, ws_qs, v_news, o_intra, k_end_ts):
        qs = [x[r1 - r0:] for (r0, r1), x in zip(segs, per_seg)]
        o = oi + (jnp.concatenate(qs, axis=0) if len(qs) > 1 else qs[0])
        outs.append(_rms_norm_rows(o, ng) * _silu(h["z"]))
        new_states = []
        for s, (r0, r1) in enumerate(segs):
            v_seg = v_new if len(segs) == 1 else _rows_only(v_new, r0, r1)
            new_states.append(jnp.exp(h["g_col"][r1 - 1:r1, :]) * h["states"][s]
                              + _dot(k_end_t, v_seg))
        leaving.append(new_states)
    n_seg = len(segs)
    return [(jnp.concatenate([outs[2 * p], outs[2 * p + 1]], axis=1),
             [[leaving[2 * p][s], leaving[2 * p + 1][s]] for s in range(n_seg)])
            for p in range(len(pairs))]


def _gdn_prompt_body(q_ref, k_ref, v_ref, z_ref, gb_ref, wq_ref, wk_ref, wv_ref, ng_ref, masks_ref,
                     og_ref, so_ref, s_ref, pq_ref, pk_ref, pv_ref, *, n_tiles, n_pairs, q_scale, hd):
    seq_block = pl.program_id(2)

    @pl.when(seq_block == 0)
    def _():
        s_ref[...] = jnp.zeros_like(s_ref)
        pq_ref[...] = jnp.zeros_like(pq_ref)
        pk_ref[...] = jnp.zeros_like(pk_ref)
        pv_ref[...] = jnp.zeros_like(pv_ref)

    def tile(t, carry):
        r0 = pl.multiple_of(t * LANES, LANES)
        rows = pl.ds(r0, LANES)
        prev = pl.ds(pl.multiple_of(jnp.maximum(r0 - SUBLANES, 0), SUBLANES), SUBLANES)

        def conv(ref, prev_ref, w_ref, cols):
            prev8 = jnp.where(t > 0, ref[prev, cols], prev_ref[:, cols])
            return _silu(_causal_conv(prev8, ref[rows, cols], w_ref[:, cols]))

        pairs = []
        for p in range(n_pairs):
            c1 = slice(p * hd, (p + 1) * hd)
            c2 = slice(2 * p * hd, 2 * (p + 1) * hd)
            q = _l2_norm_rows(conv(q_ref, pq_ref, wq_ref, c1)) * q_scale
            k = _l2_norm_rows(conv(k_ref, pk_ref, wk_ref, c1))
            v = conv(v_ref, pv_ref, wv_ref, c2)
            pairs.append((q, k, v, z_ref[rows, c2], gb_ref[p * SUBLANES:(p + 1) * SUBLANES, rows],
                          [[s_ref[2 * p], s_ref[2 * p + 1]]]))
        results = _gdn_tiles(pairs, ng_ref[...], LANES, hd, masks_ref)
        for p, (o, leaving) in enumerate(results):
            og_ref[rows, 2 * p * hd:2 * (p + 1) * hd] = o.astype(BF16)
            s_ref[2 * p] = leaving[0][0]
            s_ref[2 * p + 1] = leaving[0][1]
        return carry

    lax.fori_loop(0, n_tiles, tile, 0)
    tail = pl.ds(n_tiles * LANES - SUBLANES, SUBLANES)
    pq_ref[...] = q_ref[tail, :]
    pk_ref[...] = k_ref[tail, :]
    pv_ref[...] = v_ref[tail, :]

    @pl.when(seq_block == pl.num_programs(2) - 1)
    def _():
        so_ref[...] = s_ref[...]


def _gdn_sample_body(q_ref, k_ref, v_ref, z_ref, gb_ref, wq_ref, wk_ref, wv_ref, ng_ref, masks_ref,
                     hq_ref, hk_ref, hv_ref, s_ref, og_ref, so_ref, *, block, q_scale, hd):
    segs = _segments(block)

    def conv(ref, hist_ref, w_ref):
        x = ref[...]
        w = w_ref[...]
        return _silu(jnp.concatenate(
            [_causal_conv(hist_ref[s], x[r0:r1], w) for s, (r0, r1) in enumerate(segs)], axis=0))

    q = _l2_norm_rows(conv(q_ref, hq_ref, wq_ref)) * q_scale
    k = _l2_norm_rows(conv(k_ref, hk_ref, wk_ref))
    v = conv(v_ref, hv_ref, wv_ref)
    states = [[s_ref[s, 0], s_ref[s, 1]] for s in range(len(segs))]
    [(o, leaving)] = _gdn_tiles([(q, k, v, z_ref[...], gb_ref[...], states)], ng_ref[...], block,
                                hd, masks_ref)
    og_ref[...] = o.astype(BF16)
    for s in range(len(segs)):
        so_ref[s, 0] = leaving[s][0]
        so_ref[s, 1] = leaving[s][1]


def _gdn_recurrence(p, gb, conv_w, norm_g, hist8, state_in, layer, *,
                    n_prompt, batch, hq, hd, dec_seq):
    n = p.shape[0]
    t = n_prompt // batch
    key_w = hq * hd
    val_w = 2 * key_w
    q_scale = hd ** -0.5
    v_blk0 = (2 * key_w) // (2 * hd)
    z_blk0 = (2 * key_w + val_w) // (2 * hd)
    ng = norm_g.reshape(1, hd)
    w_specs = [pl.BlockSpec((CONV_TAPS, hd), lambda a, h: (0, h)),
               pl.BlockSpec((CONV_TAPS, hd), lambda a, h: (0, hq + h)),
               pl.BlockSpec((CONV_TAPS, 2 * hd), lambda a, h: (0, v_blk0 + h))]

    gp = math.gcd(GDN_PAIRS_PER_STEP, hq)
    sb = math.gcd(GDN_SEQ_BLOCK, t)
    nsb = t // sb
    qw, vw = gp * hd, 2 * gp * hd
    k_blk0 = key_w // qw
    masks_p = _inverse_level_masks(LANES)
    og_p, s_p = pl.pallas_call(
        functools.partial(_gdn_prompt_body, n_tiles=sb // LANES, n_pairs=gp, q_scale=q_scale, hd=hd),
        grid=(batch, hq // gp, nsb),
        in_specs=[pl.BlockSpec((sb, qw), lambda b, g, s: (b * nsb + s, g)),
                  pl.BlockSpec((sb, qw), lambda b, g, s: (b * nsb + s, k_blk0 + g)),
                  pl.BlockSpec((sb, vw), lambda b, g, s: (b * nsb + s, k_blk0 + g)),
                  pl.BlockSpec((sb, vw), lambda b, g, s: (b * nsb + s, 2 * k_blk0 + g)),
                  pl.BlockSpec((gp * SUBLANES, sb), lambda b, g, s: (g, b * nsb + s)),
                  pl.BlockSpec((CONV_TAPS, qw), lambda b, g, s: (0, g)),
                  pl.BlockSpec((CONV_TAPS, qw), lambda b, g, s: (0, k_blk0 + g)),
                  pl.BlockSpec((CONV_TAPS, vw), lambda b, g, s: (0, k_blk0 + g)),
                  _resident((1, hd)), _resident(masks_p.shape)],
        out_specs=[pl.BlockSpec((sb, vw), lambda b, g, s: (b * nsb + s, g)),
                   pl.BlockSpec((None, 2 * gp, hd, hd), lambda b, g, s: (b, g, 0, 0))],
        out_shape=[jax.ShapeDtypeStruct((n_prompt, val_w), BF16),
                   jax.ShapeDtypeStruct((batch, 2 * hq, hd, hd), F32)],
        scratch_shapes=[pltpu.VMEM((2 * gp, hd, hd), F32), pltpu.VMEM((SUBLANES, qw), F32),
                        pltpu.VMEM((SUBLANES, qw), F32), pltpu.VMEM((SUBLANES, vw), F32)],
        compiler_params=_cparams(3),
        name="gdn_prompt",
    )(p, p, p, p, gb, conv_w, conv_w, conv_w, ng, masks_p)

    n_sample = n - n_prompt
    dec_batch = n_sample // dec_seq
    seqs = LANES // dec_seq
    off = n_prompt // LANES
    masks_s = _inverse_level_masks(dec_seq)
    og_s, s_s = pl.pallas_call(
        functools.partial(_gdn_sample_body, block=dec_seq, q_scale=q_scale, hd=hd),
        grid=(n_sample // LANES, hq),
        in_specs=[pl.BlockSpec((LANES, hd), lambda g, h: (off + g, h)),
                  pl.BlockSpec((LANES, hd), lambda g, h: (off + g, hq + h)),
                  pl.BlockSpec((LANES, 2 * hd), lambda g, h: (off + g, v_blk0 + h)),
                  pl.BlockSpec((LANES, 2 * hd), lambda g, h: (off + g, z_blk0 + h)),
                  pl.BlockSpec((SUBLANES, LANES), lambda g, h: (h, off + g))] + w_specs + [
                  _resident((1, hd)), _resident(masks_s.shape),
                  pl.BlockSpec((seqs, SUBLANES, hd), lambda g, h: (g, 0, h)),
                  pl.BlockSpec((seqs, SUBLANES, hd), lambda g, h: (g, 0, hq + h)),
                  pl.BlockSpec((seqs, SUBLANES, 2 * hd), lambda g, h: (g, 0, v_blk0 + h)),
                  pl.BlockSpec((seqs, None, 2, hd, hd), lambda g, h: (g, layer, h, 0, 0))],
        out_specs=[pl.BlockSpec((LANES, 2 * hd), lambda g, h: (g, h)),
                   pl.BlockSpec((seqs, 2, hd, hd), lambda g, h: (g, h, 0, 0))],
        out_shape=[jax.ShapeDtypeStruct((n_sample, val_w), BF16),
                   jax.ShapeDtypeStruct((dec_batch, 2 * hq, hd, hd), F32)],
        compiler_params=_cparams(2),
        name="gdn_sample",
    )(p, p, p, p, gb, conv_w, conv_w, conv_w, ng, masks_s, hist8, hist8, hist8, state_in)
    return og_p, og_s, s_p, s_s


def _gdn_gate_rows(w_in, a_log, dt_bias, n_main, hq):
    d = w_in.shape[0]
    hv = a_log.shape[0]
    rep = hv // hq
    pad = SUBLANES - 2 * rep
    wa = w_in[:, n_main:n_main + hv].T.reshape(hq, rep, d)
    wb = w_in[:, n_main + hv:n_main + 2 * hv].T.reshape(hq, rep, d)
    w_ab_t = jnp.concatenate([wa, wb, jnp.zeros((hq, pad, d), w_in.dtype)], axis=1)
    zeros = jnp.zeros((hq, SUBLANES - rep), F32)
    alog = jnp.concatenate([a_log.reshape(hq, rep), zeros], axis=1).reshape(hq * SUBLANES, 1)
    dtb = jnp.concatenate([dt_bias.reshape(hq, rep), zeros], axis=1).reshape(hq * SUBLANES, 1)
    return w_ab_t.reshape(hq * SUBLANES, d).astype(BF16), alog, dtb


def kernel(x_prompt, x_sample, state_gla, state_gdn, state_gdn_conv, ln_g, ln_b, ffn1_w_gu, ffn1_w_d, ffn2_w_gu, ffn2_w_d, gla_w_in, gla_w_a1, gla_w_a2, gla_b_a, gla_norm_g, gla_w_o, gdn_w_in, gdn_conv_w, gdn_a_log, gdn_dt_bias, gdn_norm_g, gdn_w_o):
    batch, t, d = x_prompt.shape
    dec_batch, dec_seq, _ = x_sample.shape
    depth = ln_g.shape[0]
    alpha = (2.0 * depth) ** 0.25
    n_prompt, n_sample = batch * t, dec_batch * dec_seq
    tm = math.gcd(TOKEN_TILE, math.gcd(n_prompt, n_sample))
    assert t % LANES == 0 and t % GLA_CHUNK == 0 and n_sample % LANES == 0 and LANES % dec_seq == 0
    assert tm % LANES == 0

    gla_heads, gla_dk, gla_dv = state_gla.shape[2:]
    gla_qk_w = gla_heads * gla_dk
    hv, hd = state_gdn.shape[2:4]
    conv_dim = gdn_conv_w.shape[2]
    val_w = hv * hd
    hq = (conv_dim - val_w) // (2 * hd)
    assert hv == 2 * hq, "the gated-delta kernel pairs two v heads with each q/k head"
    n_main = conv_dim + val_w

    x = jnp.concatenate([x_prompt.reshape(n_prompt, d), x_sample.reshape(n_sample, d)], axis=0)
    new_gla_p, new_gla_s, new_gdn_p, new_gdn_s, new_conv_p, new_conv_s = [], [], [], [], [], []
    for i in range(depth):
        x = _ffn_ln(x, ffn1_w_gu[i].astype(BF16), ffn1_w_d[i].astype(BF16), ln_g[i, 0], ln_b[i, 0],
                    tm=tm, alpha=alpha)
        j = i // 2
        if i % 2 == 0:
            rank = gla_w_a1.shape[2]
            w_a1 = jnp.pad(gla_w_a1[j], ((0, 0), (0, LANES - rank))).astype(BF16)
            w_a2 = jnp.pad(gla_w_a2[j], ((0, LANES - rank), (0, 0))).astype(BF16)
            p, la = _gla_proj(x, gla_w_in[j].astype(BF16), w_a1, w_a2, gla_b_a[j],
                              tm=tm, qk_w=gla_qk_w, q_scale=gla_dk ** -0.5)
            og_p, og_s, s_p, s_s = _gla_recurrence(
                p, la, gla_norm_g[j], state_gla, j, n_prompt=n_prompt, batch=batch,
                heads=gla_heads, dk=gla_dk, dv=gla_dv, dec_seq=dec_seq)
            new_gla_p.append(s_p)
            new_gla_s.append(s_s)
            w_o = gla_w_o[j]
        else:
            w_ab_t, alog_rows, dtb_rows = _gdn_gate_rows(gdn_w_in[j], gdn_a_log[j], gdn_dt_bias[j],
                                                         n_main, hq)
            p, gb = _gdn_proj(x, gdn_w_in[j][:, :n_main].astype(BF16), w_ab_t, alog_rows, dtb_rows,
                              tm=tm)
            hist8 = jnp.pad(state_gdn_conv[:, j],
                            ((0, 0), (SUBLANES - (CONV_TAPS - 1), 0), (0, 0)))
            og_p, og_s, s_p, s_s = _gdn_recurrence(
                p, gb, gdn_conv_w[j], gdn_norm_g[j], hist8, state_gdn, j, n_prompt=n_prompt,
                batch=batch, hq=hq, hd=hd, dec_seq=dec_seq)
            new_gdn_p.append(s_p)
            new_gdn_s.append(s_s)
            keep = np.arange(-(CONV_TAPS - 1), 0)
            rows_p = ((np.arange(batch) + 1) * t)[:, None] + keep
            rows_s = (n_prompt + (np.arange(dec_batch) + 1) * dec_seq)[:, None] + keep
            new_conv_p.append(jnp.take(p, rows_p.reshape(-1), axis=0)[:, :conv_dim]
                              .reshape(batch, CONV_TAPS - 1, conv_dim))
            new_conv_s.append(jnp.take(p, rows_s.reshape(-1), axis=0)[:, :conv_dim]
                              .reshape(dec_batch, CONV_TAPS - 1, conv_dim))
            w_o = gdn_w_o[j]
        x = _out_ln(x, og_p, og_s, w_o.astype(BF16), ln_g[i, 1], ln_b[i, 1], tm=tm, alpha=alpha)
        x = _ffn_ln(x, ffn2_w_gu[i].astype(BF16), ffn2_w_d[i].astype(BF16), ln_g[i, 2], ln_b[i, 2],
                    tm=tm, alpha=alpha)
    return (x[:n_prompt].reshape(batch, t, d), x[n_prompt:].reshape(dec_batch, dec_seq, d),
            jnp.stack(new_gla_p, 1), jnp.stack(new_gdn_p, 1), jnp.stack(new_conv_p, 1),
            jnp.stack(new_gla_s, 1), jnp.stack(new_gdn_s, 1), jnp.stack(new_conv_s, 1))
```

```python
import functools
import math

import numpy as np

import jax
import jax.numpy as jnp
from jax import lax
from jax.experimental import pallas as pl
from jax.experimental.pallas import tpu as pltpu

F32 = jnp.float32
BF16 = jnp.bfloat16

LN_EPS = 1e-5
RMS_EPS = 1e-6
L2_EPS = 1e-6
FFN_RES = 0.5
GLA_GATE_TAU = 16.0
GLA_CHUNK = 64
CONV_TAPS = 4

LANES = 128
SUBLANES = 8
VMEM_LIMIT_BYTES = 56 * 1024 * 1024
TOKEN_TILE = 512
MATMUL_N_CHUNK = 1024
GDN_PAIRS_PER_STEP = 8
GDN_SEQ_BLOCK = 512
GLA_SEQ_BLOCK = 512


def _cparams(n_axes):
    return pltpu.CompilerParams(
        dimension_semantics=("arbitrary",) * n_axes, vmem_limit_bytes=VMEM_LIMIT_BYTES)


def _resident(shape):
    zeros = (0,) * len(shape)
    return pl.BlockSpec(shape, lambda *_: zeros, pipeline_mode=pl.Buffered(1))


def _dot(a, b):
    return jnp.dot(a.astype(BF16), b.astype(BF16), preferred_element_type=F32)


def _dot_nt(a, b):
    return lax.dot_general(a.astype(BF16), b.astype(BF16), (((1,), (1,)), ((), ())),
                           preferred_element_type=F32)


def _silu(x):
    return x * jax.nn.sigmoid(x)


def _softplus(x):
    return jnp.maximum(x, 0.0) + jnp.log1p(jnp.exp(-jnp.abs(x)))


def _layer_norm_rows(y, g, b):
    mu = jnp.mean(y, axis=-1, keepdims=True)
    d = y - mu
    var = jnp.mean(d * d, axis=-1, keepdims=True)
    return d * lax.rsqrt(var + LN_EPS) * g + b


def _rms_norm_rows(o, g):
    return o * lax.rsqrt(jnp.mean(o * o, axis=-1, keepdims=True) + RMS_EPS) * g


def _l2_norm_rows(x):
    return x * lax.rsqrt(jnp.sum(x * x, axis=-1, keepdims=True) + L2_EPS)


def _col_chunks(n, step=MATMUL_N_CHUNK):
    return [(c, min(c + step, n)) for c in range(0, n, step)]


def _ffn_ln_body(*refs, d_ff, alpha, n_in, n_prompt_tiles):
    x_refs = refs[:n_in]
    wgu_ref, wd_ref, g_ref, b_ref = refs[n_in:n_in + 4]
    o_refs, h_ref = refs[n_in + 4:-1], refs[-1]
    is_prompt = pl.program_id(0) < n_prompt_tiles
    x = x_refs[0][...] if n_in == 1 else jnp.where(is_prompt, x_refs[0][...], x_refs[1][...])
    xb = x.astype(BF16)
    for c0, c1 in _col_chunks(d_ff):
        gate = jnp.dot(xb, wgu_ref[:, c0:c1], preferred_element_type=F32)
        up = jnp.dot(xb, wgu_ref[:, d_ff + c0:d_ff + c1], preferred_element_type=F32)
        h_ref[:, c0:c1] = (_silu(gate) * up).astype(BF16)
    y = jnp.dot(h_ref[...], wd_ref[...], preferred_element_type=F32)
    out = _layer_norm_rows(alpha * x + FFN_RES * y, g_ref[...], b_ref[...])
    if len(o_refs) == 1:
        o_refs[0][...] = out
    else:
        o_refs[1][...] = out

        @pl.when(is_prompt)
        def _():
            o_refs[0][...] = out


def _ffn_ln(xs, w_gu, w_d, g, b, *, tm, alpha, n_prompt, split_out=False):
    xs = xs if isinstance(xs, (tuple, list)) else (xs,)
    d = xs[0].shape[1]
    n = sum(x.shape[0] for x in xs)
    d_ff = w_d.shape[0]
    ntp = n_prompt // tm
    tile = pl.BlockSpec((tm, d), lambda i: (i, 0))
    prompt_tile = pl.BlockSpec((tm, d), lambda i: (jnp.minimum(i, ntp - 1), 0))
    sample_tile = pl.BlockSpec((tm, d), lambda i: (jnp.maximum(i - ntp, 0), 0))
    pair = [prompt_tile, sample_tile]
    pair_shape = [jax.ShapeDtypeStruct((n_prompt, d), F32), jax.ShapeDtypeStruct((n - n_prompt, d), F32)]
    return pl.pallas_call(
        functools.partial(_ffn_ln_body, d_ff=d_ff, alpha=alpha, n_in=len(xs), n_prompt_tiles=ntp),
        grid=(n // tm,),
        in_specs=(pair if len(xs) == 2 else [tile]) + [
            _resident(w_gu.shape), _resident(w_d.shape), _resident((1, d)), _resident((1, d))],
        out_specs=pair if split_out else tile,
        out_shape=pair_shape if split_out else jax.ShapeDtypeStruct((n, d), F32),
        scratch_shapes=[pltpu.VMEM((tm, d_ff), BF16)],
        compiler_params=_cparams(1),
        name="ffn_ln",
    )(*xs, w_gu, w_d, g.reshape(1, d), b.reshape(1, d))


def _out_ln_body(x_ref, ogp_ref, ogs_ref, wo_ref, g_ref, b_ref, o_ref, *, n_prompt_tiles, alpha):
    is_prompt = pl.program_id(0) < n_prompt_tiles
    og = jnp.where(is_prompt, ogp_ref[...], ogs_ref[...])
    y = jnp.dot(og, wo_ref[...], preferred_element_type=F32)
    o_ref[...] = _layer_norm_rows(alpha * x_ref[...] + y, g_ref[...], b_ref[...])


def _out_ln(x, og_prompt, og_sample, w_o, g, b, *, tm, alpha):
    n, d = x.shape
    v = w_o.shape[0]
    ntp = og_prompt.shape[0] // tm
    tile = pl.BlockSpec((tm, d), lambda i: (i, 0))
    return pl.pallas_call(
        functools.partial(_out_ln_body, n_prompt_tiles=ntp, alpha=alpha),
        grid=(n // tm,),
        in_specs=[tile,
                  pl.BlockSpec((tm, v), lambda i: (jnp.minimum(i, ntp - 1), 0)),
                  pl.BlockSpec((tm, v), lambda i: (jnp.maximum(i - ntp, 0), 0)),
                  _resident(w_o.shape), _resident((1, d)), _resident((1, d))],
        out_specs=tile,
        out_shape=jax.ShapeDtypeStruct((n, d), F32),
        compiler_params=_cparams(1),
        name="out_ln",
    )(x, og_prompt, og_sample, w_o, g.reshape(1, d), b.reshape(1, d))


def _gla_proj_body(x_ref, win_ref, wa1_ref, wa2_ref, ba_ref, p_ref, la_ref, *, qk_w, q_scale):
    xb = x_ref[...].astype(BF16)
    n_out = win_ref.shape[1]
    p_ref[:, 0:qk_w] = jnp.dot(xb, win_ref[:, 0:qk_w], preferred_element_type=F32) * q_scale
    rest = [(qk_w, 2 * qk_w)] + [(a + 2 * qk_w, b + 2 * qk_w)
                                  for a, b in _col_chunks(n_out - 2 * qk_w)]
    for c0, c1 in rest:
        p_ref[:, c0:c1] = jnp.dot(xb, win_ref[:, c0:c1], preferred_element_type=F32)
    low = jnp.dot(xb, wa1_ref[...], preferred_element_type=F32)
    logits = _dot(low, wa2_ref[...]) + ba_ref[...]
    log_sig = jnp.minimum(logits, 0.0) - jnp.log1p(jnp.exp(-jnp.abs(logits)))
    la_ref[...] = log_sig / GLA_GATE_TAU


def _gla_proj(x, w_in, w_a1, w_a2, b_a, *, tm, qk_w, q_scale):
    n, d = x.shape
    n_out = w_in.shape[1]
    return pl.pallas_call(
        functools.partial(_gla_proj_body, qk_w=qk_w, q_scale=q_scale),
        grid=(n // tm,),
        in_specs=[pl.BlockSpec((tm, d), lambda i: (i, 0)), _resident(w_in.shape),
                  _resident(w_a1.shape), _resident(w_a2.shape), _resident((1, qk_w))],
        out_specs=[pl.BlockSpec((tm, n_out), lambda i: (i, 0)),
                   pl.BlockSpec((tm, qk_w), lambda i: (i, 0))],
        out_shape=[jax.ShapeDtypeStruct((n, n_out), F32), jax.ShapeDtypeStruct((n, qk_w), F32)],
        compiler_params=_cparams(1),
        name="gla_proj",
    )(x, w_in, w_a1, w_a2, b_a.reshape(1, qk_w))


def _gdn_proj_body(x_ref, wmain_ref, wab_ref, alog_ref, dtb_ref, p_ref, gb_ref):
    xb = x_ref[...].astype(BF16)
    for c0, c1 in _col_chunks(wmain_ref.shape[1]):
        p_ref[:, c0:c1] = jnp.dot(xb, wmain_ref[:, c0:c1], preferred_element_type=F32)
    ab = lax.dot_general(wab_ref[...], xb, (((1,), (1,)), ((), ())), preferred_element_type=F32)
    kind = _iota2(ab.shape, 0) & (SUBLANES - 1)
    g = -jnp.exp(alog_ref[...]) * _softplus(ab + dtb_ref[...])
    beta = jax.nn.sigmoid(ab)
    gb_ref[...] = jnp.where(kind < 2, g, jnp.where(kind < 4, beta, 0.0))


def _gdn_proj(x, w_main, w_ab_t, a_log_rows, dt_bias_rows, *, tm):
    n, d = x.shape
    n_out = w_main.shape[1]
    n_rows = w_ab_t.shape[0]
    return pl.pallas_call(
        _gdn_proj_body,
        grid=(n // tm,),
        in_specs=[pl.BlockSpec((tm, d), lambda i: (i, 0)), _resident(w_main.shape),
                  _resident(w_ab_t.shape), _resident((n_rows, 1)), _resident((n_rows, 1))],
        out_specs=[pl.BlockSpec((tm, n_out), lambda i: (i, 0)),
                   pl.BlockSpec((n_rows, tm), lambda i: (0, i))],
        out_shape=[jax.ShapeDtypeStruct((n, n_out), F32), jax.ShapeDtypeStruct((n_rows, n), F32)],
        compiler_params=_cparams(1),
        name="gdn_proj",
    )(x, w_main, w_ab_t, a_log_rows, dt_bias_rows)


def _iota2(shape, dim):
    return lax.broadcasted_iota(jnp.int32, shape, dim)


def _block_of(idx, block):
    return idx >> (block.bit_length() - 1)


def _block_cumsum(x, block, axis):
    pos = _iota2(x.shape, axis) & (block - 1)
    shift = 1
    while shift < block:
        x = x + jnp.where(pos >= shift, pltpu.roll(x, shift, axis), 0.0)
        shift *= 2
    return x


def _col_form(row):
    return jnp.transpose(jnp.broadcast_to(row, (LANES, LANES)))


def _segments(block):
    return [(r, r + block) for r in range(0, LANES, block)]


def _last_row_of_segments(x, block):
    return jnp.concatenate(
        [jnp.broadcast_to(x[r1 - 1:r1, :], (r1 - r0, x.shape[1])) for r0, r1 in _segments(block)],
        axis=0)


def _rows_only(x, r0, r1):
    rows = _iota2(x.shape, 0)
    return jnp.where((rows >= r0) & (rows < r1), x, 0.0)


def _gla_tiles(heads, ng, block, chained):
    row = _iota2((LANES, LANES), 0)
    col = _iota2((LANES, LANES), 1)
    causal = (_block_of(row, block) == _block_of(col, block)) & (row >= col)
    segs = _segments(block)
    bcums = [_block_cumsum(la, block, 0) for (_, _, _, la, _, _) in heads]
    q_decs = [q * jnp.exp(b) for (q, _, _, _, _, _), b in zip(heads, bcums)]
    atts = [jnp.where(causal, _dot_nt(qd, k * jnp.exp(-b)), 0.0)
            for qd, (_, k, _, _, _, _), b in zip(q_decs, heads, bcums)]
    intras = [_dot(att, v) for att, (_, _, v, _, _, _) in zip(atts, heads)]
    k_end_ts = [jnp.transpose(k * jnp.exp(_last_row_of_segments(b, block) - b))
                for (_, k, _, _, _, _), b in zip(heads, bcums)]
    lane_reps = heads[0][2].shape[1] // LANES
    states = [list(st) for (_, _, _, _, _, st) in heads]
    cur = [st[0] for st in states]
    outs = [[] for _ in heads]
    leaving = [[] for _ in heads]
    for s, (r0, r1) in enumerate(segs):
        if not chained:
            cur = [st[s] for st in states]
        inter = [_dot(qd[r0:r1], c) for qd, c in zip(q_decs, cur)]
        upd = [_dot(kt, _rows_only(v, r0, r1)) for kt, (_, _, v, _, _, _) in zip(k_end_ts, heads)]
        for i, b in enumerate(bcums):
            outs[i].append(intras[i][r0:r1] + inter[i])
            decay = jnp.concatenate([_col_form(jnp.exp(b[r1 - 1:r1, :]))] * lane_reps, axis=1)
            cur[i] = decay * cur[i] + upd[i]
            if not chained:
                leaving[i].append(cur[i])
    results = []
    for i, (_, _, _, _, r, _) in enumerate(heads):
        o = _rms_norm_rows(jnp.concatenate(outs[i], axis=0), ng)
        results.append((o * _silu(r), leaving[i] if not chained else [cur[i]]))
    return results


def _gla_prompt_body(q_ref, k_ref, v_ref, r_ref, la_ref, ng_ref, og_ref, so_ref, s_ref, *,
                     n_tiles, n_heads, dk, dv):
    seq_block = pl.program_id(1)

    @pl.when(seq_block == 0)
    def _():
        s_ref[...] = jnp.zeros_like(s_ref)

    def tile(t, carry):
        rows = pl.ds(pl.multiple_of(t * LANES, LANES), LANES)
        heads = []
        for h in range(n_heads):
            ck = slice(h * dk, (h + 1) * dk)
            cv = slice(h * dv, (h + 1) * dv)
            heads.append((q_ref[rows, ck], k_ref[rows, ck], v_ref[rows, cv], la_ref[rows, ck],
                          r_ref[rows, cv], [s_ref[h]]))
        for h, (o, leaving) in enumerate(_gla_tiles(heads, ng_ref[...], GLA_CHUNK, True)):
            og_ref[rows, h * dv:(h + 1) * dv] = o.astype(BF16)
            s_ref[h] = leaving[0]
        return carry

    lax.fori_loop(0, n_tiles, tile, 0)

    @pl.when(seq_block == pl.num_programs(1) - 1)
    def _():
        so_ref[...] = s_ref[...]


def _gla_sample_body(q_ref, k_ref, v_ref, r_ref, la_ref, ng_ref, s_ref, og_ref, so_ref, *, block):
    n_seq = LANES // block
    [(o, leaving)] = _gla_tiles(
        [(q_ref[...], k_ref[...], v_ref[...], la_ref[...], r_ref[...],
          [s_ref[i] for i in range(n_seq)])], ng_ref[...], block, False)
    og_ref[...] = o.astype(BF16)
    for i in range(n_seq):
        so_ref[i] = leaving[i]


def _gla_recurrence(p, la, norm_g, state_in, layer, *, n_prompt, batch, heads, dk, dv, dec_seq):
    n = p.shape[0]
    t = n_prompt // batch
    qk_w, v_w = heads * dk, heads * dv
    v_blk0 = (2 * qk_w) // dv
    r_blk0 = (2 * qk_w + v_w) // dv
    ng = norm_g.reshape(1, dv)

    sb = math.gcd(GLA_SEQ_BLOCK, t)
    nsb = t // sb
    og_p, s_p = pl.pallas_call(
        functools.partial(_gla_prompt_body, n_tiles=sb // LANES, n_heads=heads, dk=dk, dv=dv),
        grid=(batch, nsb),
        in_specs=[pl.BlockSpec((sb, qk_w), lambda b, s: (b * nsb + s, 0)),
                  pl.BlockSpec((sb, qk_w), lambda b, s: (b * nsb + s, 1)),
                  pl.BlockSpec((sb, v_w), lambda b, s: (b * nsb + s, (2 * qk_w) // v_w)),
                  pl.BlockSpec((sb, v_w), lambda b, s: (b * nsb + s, (2 * qk_w + v_w) // v_w)),
                  pl.BlockSpec((sb, qk_w), lambda b, s: (b * nsb + s, 0)),
                  _resident((1, dv))],
        out_specs=[pl.BlockSpec((sb, v_w), lambda b, s: (b * nsb + s, 0)),
                   pl.BlockSpec((None, heads, dk, dv), lambda b, s: (b, 0, 0, 0))],
        out_shape=[jax.ShapeDtypeStruct((n_prompt, v_w), BF16),
                   jax.ShapeDtypeStruct((batch, heads, dk, dv), F32)],
        scratch_shapes=[pltpu.VMEM((heads, dk, dv), F32)],
        compiler_params=_cparams(2),
        name="gla_prompt",
    )(p, p, p, p, la, ng)

    n_sample = n - n_prompt
    dec_batch = n_sample // dec_seq
    seqs = LANES // dec_seq
    off = n_prompt // LANES
    og_s, s_s = pl.pallas_call(
        functools.partial(_gla_sample_body, block=dec_seq),
        grid=(n_sample // LANES, heads),
        in_specs=[pl.BlockSpec((LANES, dk), lambda g, h: (off + g, h)),
                  pl.BlockSpec((LANES, dk), lambda g, h: (off + g, heads + h)),
                  pl.BlockSpec((LANES, dv), lambda g, h: (off + g, v_blk0 + h)),
                  pl.BlockSpec((LANES, dv), lambda g, h: (off + g, r_blk0 + h)),
                  pl.BlockSpec((LANES, dk), lambda g, h: (off + g, h)),
                  _resident((1, dv)),
                  pl.BlockSpec((seqs, None, None, dk, dv), lambda g, h: (g, layer, h, 0, 0))],
        out_specs=[pl.BlockSpec((LANES, dv), lambda g, h: (g, h)),
                   pl.BlockSpec((seqs, None, dk, dv), lambda g, h: (g, h, 0, 0))],
        out_shape=[jax.ShapeDtypeStruct((n_sample, v_w), BF16),
                   jax.ShapeDtypeStruct((dec_batch, heads, dk, dv), F32)],
        compiler_params=_cparams(2),
        name="gla_sample",
    )(p, p, p, p, la, ng, state_in)
    return og_p, og_s, s_p, s_s


def _causal_conv(prev8, x, w):
    n = x.shape[0]
    xs = jnp.concatenate([prev8, x], axis=0)
    acc = xs[SUBLANES:, :] * w[CONV_TAPS - 1:CONV_TAPS, :]
    for j in range(CONV_TAPS - 1):
        shifted = pltpu.roll(xs, CONV_TAPS - 1 - j, 0)
        acc = acc + shifted[SUBLANES:SUBLANES + n, :] * w[j:j + 1, :]
    return acc


def _inverse_level_masks(block):
    idx = np.arange(LANES)
    masks = []
    size = 1
    while size < block:
        pair = (idx[:, None] // (2 * size)) == (idx[None, :] // (2 * size))
        masks.append(pair & ((idx[:, None] // size) != (idx[None, :] // size)))
        size *= 2
    return jnp.asarray(np.stack(masks).astype(np.float32))


def _unit_lower_inverses(mats, masks_ref):
    row = _iota2((LANES, LANES), 0)
    col = _iota2((LANES, LANES), 1)
    eye = jnp.where(row == col, 1.0, 0.0)
    xs = [eye - a * masks_ref[0] for a in mats]
    mats_b = [a.astype(BF16) for a in mats]
    for level in range(1, masks_ref.shape[0]):
        mask = masks_ref[level]
        xbs = [x.astype(BF16) for x in xs]
        ys = [jnp.dot(xb, ab, preferred_element_type=F32) for xb, ab in zip(xbs, mats_b)]
        xs = [x - jnp.dot(y.astype(BF16), xb, preferred_element_type=F32) * mask
              for x, y, xb in zip(xs, ys, xbs)]
    return xs


def _gdn_tiles(pairs, ng, block, hd, masks_ref):
    row = _iota2((LANES, LANES), 0)
    col = _iota2((LANES, LANES), 1)
    same = _block_of(row, block) == _block_of(col, block)
    incl = same & (row >= col)
    strict = same & (row > col)
    segs = _segments(block)
    kks = [_dot_nt(k, k) for (_, k, _, _, _, _) in pairs]
    qks = [_dot_nt(q, k) for (q, k, _, _, _, _) in pairs]
    heads = []
    for (q, k, v, z, gb, states), kk, qk in zip(pairs, kks, qks):
        gcum = _block_cumsum(gb, block, 1)
        for e in range(2):
            g_row = gcum[e:e + 1, :]
            g_col = _col_form(g_row)
            b_col = _col_form(gb[2 + e:3 + e, :])
            decay = jnp.where(incl, jnp.exp(g_col - jnp.broadcast_to(g_row, (LANES, LANES))), 0.0)
            eg = jnp.exp(g_col)
            heads.append(dict(
                q_dec=q * eg, att=qk * decay, g_col=g_col,
                a=jnp.where(strict, b_col * kk * decay, 0.0),
                rhs=jnp.concatenate([k * (b_col * eg), v[:, e * hd:(e + 1) * hd] * b_col], axis=1),
                k_end=k * jnp.exp(_last_row_of_segments(g_col, block) - g_col),
                z=z[:, e * hd:(e + 1) * hd], states=[st[e] for st in states]))
    t_invs = _unit_lower_inverses([h["a"] for h in heads], masks_ref)
    wus = [_dot(t_inv, h["rhs"]) for t_inv, h in zip(t_invs, heads)]
    k_end_ts = [jnp.transpose(h["k_end"]) for h in heads]
    ws_qs = [[_dot(jnp.concatenate([wu[r0:r1, :hd], h["q_dec"][r0:r1]], axis=0), h["states"][s])
              for s, (r0, r1) in enumerate(segs)] for wu, h in zip(wus, heads)]
    v_news = []
    for wu, per_seg in zip(wus, ws_qs):
        parts = [wu[r0:r1, hd:] - x[:r1 - r0] for (r0, r1), x in zip(segs, per_seg)]
        v_news.append(jnp.concatenate(parts, axis=0) if len(parts) > 1 else parts[0])
    o_intra = [_dot(h["att"], v_new) for h, v_new in zip(heads, v_news)]
    outs, leaving = [], []
    for h, per_seg, v_new, oi, k_end_t in zip(heads, ws_qs, v_news, o_intra, k_end_ts):
        qs = [x[r1 - r0:] for (r0, r1), x in zip(segs, per_seg)]
        o = oi + (jnp.concatenate(qs, axis=0) if len(qs) > 1 else qs[0])
        outs.append(_rms_norm_rows(o, ng) * _silu(h["z"]))
        new_states = []
        for s, (r0, r1) in enumerate(segs):
            v_seg = v_new if len(segs) == 1 else _rows_only(v_new, r0, r1)
            new_states.append(jnp.exp(h["g_col"][r1 - 1:r1, :]) * h["states"][s]
                              + _dot(k_end_t, v_seg))
        leaving.append(new_states)
    n_seg = len(segs)
    return [(jnp.concatenate([outs[2 * p], outs[2 * p + 1]], axis=1),
             [[leaving[2 * p][s], leaving[2 * p + 1][s]] for s in range(n_seg)])
            for p in range(len(pairs))]


def _gdn_prompt_body(q_ref, k_ref, v_ref, z_ref, gb_ref, wq_ref, wk_ref, wv_ref, ng_ref, masks_ref,
                     og_ref, so_ref, s_ref, pq_ref, pk_ref, pv_ref, *, n_tiles, n_pairs, q_scale, hd):
    seq_block = pl.program_id(2)

    @pl.when(seq_block == 0)
    def _():
        s_ref[...] = jnp.zeros_like(s_ref)
        pq_ref[...] = jnp.zeros_like(pq_ref)
        pk_ref[...] = jnp.zeros_like(pk_ref)
        pv_ref[...] = jnp.zeros_like(pv_ref)

    def tile(t, carry):
        r0 = pl.multiple_of(t * LANES, LANES)
        rows = pl.ds(r0, LANES)
        prev = pl.ds(pl.multiple_of(jnp.maximum(r0 - SUBLANES, 0), SUBLANES), SUBLANES)

        def conv(ref, prev_ref, w_ref, cols):
            prev8 = jnp.where(t > 0, ref[prev, cols], prev_ref[:, cols])
            return _silu(_causal_conv(prev8, ref[rows, cols], w_ref[:, cols]))

        pairs = []
        for p in range(n_pairs):
            c1 = slice(p * hd, (p + 1) * hd)
            c2 = slice(2 * p * hd, 2 * (p + 1) * hd)
            q = _l2_norm_rows(conv(q_ref, pq_ref, wq_ref, c1)) * q_scale
            k = _l2_norm_rows(conv(k_ref, pk_ref, wk_ref, c1))
            v = conv(v_ref, pv_ref, wv_ref, c2)
            pairs.append((q, k, v, z_ref[rows, c2], gb_ref[p * SUBLANES:(p + 1) * SUBLANES, rows],
                          [[s_ref[2 * p], s_ref[2 * p + 1]]]))
        results = _gdn_tiles(pairs, ng_ref[...], LANES, hd, masks_ref)
        for p, (o, leaving) in enumerate(results):
            og_ref[rows, 2 * p * hd:2 * (p + 1) * hd] = o.astype(BF16)
            s_ref[2 * p] = leaving[0][0]
            s_ref[2 * p + 1] = leaving[0][1]
        return carry

    lax.fori_loop(0, n_tiles, tile, 0)
    tail = pl.ds(n_tiles * LANES - SUBLANES, SUBLANES)
    pq_ref[...] = q_ref[tail, :]
    pk_ref[...] = k_ref[tail, :]
    pv_ref[...] = v_ref[tail, :]

    @pl.when(seq_block == pl.num_programs(2) - 1)
    def _():
        so_ref[...] = s_ref[...]


def _gdn_sample_body(q_ref, k_ref, v_ref, z_ref, gb_ref, wq_ref, wk_ref, wv_ref, ng_ref, masks_ref,
                     hq_ref, hk_ref, hv_ref, s_ref, og_ref, so_ref, *, block, q_scale, hd):
    segs = _segments(block)

    def conv(ref, hist_ref, w_ref):
        x = ref[...]
        w = w_ref[...]
        return _silu(jnp.concatenate(
            [_causal_conv(hist_ref[s], x[r0:r1], w) for s, (r0, r1) in enumerate(segs)], axis=0))

    q = _l2_norm_rows(conv(q_ref, hq_ref, wq_ref)) * q_scale
    k = _l2_norm_rows(conv(k_ref, hk_ref, wk_ref))
    v = conv(v_ref, hv_ref, wv_ref)
    states = [[s_ref[s, 0], s_ref[s, 1]] for s in range(len(segs))]
    [(o, leaving)] = _gdn_tiles([(q, k, v, z_ref[...], gb_ref[...], states)], ng_ref[...], block,
                                hd, masks_ref)
    og_ref[...] = o.astype(BF16)
    for s in range(len(segs)):
        so_ref[s, 0] = leaving[s][0]
        so_ref[s, 1] = leaving[s][1]


def _gdn_recurrence(p, gb, conv_w, norm_g, hist8, state_in, layer, *,
                    n_prompt, batch, hq, hd, dec_seq):
    n = p.shape[0]
    t = n_prompt // batch
    key_w = hq * hd
    val_w = 2 * key_w
    q_scale = hd ** -0.5
    v_blk0 = (2 * key_w) // (2 * hd)
    z_blk0 = (2 * key_w + val_w) // (2 * hd)
    ng = norm_g.reshape(1, hd)
    w_specs = [pl.BlockSpec((CONV_TAPS, hd), lambda a, h: (0, h)),
               pl.BlockSpec((CONV_TAPS, hd), lambda a, h: (0, hq + h)),
               pl.BlockSpec((CONV_TAPS, 2 * hd), lambda a, h: (0, v_blk0 + h))]

    gp = math.gcd(GDN_PAIRS_PER_STEP, hq)
    sb = math.gcd(GDN_SEQ_BLOCK, t)
    nsb = t // sb
    qw, vw = gp * hd, 2 * gp * hd
    k_blk0 = key_w // qw
    masks_p = _inverse_level_masks(LANES)
    og_p, s_p = pl.pallas_call(
        functools.partial(_gdn_prompt_body, n_tiles=sb // LANES, n_pairs=gp, q_scale=q_scale, hd=hd),
        grid=(batch, hq // gp, nsb),
        in_specs=[pl.BlockSpec((sb, qw), lambda b, g, s: (b * nsb + s, g)),
                  pl.BlockSpec((sb, qw), lambda b, g, s: (b * nsb + s, k_blk0 + g)),
                  pl.BlockSpec((sb, vw), lambda b, g, s: (b * nsb + s, k_blk0 + g)),
                  pl.BlockSpec((sb, vw), lambda b, g, s: (b * nsb + s, 2 * k_blk0 + g)),
                  pl.BlockSpec((gp * SUBLANES, sb), lambda b, g, s: (g, b * nsb + s)),
                  pl.BlockSpec((CONV_TAPS, qw), lambda b, g, s: (0, g)),
                  pl.BlockSpec((CONV_TAPS, qw), lambda b, g, s: (0, k_blk0 + g)),
                  pl.BlockSpec((CONV_TAPS, vw), lambda b, g, s: (0, k_blk0 + g)),
                  _resident((1, hd)), _resident(masks_p.shape)],
        out_specs=[pl.BlockSpec((sb, vw), lambda b, g, s: (b * nsb + s, g)),
                   pl.BlockSpec((None, 2 * gp, hd, hd), lambda b, g, s: (b, g, 0, 0))],
        out_shape=[jax.ShapeDtypeStruct((n_prompt, val_w), BF16),
                   jax.ShapeDtypeStruct((batch, 2 * hq, hd, hd), F32)],
        scratch_shapes=[pltpu.VMEM((2 * gp, hd, hd), F32), pltpu.VMEM((SUBLANES, qw), F32),
                        pltpu.VMEM((SUBLANES, qw), F32), pltpu.VMEM((SUBLANES, vw), F32)],
        compiler_params=_cparams(3),
        name="gdn_prompt",
    )(p, p, p, p, gb, conv_w, conv_w, conv_w, ng, masks_p)

    n_sample = n - n_prompt
    dec_batch = n_sample // dec_seq
    seqs = LANES // dec_seq
    off = n_prompt // LANES
    masks_s = _inverse_level_masks(dec_seq)
    og_s, s_s = pl.pallas_call(
        functools.partial(_gdn_sample_body, block=dec_seq, q_scale=q_scale, hd=hd),
        grid=(n_sample // LANES, hq),
        in_specs=[pl.BlockSpec((LANES, hd), lambda g, h: (off + g, h)),
                  pl.BlockSpec((LANES, hd), lambda g, h: (off + g, hq + h)),
                  pl.BlockSpec((LANES, 2 * hd), lambda g, h: (off + g, v_blk0 + h)),
                  pl.BlockSpec((LANES, 2 * hd), lambda g, h: (off + g, z_blk0 + h)),
                  pl.BlockSpec((SUBLANES, LANES), lambda g, h: (h, off + g))] + w_specs + [
                  _resident((1, hd)), _resident(masks_s.shape),
                  pl.BlockSpec((seqs, SUBLANES, hd), lambda g, h: (g, 0, h)),
                  pl.BlockSpec((seqs, SUBLANES, hd), lambda g, h: (g, 0, hq + h)),
                  pl.BlockSpec((seqs, SUBLANES, 2 * hd), lambda g, h: (g, 0, v_blk0 + h)),
                  pl.BlockSpec((seqs, None, 2, hd, hd), lambda g, h: (g, layer, h, 0, 0))],
        out_specs=[pl.BlockSpec((LANES, 2 * hd), lambda g, h: (g, h)),
                   pl.BlockSpec((seqs, 2, hd, hd), lambda g, h: (g, h, 0, 0))],
        out_shape=[jax.ShapeDtypeStruct((n_sample, val_w), BF16),
                   jax.ShapeDtypeStruct((dec_batch, 2 * hq, hd, hd), F32)],
        compiler_params=_cparams(2),
        name="gdn_sample",
    )(p, p, p, p, gb, conv_w, conv_w, conv_w, ng, masks_s, hist8, hist8, hist8, state_in)
    return og_p, og_s, s_p, s_s


def _gdn_gate_rows(w_in, a_log, dt_bias, n_main, hq):
    d = w_in.shape[0]
    hv = a_log.shape[0]
    rep = hv // hq
    pad = SUBLANES - 2 * rep
    wa = w_in[:, n_main:n_main + hv].T.reshape(hq, rep, d)
    wb = w_in[:, n_main + hv:n_main + 2 * hv].T.reshape(hq, rep, d)
    w_ab_t = jnp.concatenate([wa, wb, jnp.zeros((hq, pad, d), w_in.dtype)], axis=1)
    zeros = jnp.zeros((hq, SUBLANES - rep), F32)
    alog = jnp.concatenate([a_log.reshape(hq, rep), zeros], axis=1).reshape(hq * SUBLANES, 1)
    dtb = jnp.concatenate([dt_bias.reshape(hq, rep), zeros], axis=1).reshape(hq * SUBLANES, 1)
    return w_ab_t.reshape(hq * SUBLANES, d).astype(BF16), alog, dtb


def kernel(x_prompt, x_sample, state_gla, state_gdn, state_gdn_conv, ln_g, ln_b, ffn1_w_gu, ffn1_w_d, ffn2_w_gu, ffn2_w_d, gla_w_in, gla_w_a1, gla_w_a2, gla_b_a, gla_norm_g, gla_w_o, gdn_w_in, gdn_conv_w, gdn_a_log, gdn_dt_bias, gdn_norm_g, gdn_w_o):
    batch, t, d = x_prompt.shape
    dec_batch, dec_seq, _ = x_sample.shape
    depth = ln_g.shape[0]
    alpha = (2.0 * depth) ** 0.25
    n_prompt, n_sample = batch * t, dec_batch * dec_seq
    tm = math.gcd(TOKEN_TILE, math.gcd(n_prompt, n_sample))
    assert t % LANES == 0 and t % GLA_CHUNK == 0 and n_sample % LANES == 0 and LANES % dec_seq == 0
    assert tm % LANES == 0

    gla_heads, gla_dk, gla_dv = state_gla.shape[2:]
    gla_qk_w = gla_heads * gla_dk
    hv, hd = state_gdn.shape[2:4]
    conv_dim = gdn_conv_w.shape[2]
    val_w = hv * hd
    hq = (conv_dim - val_w) // (2 * hd)
    assert hv == 2 * hq, "the gated-delta kernel pairs two v heads with each q/k head"
    n_main = conv_dim + val_w

    x = (x_prompt.reshape(n_prompt, d), x_sample.reshape(n_sample, d))
    new_gla_p, new_gla_s, new_gdn_p, new_gdn_s, new_conv_p, new_conv_s = [], [], [], [], [], []
    for i in range(depth):
        x = _ffn_ln(x, ffn1_w_gu[i].astype(BF16), ffn1_w_d[i].astype(BF16), ln_g[i, 0], ln_b[i, 0],
                    tm=tm, alpha=alpha, n_prompt=n_prompt)
        j = i // 2
        if i % 2 == 0:
            rank = gla_w_a1.shape[2]
            w_a1 = jnp.pad(gla_w_a1[j], ((0, 0), (0, LANES - rank))).astype(BF16)
            w_a2 = jnp.pad(gla_w_a2[j], ((0, LANES - rank), (0, 0))).astype(BF16)
            p, la = _gla_proj(x, gla_w_in[j].astype(BF16), w_a1, w_a2, gla_b_a[j],
                              tm=tm, qk_w=gla_qk_w, q_scale=gla_dk ** -0.5)
            og_p, og_s, s_p, s_s = _gla_recurrence(
                p, la, gla_norm_g[j], state_gla, j, n_prompt=n_prompt, batch=batch,
                heads=gla_heads, dk=gla_dk, dv=gla_dv, dec_seq=dec_seq)
            new_gla_p.append(s_p)
            new_gla_s.append(s_s)
            w_o = gla_w_o[j]
        else:
            w_ab_t, alog_rows, dtb_rows = _gdn_gate_rows(gdn_w_in[j], gdn_a_log[j], gdn_dt_bias[j],
                                                         n_main, hq)
            p, gb = _gdn_proj(x, gdn_w_in[j][:, :n_main].astype(BF16), w_ab_t, alog_rows, dtb_rows,
                              tm=tm)
            hist8 = jnp.pad(state_gdn_conv[:, j],
                            ((0, 0), (SUBLANES - (CONV_TAPS - 1), 0), (0, 0)))
            og_p, og_s, s_p, s_s = _gdn_recurrence(
                p, gb, gdn_conv_w[j], gdn_norm_g[j], hist8, state_gdn, j, n_prompt=n_prompt,
                batch=batch, hq=hq, hd=hd, dec_seq=dec_seq)
            new_gdn_p.append(s_p)
            new_gdn_s.append(s_s)
            keep = np.arange(-(CONV_TAPS - 1), 0)
            rows_p = ((np.arange(batch) + 1) * t)[:, None] + keep
            rows_s = (n_prompt + (np.arange(dec_batch) + 1) * dec_seq)[:, None] + keep
            new_conv_p.append(jnp.take(p, rows_p.reshape(-1), axis=0)[:, :conv_dim]
                              .reshape(batch, CONV_TAPS - 1, conv_dim))
            new_conv_s.append(jnp.take(p, rows_s.reshape(-1), axis=0)[:, :conv_dim]
                              .reshape(dec_batch, CONV_TAPS - 1, conv_dim))
            w_o = gdn_w_o[j]
        x = _out_ln(x, og_p, og_s, w_o.astype(BF16), ln_g[i, 1], ln_b[i, 1], tm=tm, alpha=alpha)
        x = _ffn_ln(x, ffn2_w_gu[i].astype(BF16), ffn2_w_d[i].astype(BF16), ln_g[i, 2], ln_b[i, 2],
                    tm=tm, alpha=alpha, n_prompt=n_prompt, split_out=(i == depth - 1))
    y_prompt, y_sample = x
    return (y_prompt.reshape(batch, t, d), y_sample.reshape(dec_batch, dec_seq, d),
            jnp.stack(new_gla_p, 1), jnp.stack(new_gdn_p, 1), jnp.stack(new_conv_p, 1),
            jnp.stack(new_gla_s, 1), jnp.stack(new_gdn_s, 1), jnp.stack(new_conv_s, 1))
```

```python
import functools
import math

import numpy as np

import jax
import jax.numpy as jnp
from jax import lax
from jax.experimental import pallas as pl
from jax.experimental.pallas import tpu as pltpu

F32 = jnp.float32
BF16 = jnp.bfloat16

LN_EPS = 1e-5
RMS_EPS = 1e-6
L2_EPS = 1e-6
FFN_RES = 0.5
GLA_GATE_TAU = 16.0
GLA_CHUNK = 64
CONV_TAPS = 4

LANES = 128
SUBLANES = 8
VMEM_LIMIT_BYTES = 56 * 1024 * 1024
TOKEN_TILE = 512
MATMUL_N_CHUNK = 1024
GDN_PAIRS_PER_STEP = 8
GDN_SEQ_BLOCK = 512
GLA_SEQ_BLOCK = 512


def _cparams(n_axes):
    return pltpu.CompilerParams(
        dimension_semantics=("arbitrary",) * n_axes, vmem_limit_bytes=VMEM_LIMIT_BYTES)


def _resident(shape):
    zeros = (0,) * len(shape)
    return pl.BlockSpec(shape, lambda *_: zeros, pipeline_mode=pl.Buffered(1))


def _dot(a, b):
    return jnp.dot(a.astype(BF16), b.astype(BF16), preferred_element_type=F32)


def _dot_nt(a, b):
    return lax.dot_general(a.astype(BF16), b.astype(BF16), (((1,), (1,)), ((), ())),
                           preferred_element_type=F32)


def _silu(x):
    return x * jax.nn.sigmoid(x)


def _softplus(x):
    return jnp.maximum(x, 0.0) + jnp.log1p(jnp.exp(-jnp.abs(x)))


def _layer_norm_rows(y, g, b):
    mu = jnp.mean(y, axis=-1, keepdims=True)
    d = y - mu
    var = jnp.mean(d * d, axis=-1, keepdims=True)
    return d * lax.rsqrt(var + LN_EPS) * g + b


def _rms_norm_rows(o, g):
    return o * lax.rsqrt(jnp.mean(o * o, axis=-1, keepdims=True) + RMS_EPS) * g


def _l2_norm_rows(x):
    return x * lax.rsqrt(jnp.sum(x * x, axis=-1, keepdims=True) + L2_EPS)


def _col_chunks(n, step=MATMUL_N_CHUNK):
    return [(c, min(c + step, n)) for c in range(0, n, step)]


def _ffn_ln_body(*refs, d_ff, alpha, n_in, has_mixer, n_prompt_tiles):
    x_refs, refs = refs[:n_in], refs[n_in:]
    is_prompt = pl.program_id(0) < n_prompt_tiles
    if has_mixer:
        (ogp_ref, ogs_ref, wo_ref, g1_ref, b1_ref), refs = refs[:5], refs[5:]
    wgu_ref, wd_ref, g_ref, b_ref = refs[:4]
    o_refs, h_ref = refs[4:-1], refs[-1]
    x = x_refs[0][...]
    if n_in == 2:
        x = jnp.where(is_prompt, x, x_refs[1][...])
    if has_mixer:
        og = jnp.where(is_prompt, ogp_ref[...], ogs_ref[...])
        x = _layer_norm_rows(alpha * x + jnp.dot(og, wo_ref[...], preferred_element_type=F32),
                             g1_ref[...], b1_ref[...])
    xb = x.astype(BF16)
    for c0, c1 in _col_chunks(d_ff):
        gate = jnp.dot(xb, wgu_ref[:, c0:c1], preferred_element_type=F32)
        up = jnp.dot(xb, wgu_ref[:, d_ff + c0:d_ff + c1], preferred_element_type=F32)
        h_ref[:, c0:c1] = (_silu(gate) * up).astype(BF16)
    y = jnp.dot(h_ref[...], wd_ref[...], preferred_element_type=F32)
    out = _layer_norm_rows(alpha * x + FFN_RES * y, g_ref[...], b_ref[...])
    if len(o_refs) == 1:
        o_refs[0][...] = out
    else:
        o_refs[1][...] = out

        @pl.when(is_prompt)
        def _():
            o_refs[0][...] = out


def _ffn_ln(xs, w_gu, w_d, g, b, *, tm, alpha, n_prompt, mixer=None, split_out=False):
    xs = xs if isinstance(xs, (tuple, list)) else (xs,)
    d = xs[0].shape[1]
    n = sum(x.shape[0] for x in xs)
    d_ff = w_d.shape[0]
    ntp = n_prompt // tm

    def pair(width):
        return [pl.BlockSpec((tm, width), lambda i: (jnp.minimum(i, ntp - 1), 0)),
                pl.BlockSpec((tm, width), lambda i: (jnp.maximum(i - ntp, 0), 0))]

    tile = pl.BlockSpec((tm, d), lambda i: (i, 0))
    vec = _resident((1, d))
    pair_shape = [jax.ShapeDtypeStruct((n_prompt, d), F32), jax.ShapeDtypeStruct((n - n_prompt, d), F32)]
    mixer_specs, mixer_args = [], []
    if mixer is not None:
        og_prompt, og_sample, w_o, g1, b1 = mixer
        mixer_specs = pair(w_o.shape[0]) + [_resident(w_o.shape), vec, vec]
        mixer_args = [og_prompt, og_sample, w_o, g1.reshape(1, d), b1.reshape(1, d)]
    return pl.pallas_call(
        functools.partial(_ffn_ln_body, d_ff=d_ff, alpha=alpha, n_in=len(xs),
                          has_mixer=mixer is not None, n_prompt_tiles=ntp),
        grid=(n // tm,),
        in_specs=(pair(d) if len(xs) == 2 else [tile]) + mixer_specs + [
            _resident(w_gu.shape), _resident(w_d.shape), vec, vec],
        out_specs=pair(d) if split_out else tile,
        out_shape=pair_shape if split_out else jax.ShapeDtypeStruct((n, d), F32),
        scratch_shapes=[pltpu.VMEM((tm, d_ff), BF16)],
        compiler_params=_cparams(1),
        name="ffn_ln",
    )(*xs, *mixer_args, w_gu, w_d, g.reshape(1, d), b.reshape(1, d))


def _gla_proj_body(x_ref, win_ref, wa1_ref, wa2_ref, ba_ref, p_ref, la_ref, *, qk_w, q_scale):
    xb = x_ref[...].astype(BF16)
    n_out = win_ref.shape[1]
    p_ref[:, 0:qk_w] = jnp.dot(xb, win_ref[:, 0:qk_w], preferred_element_type=F32) * q_scale
    rest = [(qk_w, 2 * qk_w)] + [(a + 2 * qk_w, b + 2 * qk_w)
                                  for a, b in _col_chunks(n_out - 2 * qk_w)]
    for c0, c1 in rest:
        p_ref[:, c0:c1] = jnp.dot(xb, win_ref[:, c0:c1], preferred_element_type=F32)
    low = jnp.dot(xb, wa1_ref[...], preferred_element_type=F32)
    logits = _dot(low, wa2_ref[...]) + ba_ref[...]
    log_sig = jnp.minimum(logits, 0.0) - jnp.log1p(jnp.exp(-jnp.abs(logits)))
    la_ref[...] = log_sig / GLA_GATE_TAU


def _gla_proj(x, w_in, w_a1, w_a2, b_a, *, tm, qk_w, q_scale):
    n, d = x.shape
    n_out = w_in.shape[1]
    return pl.pallas_call(
        functools.partial(_gla_proj_body, qk_w=qk_w, q_scale=q_scale),
        grid=(n // tm,),
        in_specs=[pl.BlockSpec((tm, d), lambda i: (i, 0)), _resident(w_in.shape),
                  _resident(w_a1.shape), _resident(w_a2.shape), _resident((1, qk_w))],
        out_specs=[pl.BlockSpec((tm, n_out), lambda i: (i, 0)),
                   pl.BlockSpec((tm, qk_w), lambda i: (i, 0))],
        out_shape=[jax.ShapeDtypeStruct((n, n_out), F32), jax.ShapeDtypeStruct((n, qk_w), F32)],
        compiler_params=_cparams(1),
        name="gla_proj",
    )(x, w_in, w_a1, w_a2, b_a.reshape(1, qk_w))


def _gdn_proj_body(x_ref, wmain_ref, wab_ref, alog_ref, dtb_ref, p_ref, gb_ref):
    xb = x_ref[...].astype(BF16)
    for c0, c1 in _col_chunks(wmain_ref.shape[1]):
        p_ref[:, c0:c1] = jnp.dot(xb, wmain_ref[:, c0:c1], preferred_element_type=F32)
    ab = lax.dot_general(wab_ref[...], xb, (((1,), (1,)), ((), ())), preferred_element_type=F32)
    kind = _iota2(ab.shape, 0) & (SUBLANES - 1)
    g = -jnp.exp(alog_ref[...]) * _softplus(ab + dtb_ref[...])
    beta = jax.nn.sigmoid(ab)
    gb_ref[...] = jnp.where(kind < 2, g, jnp.where(kind < 4, beta, 0.0))


def _gdn_proj(x, w_main, w_ab_t, a_log_rows, dt_bias_rows, *, tm):
    n, d = x.shape
    n_out = w_main.shape[1]
    n_rows = w_ab_t.shape[0]
    return pl.pallas_call(
        _gdn_proj_body,
        grid=(n // tm,),
        in_specs=[pl.BlockSpec((tm, d), lambda i: (i, 0)), _resident(w_main.shape),
                  _resident(w_ab_t.shape), _resident((n_rows, 1)), _resident((n_rows, 1))],
        out_specs=[pl.BlockSpec((tm, n_out), lambda i: (i, 0)),
                   pl.BlockSpec((n_rows, tm), lambda i: (0, i))],
        out_shape=[jax.ShapeDtypeStruct((n, n_out), F32), jax.ShapeDtypeStruct((n_rows, n), F32)],
        compiler_params=_cparams(1),
        name="gdn_proj",
    )(x, w_main, w_ab_t, a_log_rows, dt_bias_rows)


def _iota2(shape, dim):
    return lax.broadcasted_iota(jnp.int32, shape, dim)


def _block_of(idx, block):
    return idx >> (block.bit_length() - 1)


def _block_cumsum(x, block, axis):
    pos = _iota2(x.shape, axis) & (block - 1)
    shift = 1
    while shift < block:
        x = x + jnp.where(pos >= shift, pltpu.roll(x, shift, axis), 0.0)
        shift *= 2
    return x


def _col_form(row):
    return jnp.transpose(jnp.broadcast_to(row, (LANES, LANES)))


def _segments(block):
    return [(r, r + block) for r in range(0, LANES, block)]


def _last_row_of_segments(x, block):
    return jnp.concatenate(
        [jnp.broadcast_to(x[r1 - 1:r1, :], (r1 - r0, x.shape[1])) for r0, r1 in _segments(block)],
        axis=0)


def _rows_only(x, r0, r1):
    rows = _iota2(x.shape, 0)
    return jnp.where((rows >= r0) & (rows < r1), x, 0.0)


def _gla_tiles(heads, ng, block, chained):
    row = _iota2((LANES, LANES), 0)
    col = _iota2((LANES, LANES), 1)
    causal = (_block_of(row, block) == _block_of(col, block)) & (row >= col)
    segs = _segments(block)
    bcums = [_block_cumsum(la, block, 0) for (_, _, _, la, _, _) in heads]
    q_decs = [q * jnp.exp(b) for (q, _, _, _, _, _), b in zip(heads, bcums)]
    atts = [jnp.where(causal, _dot_nt(qd, k * jnp.exp(-b)), 0.0)
            for qd, (_, k, _, _, _, _), b in zip(q_decs, heads, bcums)]
    intras = [_dot(att, v) for att, (_, _, v, _, _, _) in zip(atts, heads)]
    k_end_ts = [jnp.transpose(k * jnp.exp(_last_row_of_segments(b, block) - b))
                for (_, k, _, _, _, _), b in zip(heads, bcums)]
    lane_reps = heads[0][2].shape[1] // LANES
    states = [list(st) for (_, _, _, _, _, st) in heads]
    cur = [st[0] for st in states]
    outs = [[] for _ in heads]
    leaving = [[] for _ in heads]
    for s, (r0, r1) in enumerate(segs):
        if not chained:
            cur = [st[s] for st in states]
        inter = [_dot(qd[r0:r1], c) for qd, c in zip(q_decs, cur)]
        upd = [_dot(kt, _rows_only(v, r0, r1)) for kt, (_, _, v, _, _, _) in zip(k_end_ts, heads)]
        for i, b in enumerate(bcums):
            outs[i].append(intras[i][r0:r1] + inter[i])
            decay = jnp.concatenate([_col_form(jnp.exp(b[r1 - 1:r1, :]))] * lane_reps, axis=1)
            cur[i] = decay * cur[i] + upd[i]
            if not chained:
                leaving[i].append(cur[i])
    results = []
    for i, (_, _, _, _, r, _) in enumerate(heads):
        o = _rms_norm_rows(jnp.concatenate(outs[i], axis=0), ng)
        results.append((o * _silu(r), leaving[i] if not chained else [cur[i]]))
    return results


def _gla_prompt_body(q_ref, k_ref, v_ref, r_ref, la_ref, ng_ref, og_ref, so_ref, s_ref, *,
                     n_tiles, n_heads, dk, dv):
    seq_block = pl.program_id(1)

    @pl.when(seq_block == 0)
    def _():
        s_ref[...] = jnp.zeros_like(s_ref)

    def tile(t, carry):
        rows = pl.ds(pl.multiple_of(t * LANES, LANES), LANES)
        heads = []
        for h in range(n_heads):
            ck = slice(h * dk, (h + 1) * dk)
            cv = slice(h * dv, (h + 1) * dv)
            heads.append((q_ref[rows, ck], k_ref[rows, ck], v_ref[rows, cv], la_ref[rows, ck],
                          r_ref[rows, cv], [s_ref[h]]))
        for h, (o, leaving) in enumerate(_gla_tiles(heads, ng_ref[...], GLA_CHUNK, True)):
            og_ref[rows, h * dv:(h + 1) * dv] = o.astype(BF16)
            s_ref[h] = leaving[0]
        return carry

    lax.fori_loop(0, n_tiles, tile, 0)

    @pl.when(seq_block == pl.num_programs(1) - 1)
    def _():
        so_ref[...] = s_ref[...]


def _gla_sample_body(q_ref, k_ref, v_ref, r_ref, la_ref, ng_ref, s_ref, og_ref, so_ref, *, block):
    n_seq = LANES // block
    [(o, leaving)] = _gla_tiles(
        [(q_ref[...], k_ref[...], v_ref[...], la_ref[...], r_ref[...],
          [s_ref[i] for i in range(n_seq)])], ng_ref[...], block, False)
    og_ref[...] = o.astype(BF16)
    for i in range(n_seq):
        so_ref[i] = leaving[i]


def _gla_recurrence(p, la, norm_g, state_in, layer, *, n_prompt, batch, heads, dk, dv, dec_seq):
    n = p.shape[0]
    t = n_prompt // batch
    qk_w, v_w = heads * dk, heads * dv
    v_blk0 = (2 * qk_w) // dv
    r_blk0 = (2 * qk_w + v_w) // dv
    ng = norm_g.reshape(1, dv)

    sb = math.gcd(GLA_SEQ_BLOCK, t)
    nsb = t // sb
    og_p, s_p = pl.pallas_call(
        functools.partial(_gla_prompt_body, n_tiles=sb // LANES, n_heads=heads, dk=dk, dv=dv),
        grid=(batch, nsb),
        in_specs=[pl.BlockSpec((sb, qk_w), lambda b, s: (b * nsb + s, 0)),
                  pl.BlockSpec((sb, qk_w), lambda b, s: (b * nsb + s, 1)),
                  pl.BlockSpec((sb, v_w), lambda b, s: (b * nsb + s, (2 * qk_w) // v_w)),
                  pl.BlockSpec((sb, v_w), lambda b, s: (b * nsb + s, (2 * qk_w + v_w) // v_w)),
                  pl.BlockSpec((sb, qk_w), lambda b, s: (b * nsb + s, 0)),
                  _resident((1, dv))],
        out_specs=[pl.BlockSpec((sb, v_w), lambda b, s: (b * nsb + s, 0)),
                   pl.BlockSpec((None, heads, dk, dv), lambda b, s: (b, 0, 0, 0))],
        out_shape=[jax.ShapeDtypeStruct((n_prompt, v_w), BF16),
                   jax.ShapeDtypeStruct((batch, heads, dk, dv), F32)],
        scratch_shapes=[pltpu.VMEM((heads, dk, dv), F32)],
        compiler_params=_cparams(2),
        name="gla_prompt",
    )(p, p, p, p, la, ng)

    n_sample = n - n_prompt
    dec_batch = n_sample // dec_seq
    seqs = LANES // dec_seq
    off = n_prompt // LANES
    og_s, s_s = pl.pallas_call(
        functools.partial(_gla_sample_body, block=dec_seq),
        grid=(n_sample // LANES, heads),
        in_specs=[pl.BlockSpec((LANES, dk), lambda g, h: (off + g, h)),
                  pl.BlockSpec((LANES, dk), lambda g, h: (off + g, heads + h)),
                  pl.BlockSpec((LANES, dv), lambda g, h: (off + g, v_blk0 + h)),
                  pl.BlockSpec((LANES, dv), lambda g, h: (off + g, r_blk0 + h)),
                  pl.BlockSpec((LANES, dk), lambda g, h: (off + g, h)),
                  _resident((1, dv)),
                  pl.BlockSpec((seqs, None, None, dk, dv), lambda g, h: (g, layer, h, 0, 0))],
        out_specs=[pl.BlockSpec((LANES, dv), lambda g, h: (g, h)),
                   pl.BlockSpec((seqs, None, dk, dv), lambda g, h: (g, h, 0, 0))],
        out_shape=[jax.ShapeDtypeStruct((n_sample, v_w), BF16),
                   jax.ShapeDtypeStruct((dec_batch, heads, dk, dv), F32)],
        compiler_params=_cparams(2),
        name="gla_sample",
    )(p, p, p, p, la, ng, state_in)
    return og_p, og_s, s_p, s_s


def _causal_conv(prev8, x, w):
    n = x.shape[0]
    xs = jnp.concatenate([prev8, x], axis=0)
    acc = xs[SUBLANES:, :] * w[CONV_TAPS - 1:CONV_TAPS, :]
    for j in range(CONV_TAPS - 1):
        shifted = pltpu.roll(xs, CONV_TAPS - 1 - j, 0)
        acc = acc + shifted[SUBLANES:SUBLANES + n, :] * w[j:j + 1, :]
    return acc


def _inverse_level_masks(block):
    idx = np.arange(LANES)
    masks = []
    size = 1
    while size < block:
        pair = (idx[:, None] // (2 * size)) == (idx[None, :] // (2 * size))
        masks.append(pair & ((idx[:, None] // size) != (idx[None, :] // size)))
        size *= 2
    return jnp.asarray(np.stack(masks).astype(np.float32))


def _unit_lower_inverses(mats, masks_ref):
    row = _iota2((LANES, LANES), 0)
    col = _iota2((LANES, LANES), 1)
    eye = jnp.where(row == col, 1.0, 0.0)
    xs = [eye - a * masks_ref[0] for a in mats]
    mats_b = [a.astype(BF16) for a in mats]
    for level in range(1, masks_ref.shape[0]):
        mask = masks_ref[level]
        xbs = [x.astype(BF16) for x in xs]
        ys = [jnp.dot(xb, ab, preferred_element_type=F32) for xb, ab in zip(xbs, mats_b)]
        xs = [x - jnp.dot(y.astype(BF16), xb, preferred_element_type=F32) * mask
              for x, y, xb in zip(xs, ys, xbs)]
    return xs


def _gdn_intra(pairs, block, hd, masks_ref):
    row = _iota2((LANES, LANES), 0)
    col = _iota2((LANES, LANES), 1)
    same = _block_of(row, block) == _block_of(col, block)
    incl = same & (row >= col)
    strict = same & (row > col)
    kks = [_dot_nt(k, k) for (_, k, _, _, _) in pairs]
    qks = [_dot_nt(q, k) for (q, k, _, _, _) in pairs]
    heads = []
    for (q, k, v, z, gb), kk, qk in zip(pairs, kks, qks):
        gcum = _block_cumsum(gb, block, 1)
        for e in range(2):
            g_row = gcum[e:e + 1, :]
            g_col = _col_form(g_row)
            b_col = _col_form(gb[2 + e:3 + e, :])
            decay = jnp.where(incl, jnp.exp(g_col - jnp.broadcast_to(g_row, (LANES, LANES))), 0.0)
            eg = jnp.exp(g_col)
            heads.append(dict(
                q_dec=q * eg, att=qk * decay, g_col=g_col,
                a=jnp.where(strict, b_col * kk * decay, 0.0),
                rhs=jnp.concatenate([k * (b_col * eg), v[:, e * hd:(e + 1) * hd] * b_col], axis=1),
                k_end=k * jnp.exp(_last_row_of_segments(g_col, block) - g_col),
                z=z[:, e * hd:(e + 1) * hd]))
    t_invs = _unit_lower_inverses([h["a"] for h in heads], masks_ref)
    for h, t_inv in zip(heads, t_invs):
        h["wu"] = _dot(t_inv, h["rhs"])
    for h in heads:
        h["k_end_t"] = jnp.transpose(h["k_end"])
    return heads


def _gdn_inter(heads, states, ng, block, hd):
    segs = _segments(block)
    ws_qs = [[_dot(jnp.concatenate([h["wu"][r0:r1, :hd], h["q_dec"][r0:r1]], axis=0), st[s])
              for s, (r0, r1) in enumerate(segs)] for h, st in zip(heads, states)]
    v_news = []
    for h, per_seg in zip(heads, ws_qs):
        parts = [h["wu"][r0:r1, hd:] - x[:r1 - r0] for (r0, r1), x in zip(segs, per_seg)]
        v_news.append(jnp.concatenate(parts, axis=0) if len(parts) > 1 else parts[0])
    o_intra = [_dot(h["att"], v_new) for h, v_new in zip(heads, v_news)]
    updates = [[_dot(h["k_end_t"], v_new if len(segs) == 1 else _rows_only(v_new, r0, r1))
                for (r0, r1) in segs] for h, v_new in zip(heads, v_news)]
    outs, leaving = [], []
    for h, st, per_seg, oi, upd in zip(heads, states, ws_qs, o_intra, updates):
        qs = [x[r1 - r0:] for (r0, r1), x in zip(segs, per_seg)]
        o = oi + (jnp.concatenate(qs, axis=0) if len(qs) > 1 else qs[0])
        outs.append(_rms_norm_rows(o, ng) * _silu(h["z"]))
        leaving.append([jnp.exp(h["g_col"][r1 - 1:r1, :]) * st[s] + upd[s]
                        for s, (r0, r1) in enumerate(segs)])
    return outs, leaving


def _gdn_prompt_body(q_ref, k_ref, v_ref, z_ref, gb_ref, wq_ref, wk_ref, wv_ref, ng_ref, masks_ref,
                     og_ref, so_ref, s_ref, pq_ref, pk_ref, pv_ref, *, n_tiles, n_pairs, q_scale, hd):
    seq_block = pl.program_id(2)

    @pl.when(seq_block == 0)
    def _():
        s_ref[...] = jnp.zeros_like(s_ref)
        pq_ref[...] = jnp.zeros_like(pq_ref)
        pk_ref[...] = jnp.zeros_like(pk_ref)
        pv_ref[...] = jnp.zeros_like(pv_ref)

    n_heads = 2 * n_pairs

    def tile(t, carry):
        r0 = pl.multiple_of(t * LANES, LANES)
        rows = pl.ds(r0, LANES)
        prev = pl.ds(pl.multiple_of(jnp.maximum(r0 - SUBLANES, 0), SUBLANES), SUBLANES)

        def conv(ref, prev_ref, w_ref, cols):
            prev8 = jnp.where(t > 0, ref[prev, cols], prev_ref[:, cols])
            return _silu(_causal_conv(prev8, ref[rows, cols], w_ref[:, cols]))

        pairs = []
        for p in range(n_pairs):
            c1 = slice(p * hd, (p + 1) * hd)
            c2 = slice(2 * p * hd, 2 * (p + 1) * hd)
            q = _l2_norm_rows(conv(q_ref, pq_ref, wq_ref, c1)) * q_scale
            k = _l2_norm_rows(conv(k_ref, pk_ref, wk_ref, c1))
            v = conv(v_ref, pv_ref, wv_ref, c2)
            pairs.append((q, k, v, z_ref[rows, c2], gb_ref[p * SUBLANES:(p + 1) * SUBLANES, rows]))
        heads = _gdn_intra(pairs, LANES, hd, masks_ref)
        outs, leaving = _gdn_inter(heads, [[s_ref[h]] for h in range(n_heads)], ng_ref[...],
                                   LANES, hd)
        for h in range(n_heads):
            og_ref[rows, h * hd:(h + 1) * hd] = outs[h].astype(BF16)
            s_ref[h] = leaving[h][0]
        return carry

    lax.fori_loop(0, n_tiles, tile, 0)
    tail = pl.ds(n_tiles * LANES - SUBLANES, SUBLANES)
    pq_ref[...] = q_ref[tail, :]
    pk_ref[...] = k_ref[tail, :]
    pv_ref[...] = v_ref[tail, :]

    @pl.when(seq_block == pl.num_programs(2) - 1)
    def _():
        so_ref[...] = s_ref[...]


def _gdn_sample_body(q_ref, k_ref, v_ref, z_ref, gb_ref, wq_ref, wk_ref, wv_ref, ng_ref, masks_ref,
                     hq_ref, hk_ref, hv_ref, s_ref, og_ref, so_ref, *, block, q_scale, hd):
    segs = _segments(block)

    def conv(ref, hist_ref, w_ref):
        x = ref[...]
        w = w_ref[...]
        return _silu(jnp.concatenate(
            [_causal_conv(hist_ref[s], x[r0:r1], w) for s, (r0, r1) in enumerate(segs)], axis=0))

    q = _l2_norm_rows(conv(q_ref, hq_ref, wq_ref)) * q_scale
    k = _l2_norm_rows(conv(k_ref, hk_ref, wk_ref))
    v = conv(v_ref, hv_ref, wv_ref)
    heads = _gdn_intra([(q, k, v, z_ref[...], gb_ref[...])], block, hd, masks_ref)
    states = [[s_ref[s, e] for s in range(len(segs))] for e in range(2)]
    outs, leaving = _gdn_inter(heads, states, ng_ref[...], block, hd)
    og_ref[...] = jnp.concatenate(outs, axis=1).astype(BF16)
    for e in range(2):
        for s in range(len(segs)):
            so_ref[s, e] = leaving[e][s]


def _gdn_recurrence(p, gb, conv_w, norm_g, hist8, state_in, layer, *,
                    n_prompt, batch, hq, hd, dec_seq):
    n = p.shape[0]
    t = n_prompt // batch
    key_w = hq * hd
    val_w = 2 * key_w
    q_scale = hd ** -0.5
    v_blk0 = (2 * key_w) // (2 * hd)
    z_blk0 = (2 * key_w + val_w) // (2 * hd)
    ng = norm_g.reshape(1, hd)
    w_specs = [pl.BlockSpec((CONV_TAPS, hd), lambda a, h: (0, h)),
               pl.BlockSpec((CONV_TAPS, hd), lambda a, h: (0, hq + h)),
               pl.BlockSpec((CONV_TAPS, 2 * hd), lambda a, h: (0, v_blk0 + h))]

    gp = math.gcd(GDN_PAIRS_PER_STEP, hq)
    sb = math.gcd(GDN_SEQ_BLOCK, t)
    nsb = t // sb
    qw, vw = gp * hd, 2 * gp * hd
    k_blk0 = key_w // qw
    masks_p = _inverse_level_masks(LANES)
    og_p, s_p = pl.pallas_call(
        functools.partial(_gdn_prompt_body, n_tiles=sb // LANES, n_pairs=gp, q_scale=q_scale, hd=hd),
        grid=(batch, hq // gp, nsb),
        in_specs=[pl.BlockSpec((sb, qw), lambda b, g, s: (b * nsb + s, g)),
                  pl.BlockSpec((sb, qw), lambda b, g, s: (b * nsb + s, k_blk0 + g)),
                  pl.BlockSpec((sb, vw), lambda b, g, s: (b * nsb + s, k_blk0 + g)),
                  pl.BlockSpec((sb, vw), lambda b, g, s: (b * nsb + s, 2 * k_blk0 + g)),
                  pl.BlockSpec((gp * SUBLANES, sb), lambda b, g, s: (g, b * nsb + s)),
                  pl.BlockSpec((CONV_TAPS, qw), lambda b, g, s: (0, g)),
                  pl.BlockSpec((CONV_TAPS, qw), lambda b, g, s: (0, k_blk0 + g)),
                  pl.BlockSpec((CONV_TAPS, vw), lambda b, g, s: (0, k_blk0 + g)),
                  _resident((1, hd)), _resident(masks_p.shape)],
        out_specs=[pl.BlockSpec((sb, vw), lambda b, g, s: (b * nsb + s, g)),
                   pl.BlockSpec((None, 2 * gp, hd, hd), lambda b, g, s: (b, g, 0, 0))],
        out_shape=[jax.ShapeDtypeStruct((n_prompt, val_w), BF16),
                   jax.ShapeDtypeStruct((batch, 2 * hq, hd, hd), F32)],
        scratch_shapes=[pltpu.VMEM((2 * gp, hd, hd), F32), pltpu.VMEM((SUBLANES, qw), F32),
                        pltpu.VMEM((SUBLANES, qw), F32), pltpu.VMEM((SUBLANES, vw), F32)],
        compiler_params=_cparams(3),
        name="gdn_prompt",
    )(p, p, p, p, gb, conv_w, conv_w, conv_w, ng, masks_p)

    n_sample = n - n_prompt
    dec_batch = n_sample // dec_seq
    seqs = LANES // dec_seq
    off = n_prompt // LANES
    masks_s = _inverse_level_masks(dec_seq)
    og_s, s_s = pl.pallas_call(
        functools.partial(_gdn_sample_body, block=dec_seq, q_scale=q_scale, hd=hd),
        grid=(n_sample // LANES, hq),
        in_specs=[pl.BlockSpec((LANES, hd), lambda g, h: (off + g, h)),
                  pl.BlockSpec((LANES, hd), lambda g, h: (off + g, hq + h)),
                  pl.BlockSpec((LANES, 2 * hd), lambda g, h: (off + g, v_blk0 + h)),
                  pl.BlockSpec((LANES, 2 * hd), lambda g, h: (off + g, z_blk0 + h)),
                  pl.BlockSpec((SUBLANES, LANES), lambda g, h: (h, off + g))] + w_specs + [
                  _resident((1, hd)), _resident(masks_s.shape),
                  pl.BlockSpec((seqs, SUBLANES, hd), lambda g, h: (g, 0, h)),
                  pl.BlockSpec((seqs, SUBLANES, hd), lambda g, h: (g, 0, hq + h)),
                  pl.BlockSpec((seqs, SUBLANES, 2 * hd), lambda g, h: (g, 0, v_blk0 + h)),
                  pl.BlockSpec((seqs, None, 2, hd, hd), lambda g, h: (g, layer, h, 0, 0))],
        out_specs=[pl.BlockSpec((LANES, 2 * hd), lambda g, h: (g, h)),
                   pl.BlockSpec((seqs, 2, hd, hd), lambda g, h: (g, h, 0, 0))],
        out_shape=[jax.ShapeDtypeStruct((n_sample, val_w), BF16),
                   jax.ShapeDtypeStruct((dec_batch, 2 * hq, hd, hd), F32)],
        compiler_params=_cparams(2),
        name="gdn_sample",
    )(p, p, p, p, gb, conv_w, conv_w, conv_w, ng, masks_s, hist8, hist8, hist8, state_in)
    return og_p, og_s, s_p, s_s


def _gdn_gate_rows(w_in, a_log, dt_bias, n_main, hq):
    d = w_in.shape[0]
    hv = a_log.shape[0]
    rep = hv // hq
    pad = SUBLANES - 2 * rep
    wa = w_in[:, n_main:n_main + hv].T.reshape(hq, rep, d)
    wb = w_in[:, n_main + hv:n_main + 2 * hv].T.reshape(hq, rep, d)
    w_ab_t = jnp.concatenate([wa, wb, jnp.zeros((hq, pad, d), w_in.dtype)], axis=1)
    zeros = jnp.zeros((hq, SUBLANES - rep), F32)
    alog = jnp.concatenate([a_log.reshape(hq, rep), zeros], axis=1).reshape(hq * SUBLANES, 1)
    dtb = jnp.concatenate([dt_bias.reshape(hq, rep), zeros], axis=1).reshape(hq * SUBLANES, 1)
    return w_ab_t.reshape(hq * SUBLANES, d).astype(BF16), alog, dtb


def kernel(x_prompt, x_sample, state_gla, state_gdn, state_gdn_conv, ln_g, ln_b, ffn1_w_gu, ffn1_w_d, ffn2_w_gu, ffn2_w_d, gla_w_in, gla_w_a1, gla_w_a2, gla_b_a, gla_norm_g, gla_w_o, gdn_w_in, gdn_conv_w, gdn_a_log, gdn_dt_bias, gdn_norm_g, gdn_w_o):
    batch, t, d = x_prompt.shape
    dec_batch, dec_seq, _ = x_sample.shape
    depth = ln_g.shape[0]
    alpha = (2.0 * depth) ** 0.25
    n_prompt, n_sample = batch * t, dec_batch * dec_seq
    tm = math.gcd(TOKEN_TILE, math.gcd(n_prompt, n_sample))
    assert t % LANES == 0 and t % GLA_CHUNK == 0 and n_sample % LANES == 0 and LANES % dec_seq == 0
    assert tm % LANES == 0

    gla_heads, gla_dk, gla_dv = state_gla.shape[2:]
    gla_qk_w = gla_heads * gla_dk
    hv, hd = state_gdn.shape[2:4]
    conv_dim = gdn_conv_w.shape[2]
    val_w = hv * hd
    hq = (conv_dim - val_w) // (2 * hd)
    assert hv == 2 * hq, "the gated-delta kernel pairs two v heads with each q/k head"
    n_main = conv_dim + val_w

    x = (x_prompt.reshape(n_prompt, d), x_sample.reshape(n_sample, d))
    new_gla_p, new_gla_s, new_gdn_p, new_gdn_s, new_conv_p, new_conv_s = [], [], [], [], [], []
    for i in range(depth):
        x = _ffn_ln(x, ffn1_w_gu[i].astype(BF16), ffn1_w_d[i].astype(BF16), ln_g[i, 0], ln_b[i, 0],
                    tm=tm, alpha=alpha, n_prompt=n_prompt)
        j = i // 2
        if i % 2 == 0:
            rank = gla_w_a1.shape[2]
            w_a1 = jnp.pad(gla_w_a1[j], ((0, 0), (0, LANES - rank))).astype(BF16)
            w_a2 = jnp.pad(gla_w_a2[j], ((0, LANES - rank), (0, 0))).astype(BF16)
            p, la = _gla_proj(x, gla_w_in[j].astype(BF16), w_a1, w_a2, gla_b_a[j],
                              tm=tm, qk_w=gla_qk_w, q_scale=gla_dk ** -0.5)
            og_p, og_s, s_p, s_s = _gla_recurrence(
                p, la, gla_norm_g[j], state_gla, j, n_prompt=n_prompt, batch=batch,
                heads=gla_heads, dk=gla_dk, dv=gla_dv, dec_seq=dec_seq)
            new_gla_p.append(s_p)
            new_gla_s.append(s_s)
            w_o = gla_w_o[j]
        else:
            w_ab_t, alog_rows, dtb_rows = _gdn_gate_rows(gdn_w_in[j], gdn_a_log[j], gdn_dt_bias[j],
                                                         n_main, hq)
            p, gb = _gdn_proj(x, gdn_w_in[j][:, :n_main].astype(BF16), w_ab_t, alog_rows, dtb_rows,
                              tm=tm)
            hist8 = jnp.pad(state_gdn_conv[:, j],
                            ((0, 0), (SUBLANES - (CONV_TAPS - 1), 0), (0, 0)))
            og_p, og_s, s_p, s_s = _gdn_recurrence(
                p, gb, gdn_conv_w[j], gdn_norm_g[j], hist8, state_gdn, j, n_prompt=n_prompt,
                batch=batch, hq=hq, hd=hd, dec_seq=dec_seq)
            new_gdn_p.append(s_p)
            new_gdn_s.append(s_s)
            keep = np.arange(-(CONV_TAPS - 1), 0)
            rows_p = ((np.arange(batch) + 1) * t)[:, None] + keep
            rows_s = (n_prompt + (np.arange(dec_batch) + 1) * dec_seq)[:, None] + keep
            new_conv_p.append(jnp.take(p, rows_p.reshape(-1), axis=0)[:, :conv_dim]
                              .reshape(batch, CONV_TAPS - 1, conv_dim))
            new_conv_s.append(jnp.take(p, rows_s.reshape(-1), axis=0)[:, :conv_dim]
                              .reshape(dec_batch, CONV_TAPS - 1, conv_dim))
            w_o = gdn_w_o[j]
        x = _ffn_ln(x, ffn2_w_gu[i].astype(BF16), ffn2_w_d[i].astype(BF16), ln_g[i, 2], ln_b[i, 2],
                    tm=tm, alpha=alpha, n_prompt=n_prompt, split_out=(i == depth - 1),
                    mixer=(og_p, og_s, w_o.astype(BF16), ln_g[i, 1], ln_b[i, 1]))
    y_prompt, y_sample = x
    return (y_prompt.reshape(batch, t, d), y_sample.reshape(dec_batch, dec_seq, d),
            jnp.stack(new_gla_p, 1), jnp.stack(new_gdn_p, 1), jnp.stack(new_conv_p, 1),
            jnp.stack(new_gla_s, 1), jnp.stack(new_gdn_s, 1), jnp.stack(new_conv_s, 1))
```

```python
import functools
import math

import numpy as np

import jax
import jax.numpy as jnp
from jax import lax
from jax.experimental import pallas as pl
from jax.experimental.pallas import tpu as pltpu

F32 = jnp.float32
BF16 = jnp.bfloat16

LN_EPS = 1e-5
RMS_EPS = 1e-6
L2_EPS = 1e-6
FFN_RES = 0.5
GLA_GATE_TAU = 16.0
GLA_CHUNK = 64
CONV_TAPS = 4

LANES = 128
SUBLANES = 8
VMEM_LIMIT_BYTES = 56 * 1024 * 1024
TOKEN_TILE = 512
MATMUL_N_CHUNK = 1024
GDN_SEQ_BLOCK = 256
GLA_SEQ_BLOCK = 512


def _cparams(n_axes):
    return pltpu.CompilerParams(
        dimension_semantics=("arbitrary",) * n_axes, vmem_limit_bytes=VMEM_LIMIT_BYTES)


def _resident(shape):
    zeros = (0,) * len(shape)
    return pl.BlockSpec(shape, lambda *_: zeros, pipeline_mode=pl.Buffered(1))


def _dot(a, b):
    return jnp.dot(a.astype(BF16), b.astype(BF16), preferred_element_type=F32)


def _dot_nt(a, b):
    return lax.dot_general(a.astype(BF16), b.astype(BF16), (((1,), (1,)), ((), ())),
                           preferred_element_type=F32)


def _silu(x):
    return x * jax.nn.sigmoid(x)


def _softplus(x):
    return jnp.maximum(x, 0.0) + jnp.log1p(jnp.exp(-jnp.abs(x)))


def _layer_norm_rows(y, g, b):
    mu = jnp.mean(y, axis=-1, keepdims=True)
    d = y - mu
    var = jnp.mean(d * d, axis=-1, keepdims=True)
    return d * lax.rsqrt(var + LN_EPS) * g + b


def _rms_norm_rows(o, g):
    return o * lax.rsqrt(jnp.mean(o * o, axis=-1, keepdims=True) + RMS_EPS) * g


def _l2_norm_rows(x):
    return x * lax.rsqrt(jnp.sum(x * x, axis=-1, keepdims=True) + L2_EPS)


def _col_chunks(n, step=MATMUL_N_CHUNK):
    return [(c, min(c + step, n)) for c in range(0, n, step)]


def _ffn_ln_body(*refs, d_ff, alpha, n_in, has_mixer, n_prompt_tiles):
    x_refs, refs = refs[:n_in], refs[n_in:]
    is_prompt = pl.program_id(0) < n_prompt_tiles
    if has_mixer:
        (ogp_ref, ogs_ref, wo_ref, g1_ref, b1_ref), refs = refs[:5], refs[5:]
    wgu_ref, wd_ref, g_ref, b_ref = refs[:4]
    o_refs, h_ref = refs[4:-1], refs[-1]
    x = x_refs[0][...]
    if n_in == 2:
        x = jnp.where(is_prompt, x, x_refs[1][...])
    if has_mixer:
        og = jnp.where(is_prompt, ogp_ref[...], ogs_ref[...])
        x = _layer_norm_rows(alpha * x + jnp.dot(og, wo_ref[...], preferred_element_type=F32),
                             g1_ref[...], b1_ref[...])
    xb = x.astype(BF16)
    for c0, c1 in _col_chunks(d_ff):
        gate = jnp.dot(xb, wgu_ref[:, c0:c1], preferred_element_type=F32)
        up = jnp.dot(xb, wgu_ref[:, d_ff + c0:d_ff + c1], preferred_element_type=F32)
        h_ref[:, c0:c1] = (_silu(gate) * up).astype(BF16)
    y = jnp.dot(h_ref[...], wd_ref[...], preferred_element_type=F32)
    out = _layer_norm_rows(alpha * x + FFN_RES * y, g_ref[...], b_ref[...])
    if len(o_refs) == 1:
        o_refs[0][...] = out
    else:
        o_refs[1][...] = out

        @pl.when(is_prompt)
        def _():
            o_refs[0][...] = out


def _ffn_ln(xs, w_gu, w_d, g, b, *, tm, alpha, n_prompt, mixer=None, split_out=False):
    xs = xs if isinstance(xs, (tuple, list)) else (xs,)
    d = xs[0].shape[1]
    n = sum(x.shape[0] for x in xs)
    d_ff = w_d.shape[0]
    ntp = n_prompt // tm

    def pair(width):
        return [pl.BlockSpec((tm, width), lambda i: (jnp.minimum(i, ntp - 1), 0)),
                pl.BlockSpec((tm, width), lambda i: (jnp.maximum(i - ntp, 0), 0))]

    tile = pl.BlockSpec((tm, d), lambda i: (i, 0))
    vec = _resident((1, d))
    pair_shape = [jax.ShapeDtypeStruct((n_prompt, d), F32), jax.ShapeDtypeStruct((n - n_prompt, d), F32)]
    mixer_specs, mixer_args = [], []
    if mixer is not None:
        og_prompt, og_sample, w_o, g1, b1 = mixer
        mixer_specs = pair(w_o.shape[0]) + [_resident(w_o.shape), vec, vec]
        mixer_args = [og_prompt, og_sample, w_o, g1.reshape(1, d), b1.reshape(1, d)]
    return pl.pallas_call(
        functools.partial(_ffn_ln_body, d_ff=d_ff, alpha=alpha, n_in=len(xs),
                          has_mixer=mixer is not None, n_prompt_tiles=ntp),
        grid=(n // tm,),
        in_specs=(pair(d) if len(xs) == 2 else [tile]) + mixer_specs + [
            _resident(w_gu.shape), _resident(w_d.shape), vec, vec],
        out_specs=pair(d) if split_out else tile,
        out_shape=pair_shape if split_out else jax.ShapeDtypeStruct((n, d), F32),
        scratch_shapes=[pltpu.VMEM((tm, d_ff), BF16)],
        compiler_params=_cparams(1),
        name="ffn_ln",
    )(*xs, *mixer_args, w_gu, w_d, g.reshape(1, d), b.reshape(1, d))


def _gla_proj_body(x_ref, win_ref, wa1_ref, wa2_ref, ba_ref, p_ref, la_ref, *, qk_w, q_scale):
    xb = x_ref[...].astype(BF16)
    n_out = win_ref.shape[1]
    p_ref[:, 0:qk_w] = jnp.dot(xb, win_ref[:, 0:qk_w], preferred_element_type=F32) * q_scale
    rest = [(qk_w, 2 * qk_w)] + [(a + 2 * qk_w, b + 2 * qk_w)
                                  for a, b in _col_chunks(n_out - 2 * qk_w)]
    for c0, c1 in rest:
        p_ref[:, c0:c1] = jnp.dot(xb, win_ref[:, c0:c1], preferred_element_type=F32)
    low = jnp.dot(xb, wa1_ref[...], preferred_element_type=F32)
    logits = _dot(low, wa2_ref[...]) + ba_ref[...]
    log_sig = jnp.minimum(logits, 0.0) - jnp.log1p(jnp.exp(-jnp.abs(logits)))
    la_ref[...] = log_sig / GLA_GATE_TAU


def _gla_proj(x, w_in, w_a1, w_a2, b_a, *, tm, qk_w, q_scale):
    n, d = x.shape
    n_out = w_in.shape[1]
    return pl.pallas_call(
        functools.partial(_gla_proj_body, qk_w=qk_w, q_scale=q_scale),
        grid=(n // tm,),
        in_specs=[pl.BlockSpec((tm, d), lambda i: (i, 0)), _resident(w_in.shape),
                  _resident(w_a1.shape), _resident(w_a2.shape), _resident((1, qk_w))],
        out_specs=[pl.BlockSpec((tm, n_out), lambda i: (i, 0)),
                   pl.BlockSpec((tm, qk_w), lambda i: (i, 0))],
        out_shape=[jax.ShapeDtypeStruct((n, n_out), F32), jax.ShapeDtypeStruct((n, qk_w), F32)],
        compiler_params=_cparams(1),
        name="gla_proj",
    )(x, w_in, w_a1, w_a2, b_a.reshape(1, qk_w))


def _gdn_gate_values(xb, wab, alog, dtb):
    ab = lax.dot_general(wab, xb, (((1,), (1,)), ((), ())), preferred_element_type=F32)
    kind = _iota2(ab.shape, 0) & (SUBLANES - 1)
    g = -jnp.exp(alog) * _softplus(ab + dtb)
    beta = jax.nn.sigmoid(ab)
    return jnp.where(kind < 2, g, jnp.where(kind < 4, beta, 0.0))


def _gdn_proj_body(x_ref, wmain_ref, wab_ref, alog_ref, dtb_ref, p_ref, gb_ref):
    xb = x_ref[...].astype(BF16)
    for c0, c1 in _col_chunks(wmain_ref.shape[1]):
        p_ref[:, c0:c1] = jnp.dot(xb, wmain_ref[:, c0:c1], preferred_element_type=F32)
    gb_ref[...] = _gdn_gate_values(xb, wab_ref[...], alog_ref[...], dtb_ref[...])


def _gdn_proj(x, w_main, w_ab_t, a_log_rows, dt_bias_rows, *, tm, first_tile, n_tiles):
    d = x.shape[1]
    n = n_tiles * tm
    n_out = w_main.shape[1]
    n_rows = w_ab_t.shape[0]
    return pl.pallas_call(
        _gdn_proj_body,
        grid=(n_tiles,),
        in_specs=[pl.BlockSpec((tm, d), lambda i: (first_tile + i, 0)), _resident(w_main.shape),
                  _resident(w_ab_t.shape), _resident((n_rows, 1)), _resident((n_rows, 1))],
        out_specs=[pl.BlockSpec((tm, n_out), lambda i: (i, 0)),
                   pl.BlockSpec((n_rows, tm), lambda i: (0, i))],
        out_shape=[jax.ShapeDtypeStruct((n, n_out), F32), jax.ShapeDtypeStruct((n_rows, n), F32)],
        compiler_params=_cparams(1),
        name="gdn_proj",
    )(x, w_main, w_ab_t, a_log_rows, dt_bias_rows)


def _iota2(shape, dim):
    return lax.broadcasted_iota(jnp.int32, shape, dim)


def _block_of(idx, block):
    return idx >> (block.bit_length() - 1)


def _block_cumsum(x, block, axis):
    pos = _iota2(x.shape, axis) & (block - 1)
    shift = 1
    while shift < block:
        x = x + jnp.where(pos >= shift, pltpu.roll(x, shift, axis), 0.0)
        shift *= 2
    return x


def _col_form(row):
    return jnp.transpose(jnp.broadcast_to(row, (LANES, LANES)))


def _segments(block):
    return [(r, r + block) for r in range(0, LANES, block)]


def _last_row_of_segments(x, block):
    return jnp.concatenate(
        [jnp.broadcast_to(x[r1 - 1:r1, :], (r1 - r0, x.shape[1])) for r0, r1 in _segments(block)],
        axis=0)


def _rows_only(x, r0, r1):
    rows = _iota2(x.shape, 0)
    return jnp.where((rows >= r0) & (rows < r1), x, 0.0)


def _gla_tiles(heads, ng, block, chained):
    row = _iota2((LANES, LANES), 0)
    col = _iota2((LANES, LANES), 1)
    causal = (_block_of(row, block) == _block_of(col, block)) & (row >= col)
    segs = _segments(block)
    bcums = [_block_cumsum(la, block, 0) for (_, _, _, la, _, _) in heads]
    q_decs = [q * jnp.exp(b) for (q, _, _, _, _, _), b in zip(heads, bcums)]
    atts = [jnp.where(causal, _dot_nt(qd, k * jnp.exp(-b)), 0.0)
            for qd, (_, k, _, _, _, _), b in zip(q_decs, heads, bcums)]
    intras = [_dot(att, v) for att, (_, _, v, _, _, _) in zip(atts, heads)]
    k_end_ts = [jnp.transpose(k * jnp.exp(_last_row_of_segments(b, block) - b))
                for (_, k, _, _, _, _), b in zip(heads, bcums)]
    lane_reps = heads[0][2].shape[1] // LANES
    states = [list(st) for (_, _, _, _, _, st) in heads]
    cur = [st[0] for st in states]
    outs = [[] for _ in heads]
    leaving = [[] for _ in heads]
    for s, (r0, r1) in enumerate(segs):
        if not chained:
            cur = [st[s] for st in states]
        inter = [_dot(qd[r0:r1], c) for qd, c in zip(q_decs, cur)]
        upd = [_dot(kt, _rows_only(v, r0, r1)) for kt, (_, _, v, _, _, _) in zip(k_end_ts, heads)]
        for i, b in enumerate(bcums):
            outs[i].append(intras[i][r0:r1] + inter[i])
            decay = jnp.concatenate([_col_form(jnp.exp(b[r1 - 1:r1, :]))] * lane_reps, axis=1)
            cur[i] = decay * cur[i] + upd[i]
            if not chained:
                leaving[i].append(cur[i])
    results = []
    for i, (_, _, _, _, r, _) in enumerate(heads):
        o = _rms_norm_rows(jnp.concatenate(outs[i], axis=0), ng)
        results.append((o * _silu(r), leaving[i] if not chained else [cur[i]]))
    return results


def _gla_prompt_body(q_ref, k_ref, v_ref, r_ref, la_ref, ng_ref, og_ref, so_ref, s_ref, *,
                     n_tiles, n_heads, dk, dv):
    seq_block = pl.program_id(1)

    @pl.when(seq_block == 0)
    def _():
        s_ref[...] = jnp.zeros_like(s_ref)

    def tile(t, carry):
        rows = pl.ds(pl.multiple_of(t * LANES, LANES), LANES)
        heads = []
        for h in range(n_heads):
            ck = slice(h * dk, (h + 1) * dk)
            cv = slice(h * dv, (h + 1) * dv)
            heads.append((q_ref[rows, ck], k_ref[rows, ck], v_ref[rows, cv], la_ref[rows, ck],
                          r_ref[rows, cv], [s_ref[h]]))
        for h, (o, leaving) in enumerate(_gla_tiles(heads, ng_ref[...], GLA_CHUNK, True)):
            og_ref[rows, h * dv:(h + 1) * dv] = o.astype(BF16)
            s_ref[h] = leaving[0]
        return carry

    lax.fori_loop(0, n_tiles, tile, 0)

    @pl.when(seq_block == pl.num_programs(1) - 1)
    def _():
        so_ref[...] = s_ref[...]


def _gla_sample_body(q_ref, k_ref, v_ref, r_ref, la_ref, ng_ref, s_ref, og_ref, so_ref, *, block):
    n_seq = LANES // block
    [(o, leaving)] = _gla_tiles(
        [(q_ref[...], k_ref[...], v_ref[...], la_ref[...], r_ref[...],
          [s_ref[i] for i in range(n_seq)])], ng_ref[...], block, False)
    og_ref[...] = o.astype(BF16)
    for i in range(n_seq):
        so_ref[i] = leaving[i]


def _gla_recurrence(p, la, norm_g, state_in, layer, *, n_prompt, batch, heads, dk, dv, dec_seq):
    n = p.shape[0]
    t = n_prompt // batch
    qk_w, v_w = heads * dk, heads * dv
    v_blk0 = (2 * qk_w) // dv
    r_blk0 = (2 * qk_w + v_w) // dv
    ng = norm_g.reshape(1, dv)

    sb = math.gcd(GLA_SEQ_BLOCK, t)
    nsb = t // sb
    og_p, s_p = pl.pallas_call(
        functools.partial(_gla_prompt_body, n_tiles=sb // LANES, n_heads=heads, dk=dk, dv=dv),
        grid=(batch, nsb),
        in_specs=[pl.BlockSpec((sb, qk_w), lambda b, s: (b * nsb + s, 0)),
                  pl.BlockSpec((sb, qk_w), lambda b, s: (b * nsb + s, 1)),
                  pl.BlockSpec((sb, v_w), lambda b, s: (b * nsb + s, (2 * qk_w) // v_w)),
                  pl.BlockSpec((sb, v_w), lambda b, s: (b * nsb + s, (2 * qk_w + v_w) // v_w)),
                  pl.BlockSpec((sb, qk_w), lambda b, s: (b * nsb + s, 0)),
                  _resident((1, dv))],
        out_specs=[pl.BlockSpec((sb, v_w), lambda b, s: (b * nsb + s, 0)),
                   pl.BlockSpec((None, heads, dk, dv), lambda b, s: (b, 0, 0, 0))],
        out_shape=[jax.ShapeDtypeStruct((n_prompt, v_w), BF16),
                   jax.ShapeDtypeStruct((batch, heads, dk, dv), F32)],
        scratch_shapes=[pltpu.VMEM((heads, dk, dv), F32)],
        compiler_params=_cparams(2),
        name="gla_prompt",
    )(p, p, p, p, la, ng)

    n_sample = n - n_prompt
    dec_batch = n_sample // dec_seq
    seqs = LANES // dec_seq
    off = n_prompt // LANES
    og_s, s_s = pl.pallas_call(
        functools.partial(_gla_sample_body, block=dec_seq),
        grid=(n_sample // LANES, heads),
        in_specs=[pl.BlockSpec((LANES, dk), lambda g, h: (off + g, h)),
                  pl.BlockSpec((LANES, dk), lambda g, h: (off + g, heads + h)),
                  pl.BlockSpec((LANES, dv), lambda g, h: (off + g, v_blk0 + h)),
                  pl.BlockSpec((LANES, dv), lambda g, h: (off + g, r_blk0 + h)),
                  pl.BlockSpec((LANES, dk), lambda g, h: (off + g, h)),
                  _resident((1, dv)),
                  pl.BlockSpec((seqs, None, None, dk, dv), lambda g, h: (g, layer, h, 0, 0))],
        out_specs=[pl.BlockSpec((LANES, dv), lambda g, h: (g, h)),
                   pl.BlockSpec((seqs, None, dk, dv), lambda g, h: (g, h, 0, 0))],
        out_shape=[jax.ShapeDtypeStruct((n_sample, v_w), BF16),
                   jax.ShapeDtypeStruct((dec_batch, heads, dk, dv), F32)],
        compiler_params=_cparams(2),
        name="gla_sample",
    )(p, p, p, p, la, ng, state_in)
    return og_p, og_s, s_p, s_s


def _causal_conv(prev8, x, w):
    n = x.shape[0]
    xs = jnp.concatenate([prev8, x], axis=0)
    acc = xs[SUBLANES:, :] * w[CONV_TAPS - 1:CONV_TAPS, :]
    for j in range(CONV_TAPS - 1):
        shifted = pltpu.roll(xs, CONV_TAPS - 1 - j, 0)
        acc = acc + shifted[SUBLANES:SUBLANES + n, :] * w[j:j + 1, :]
    return acc


def _inverse_level_masks(block):
    idx = np.arange(LANES)
    masks = []
    size = 1
    while size < block:
        pair = (idx[:, None] // (2 * size)) == (idx[None, :] // (2 * size))
        masks.append(pair & ((idx[:, None] // size) != (idx[None, :] // size)))
        size *= 2
    return jnp.asarray(np.stack(masks).astype(np.float32))


def _unit_lower_inverses(mats, masks_ref):
    row = _iota2((LANES, LANES), 0)
    col = _iota2((LANES, LANES), 1)
    eye = jnp.where(row == col, 1.0, 0.0)
    xs = [eye - a * masks_ref[0] for a in mats]
    mats_b = [a.astype(BF16) for a in mats]
    for level in range(1, masks_ref.shape[0]):
        mask = masks_ref[level]
        xbs = [x.astype(BF16) for x in xs]
        ys = [jnp.dot(xb, ab, preferred_element_type=F32) for xb, ab in zip(xbs, mats_b)]
        xs = [x - jnp.dot(y.astype(BF16), xb, preferred_element_type=F32) * mask
              for x, y, xb in zip(xs, ys, xbs)]
    return xs


def _gdn_intra(pairs, block, hd, masks_ref):
    row = _iota2((LANES, LANES), 0)
    col = _iota2((LANES, LANES), 1)
    same = _block_of(row, block) == _block_of(col, block)
    incl = same & (row >= col)
    strict = same & (row > col)
    kks = [_dot_nt(k, k) for (_, k, _, _, _) in pairs]
    qks = [_dot_nt(q, k) for (q, k, _, _, _) in pairs]
    heads = []
    for (q, k, v, z, gb), kk, qk in zip(pairs, kks, qks):
        gcum = _block_cumsum(gb, block, 1)
        for e in range(2):
            g_row = gcum[e:e + 1, :]
            g_col = _col_form(g_row)
            b_col = _col_form(gb[2 + e:3 + e, :])
            decay = jnp.where(incl, jnp.exp(g_col - jnp.broadcast_to(g_row, (LANES, LANES))), 0.0)
            eg = jnp.exp(g_col)
            heads.append(dict(
                q_dec=q * eg, att=qk * decay, g_col=g_col,
                a=jnp.where(strict, b_col * kk * decay, 0.0),
                rhs=jnp.concatenate([k * (b_col * eg), v[:, e * hd:(e + 1) * hd] * b_col], axis=1),
                k_end=k * jnp.exp(_last_row_of_segments(g_col, block) - g_col),
                z=z[:, e * hd:(e + 1) * hd]))
    t_invs = _unit_lower_inverses([h["a"] for h in heads], masks_ref)
    for h, t_inv in zip(heads, t_invs):
        h["wu"] = _dot(t_inv, h["rhs"])
    for h in heads:
        h["k_end_t"] = jnp.transpose(h["k_end"])
    return heads


def _gdn_inter(heads, states, ng, block, hd):
    segs = _segments(block)
    ws_qs = [[_dot(jnp.concatenate([h["wu"][r0:r1, :hd], h["q_dec"][r0:r1]], axis=0), st[s])
              for s, (r0, r1) in enumerate(segs)] for h, st in zip(heads, states)]
    v_news = []
    for h, per_seg in zip(heads, ws_qs):
        parts = [h["wu"][r0:r1, hd:] - x[:r1 - r0] for (r0, r1), x in zip(segs, per_seg)]
        v_news.append(jnp.concatenate(parts, axis=0) if len(parts) > 1 else parts[0])
    o_intra = [_dot(h["att"], v_new) for h, v_new in zip(heads, v_news)]
    updates = [[_dot(h["k_end_t"], v_new if len(segs) == 1 else _rows_only(v_new, r0, r1))
                for (r0, r1) in segs] for h, v_new in zip(heads, v_news)]
    outs, leaving = [], []
    for h, st, per_seg, oi, upd in zip(heads, states, ws_qs, o_intra, updates):
        qs = [x[r1 - r0:] for (r0, r1), x in zip(segs, per_seg)]
        o = oi + (jnp.concatenate(qs, axis=0) if len(qs) > 1 else qs[0])
        outs.append(_rms_norm_rows(o, ng) * _silu(h["z"]))
        leaving.append([jnp.exp(h["g_col"][r1 - 1:r1, :]) * st[s] + upd[s]
                        for s, (r0, r1) in enumerate(segs)])
    return outs, leaving


def _gdn_prompt_body(x_ref, wg_ref, wab_ref, alog_ref, dtb_ref, cw_ref, ng_ref, masks_ref,
                     og_ref, so_ref, tail_ref, p0_ref, p1_ref, gb0_ref, gb1_ref, xb_ref, s_ref, prev_ref,
                     *, blocks_per_seq, n_pairs, q_scale, hd):
    step = pl.program_id(0)
    n_groups, sb, group_w = p0_ref.shape
    n_heads = 2 * n_pairs
    key_w = n_pairs * hd

    @pl.when(step == 0)
    def _():
        p1_ref[...] = jnp.zeros_like(p1_ref)
        gb1_ref[...] = jnp.zeros_like(gb1_ref)
        s_ref[...] = jnp.zeros_like(s_ref)
        prev_ref[...] = jnp.zeros_like(prev_ref)

    seq_start = lax.rem(jnp.maximum(step - 1, 0), blocks_per_seq) == 0
    s_ref[...] = jnp.where(seq_start, 0.0, s_ref[...])
    xb_ref[...] = x_ref[...].astype(BF16)

    def cols_of(c0, width):
        g, off = divmod(c0, group_w)
        assert off + width <= group_w
        return g, slice(off, off + width)

    def run(p_cur, gb_cur, p_nxt, gb_nxt):
        gb_nxt[...] = _gdn_gate_values(xb_ref[...], wab_ref[...], alog_ref[...], dtb_ref[...])

        def tile(t, carry):
            p_nxt[t] = jnp.dot(xb_ref[...], wg_ref[t], preferred_element_type=F32)
            r0 = pl.multiple_of(t * LANES, LANES)
            rows = pl.ds(r0, LANES)
            prev = pl.ds(pl.multiple_of(jnp.maximum(r0 - SUBLANES, 0), SUBLANES), SUBLANES)

            def conv(c0, width):
                g, cols = cols_of(c0, width)
                carried = jnp.where(seq_start, 0.0, prev_ref[:, c0:c0 + width])
                prev8 = jnp.where(t > 0, p_cur[g, prev, cols], carried)
                return _silu(_causal_conv(prev8, p_cur[g, rows, cols], cw_ref[:, c0:c0 + width]))

            pairs = []
            for p in range(n_pairs):
                q = _l2_norm_rows(conv(p * hd, hd)) * q_scale
                k = _l2_norm_rows(conv(key_w + p * hd, hd))
                v = conv(2 * key_w + 2 * p * hd, 2 * hd)
                gz, zc = cols_of(4 * key_w + 2 * p * hd, 2 * hd)
                pairs.append((q, k, v, p_cur[gz, rows, zc],
                              gb_cur[p * SUBLANES:(p + 1) * SUBLANES, rows]))
            heads = _gdn_intra(pairs, LANES, hd, masks_ref)
            outs, leaving = _gdn_inter(heads, [[s_ref[h]] for h in range(n_heads)], ng_ref[...],
                                       LANES, hd)
            for h in range(n_heads):
                og_ref[rows, h * hd:(h + 1) * hd] = outs[h].astype(BF16)
                s_ref[h] = leaving[h][0]
            return carry

        lax.fori_loop(0, n_groups, tile, 0)
        for c0 in range(0, 4 * key_w, hd):
            g, cols = cols_of(c0, hd)
            tail = p_cur[g, sb - SUBLANES:sb, cols]
            tail_ref[:, c0:c0 + hd] = tail
            prev_ref[:, c0:c0 + hd] = tail

    parity = lax.rem(step, 2)

    @pl.when(parity == 0)
    def _():
        run(p1_ref, gb1_ref, p0_ref, gb0_ref)

    @pl.when(parity == 1)
    def _():
        run(p0_ref, gb0_ref, p1_ref, gb1_ref)

    so_ref[...] = s_ref[...]


def _gdn_sample_body(q_ref, k_ref, v_ref, z_ref, gb_ref, wq_ref, wk_ref, wv_ref, ng_ref, masks_ref,
                     hq_ref, hk_ref, hv_ref, s_ref, og_ref, so_ref, *, block, q_scale, hd):
    segs = _segments(block)

    def conv(ref, hist_ref, w_ref):
        x = ref[...]
        w = w_ref[...]
        return _silu(jnp.concatenate(
            [_causal_conv(hist_ref[s], x[r0:r1], w) for s, (r0, r1) in enumerate(segs)], axis=0))

    q = _l2_norm_rows(conv(q_ref, hq_ref, wq_ref)) * q_scale
    k = _l2_norm_rows(conv(k_ref, hk_ref, wk_ref))
    v = conv(v_ref, hv_ref, wv_ref)
    heads = _gdn_intra([(q, k, v, z_ref[...], gb_ref[...])], block, hd, masks_ref)
    states = [[s_ref[s, e] for s in range(len(segs))] for e in range(2)]
    outs, leaving = _gdn_inter(heads, states, ng_ref[...], block, hd)
    og_ref[...] = jnp.concatenate(outs, axis=1).astype(BF16)
    for e in range(2):
        for s in range(len(segs)):
            so_ref[s, e] = leaving[e][s]


def _gdn_prompt(x, w_main, w_ab_t, a_log_rows, dt_bias_rows, conv_w, norm_g, *, n_prompt, batch,
                hq, hd):
    d = x.shape[1]
    t = n_prompt // batch
    n_main = w_main.shape[1]
    conv_dim = conv_w.shape[1]
    val_w = 2 * hq * hd
    n_rows = w_ab_t.shape[0]
    sb = math.gcd(GDN_SEQ_BLOCK, t)
    n_groups = sb // LANES
    group_w = n_main // n_groups
    assert n_main % n_groups == 0 and group_w % (2 * hd) == 0
    bps = t // sb
    n_blocks = n_prompt // sb
    w_groups = w_main.reshape(d, n_groups, group_w).transpose(1, 0, 2)
    masks = _inverse_level_masks(LANES)

    def recurred(i):
        return jnp.maximum(i - 1, 0)

    return pl.pallas_call(
        functools.partial(_gdn_prompt_body, blocks_per_seq=bps, n_pairs=hq, q_scale=hd ** -0.5, hd=hd),
        grid=(n_blocks + 1,),
        in_specs=[pl.BlockSpec((sb, d), lambda i: (jnp.minimum(i, n_blocks - 1), 0)),
                  _resident(w_groups.shape), _resident(w_ab_t.shape), _resident((n_rows, 1)),
                  _resident((n_rows, 1)), _resident(conv_w.shape), _resident((1, hd)),
                  _resident(masks.shape)],
        out_specs=[pl.BlockSpec((sb, val_w), lambda i: (recurred(i), 0)),
                   pl.BlockSpec((None, 2 * hq, hd, hd), lambda i: (recurred(i) // bps, 0, 0, 0)),
                   pl.BlockSpec((SUBLANES, conv_dim), lambda i: (recurred(i), 0))],
        out_shape=[jax.ShapeDtypeStruct((n_prompt, val_w), BF16),
                   jax.ShapeDtypeStruct((batch, 2 * hq, hd, hd), F32),
                   jax.ShapeDtypeStruct((n_blocks * SUBLANES, conv_dim), F32)],
        scratch_shapes=[pltpu.VMEM((n_groups, sb, group_w), F32), pltpu.VMEM((n_groups, sb, group_w), F32),
                        pltpu.VMEM((n_rows, sb), F32), pltpu.VMEM((n_rows, sb), F32),
                        pltpu.VMEM((sb, d), BF16), pltpu.VMEM((2 * hq, hd, hd), F32),
                        pltpu.VMEM((SUBLANES, conv_dim), F32)],
        compiler_params=_cparams(1),
        name="gdn_prompt",
    )(x, w_groups, w_ab_t, a_log_rows, dt_bias_rows, conv_w, norm_g.reshape(1, hd), masks)


def _gdn_sample(p, gb, conv_w, norm_g, hist8, state_in, layer, *, hq, hd, dec_seq):
    n_sample = p.shape[0]
    key_w = hq * hd
    val_w = 2 * key_w
    q_scale = hd ** -0.5
    v_blk0 = (2 * key_w) // (2 * hd)
    z_blk0 = (2 * key_w + val_w) // (2 * hd)
    ng = norm_g.reshape(1, hd)
    w_specs = [pl.BlockSpec((CONV_TAPS, hd), lambda a, h: (0, h)),
               pl.BlockSpec((CONV_TAPS, hd), lambda a, h: (0, hq + h)),
               pl.BlockSpec((CONV_TAPS, 2 * hd), lambda a, h: (0, v_blk0 + h))]
    dec_batch = n_sample // dec_seq
    seqs = LANES // dec_seq
    off = 0
    masks_s = _inverse_level_masks(dec_seq)
    og_s, s_s = pl.pallas_call(
        functools.partial(_gdn_sample_body, block=dec_seq, q_scale=q_scale, hd=hd),
        grid=(n_sample // LANES, hq),
        in_specs=[pl.BlockSpec((LANES, hd), lambda g, h: (off + g, h)),
                  pl.BlockSpec((LANES, hd), lambda g, h: (off + g, hq + h)),
                  pl.BlockSpec((LANES, 2 * hd), lambda g, h: (off + g, v_blk0 + h)),
                  pl.BlockSpec((LANES, 2 * hd), lambda g, h: (off + g, z_blk0 + h)),
                  pl.BlockSpec((SUBLANES, LANES), lambda g, h: (h, off + g))] + w_specs + [
                  _resident((1, hd)), _resident(masks_s.shape),
                  pl.BlockSpec((seqs, SUBLANES, hd), lambda g, h: (g, 0, h)),
                  pl.BlockSpec((seqs, SUBLANES, hd), lambda g, h: (g, 0, hq + h)),
                  pl.BlockSpec((seqs, SUBLANES, 2 * hd), lambda g, h: (g, 0, v_blk0 + h)),
                  pl.BlockSpec((seqs, None, 2, hd, hd), lambda g, h: (g, layer, h, 0, 0))],
        out_specs=[pl.BlockSpec((LANES, 2 * hd), lambda g, h: (g, h)),
                   pl.BlockSpec((seqs, 2, hd, hd), lambda g, h: (g, h, 0, 0))],
        out_shape=[jax.ShapeDtypeStruct((n_sample, val_w), BF16),
                   jax.ShapeDtypeStruct((dec_batch, 2 * hq, hd, hd), F32)],
        compiler_params=_cparams(2),
        name="gdn_sample",
    )(p, p, p, p, gb, conv_w, conv_w, conv_w, ng, masks_s, hist8, hist8, hist8, state_in)
    return og_s, s_s


def _gdn_gate_rows(w_in, a_log, dt_bias, n_main, hq):
    d = w_in.shape[0]
    hv = a_log.shape[0]
    rep = hv // hq
    pad = SUBLANES - 2 * rep
    wa = w_in[:, n_main:n_main + hv].T.reshape(hq, rep, d)
    wb = w_in[:, n_main + hv:n_main + 2 * hv].T.reshape(hq, rep, d)
    w_ab_t = jnp.concatenate([wa, wb, jnp.zeros((hq, pad, d), w_in.dtype)], axis=1)
    zeros = jnp.zeros((hq, SUBLANES - rep), F32)
    alog = jnp.concatenate([a_log.reshape(hq, rep), zeros], axis=1).reshape(hq * SUBLANES, 1)
    dtb = jnp.concatenate([dt_bias.reshape(hq, rep), zeros], axis=1).reshape(hq * SUBLANES, 1)
    return w_ab_t.reshape(hq * SUBLANES, d).astype(BF16), alog, dtb


def kernel(x_prompt, x_sample, state_gla, state_gdn, state_gdn_conv, ln_g, ln_b, ffn1_w_gu, ffn1_w_d, ffn2_w_gu, ffn2_w_d, gla_w_in, gla_w_a1, gla_w_a2, gla_b_a, gla_norm_g, gla_w_o, gdn_w_in, gdn_conv_w, gdn_a_log, gdn_dt_bias, gdn_norm_g, gdn_w_o):
    batch, t, d = x_prompt.shape
    dec_batch, dec_seq, _ = x_sample.shape
    depth = ln_g.shape[0]
    alpha = (2.0 * depth) ** 0.25
    n_prompt, n_sample = batch * t, dec_batch * dec_seq
    tm = math.gcd(TOKEN_TILE, math.gcd(n_prompt, n_sample))
    assert t % LANES == 0 and t % GLA_CHUNK == 0 and n_sample % LANES == 0 and LANES % dec_seq == 0
    assert tm % LANES == 0

    gla_heads, gla_dk, gla_dv = state_gla.shape[2:]
    gla_qk_w = gla_heads * gla_dk
    hv, hd = state_gdn.shape[2:4]
    conv_dim = gdn_conv_w.shape[2]
    val_w = hv * hd
    hq = (conv_dim - val_w) // (2 * hd)
    assert hv == 2 * hq, "the gated-delta kernel pairs two v heads with each q/k head"
    n_main = conv_dim + val_w

    x = (x_prompt.reshape(n_prompt, d), x_sample.reshape(n_sample, d))
    new_gla_p, new_gla_s, new_gdn_p, new_gdn_s, new_conv_p, new_conv_s = [], [], [], [], [], []
    for i in range(depth):
        x = _ffn_ln(x, ffn1_w_gu[i].astype(BF16), ffn1_w_d[i].astype(BF16), ln_g[i, 0], ln_b[i, 0],
                    tm=tm, alpha=alpha, n_prompt=n_prompt)
        j = i // 2
        if i % 2 == 0:
            rank = gla_w_a1.shape[2]
            w_a1 = jnp.pad(gla_w_a1[j], ((0, 0), (0, LANES - rank))).astype(BF16)
            w_a2 = jnp.pad(gla_w_a2[j], ((0, LANES - rank), (0, 0))).astype(BF16)
            p, la = _gla_proj(x, gla_w_in[j].astype(BF16), w_a1, w_a2, gla_b_a[j],
                              tm=tm, qk_w=gla_qk_w, q_scale=gla_dk ** -0.5)
            og_p, og_s, s_p, s_s = _gla_recurrence(
                p, la, gla_norm_g[j], state_gla, j, n_prompt=n_prompt, batch=batch,
                heads=gla_heads, dk=gla_dk, dv=gla_dv, dec_seq=dec_seq)
            new_gla_p.append(s_p)
            new_gla_s.append(s_s)
            w_o = gla_w_o[j]
        else:
            w_ab_t, alog_rows, dtb_rows = _gdn_gate_rows(gdn_w_in[j], gdn_a_log[j], gdn_dt_bias[j],
                                                         n_main, hq)
            w_main = gdn_w_in[j][:, :n_main].astype(BF16)
            og_p, s_p, tails = _gdn_prompt(x, w_main, w_ab_t, alog_rows, dtb_rows, gdn_conv_w[j],
                                           gdn_norm_g[j], n_prompt=n_prompt, batch=batch, hq=hq,
                                           hd=hd)
            p_s, gb_s = _gdn_proj(x, w_main, w_ab_t, alog_rows, dtb_rows, tm=tm,
                                  first_tile=n_prompt // tm, n_tiles=n_sample // tm)
            hist8 = jnp.pad(state_gdn_conv[:, j],
                            ((0, 0), (SUBLANES - (CONV_TAPS - 1), 0), (0, 0)))
            og_s, s_s = _gdn_sample(p_s, gb_s, gdn_conv_w[j], gdn_norm_g[j], hist8, state_gdn, j,
                                    hq=hq, hd=hd, dec_seq=dec_seq)
            new_gdn_p.append(s_p)
            new_gdn_s.append(s_s)
            keep = np.arange(-(CONV_TAPS - 1), 0)
            rows_p = ((np.arange(batch) + 1) * (tails.shape[0] // batch))[:, None] + keep
            rows_s = ((np.arange(dec_batch) + 1) * dec_seq)[:, None] + keep
            new_conv_p.append(jnp.take(tails, rows_p.reshape(-1), axis=0)
                              .reshape(batch, CONV_TAPS - 1, conv_dim))
            new_conv_s.append(jnp.take(p_s, rows_s.reshape(-1), axis=0)[:, :conv_dim]
                              .reshape(dec_batch, CONV_TAPS - 1, conv_dim))
            w_o = gdn_w_o[j]
        x = _ffn_ln(x, ffn2_w_gu[i].astype(BF16), ffn2_w_d[i].astype(BF16), ln_g[i, 2], ln_b[i, 2],
                    tm=tm, alpha=alpha, n_prompt=n_prompt, split_out=(i == depth - 1),
                    mixer=(og_p, og_s, w_o.astype(BF16), ln_g[i, 1], ln_b[i, 1]))
    y_prompt, y_sample = x
    return (y_prompt.reshape(batch, t, d), y_sample.reshape(dec_batch, dec_seq, d),
            jnp.stack(new_gla_p, 1), jnp.stack(new_gdn_p, 1), jnp.stack(new_conv_p, 1),
            jnp.stack(new_gla_s, 1), jnp.stack(new_gdn_s, 1), jnp.stack(new_conv_s, 1))
```

```python
import functools
import math

import numpy as np

import jax
import jax.numpy as jnp
from jax import lax
from jax.experimental import pallas as pl
from jax.experimental.pallas import tpu as pltpu

F32 = jnp.float32
BF16 = jnp.bfloat16

LN_EPS = 1e-5
RMS_EPS = 1e-6
L2_EPS = 1e-6
FFN_RES = 0.5
GLA_GATE_TAU = 16.0
GLA_CHUNK = 64
CONV_TAPS = 4

LANES = 128
SUBLANES = 8
VMEM_LIMIT_BYTES = 56 * 1024 * 1024
TOKEN_TILE = 512
MATMUL_N_CHUNK = 1024
GDN_SEQ_BLOCK = 256
GLA_SEQ_BLOCK = 512


def _cparams(n_axes):
    return pltpu.CompilerParams(
        dimension_semantics=("arbitrary",) * n_axes, vmem_limit_bytes=VMEM_LIMIT_BYTES)


def _resident(shape):
    zeros = (0,) * len(shape)
    return pl.BlockSpec(shape, lambda *_: zeros, pipeline_mode=pl.Buffered(1))


def _dot(a, b):
    return jnp.dot(a.astype(BF16), b.astype(BF16), preferred_element_type=F32)


def _dot_nt(a, b):
    return lax.dot_general(a.astype(BF16), b.astype(BF16), (((1,), (1,)), ((), ())),
                           preferred_element_type=F32)


def _silu(x):
    return x * jax.nn.sigmoid(x)


def _softplus(x):
    return jnp.maximum(x, 0.0) + jnp.log1p(jnp.exp(-jnp.abs(x)))


def _layer_norm_rows(y, g, b):
    mu = jnp.mean(y, axis=-1, keepdims=True)
    d = y - mu
    var = jnp.mean(d * d, axis=-1, keepdims=True)
    return d * lax.rsqrt(var + LN_EPS) * g + b


def _rms_norm_rows(o, g):
    return o * lax.rsqrt(jnp.mean(o * o, axis=-1, keepdims=True) + RMS_EPS) * g


def _l2_norm_rows(x):
    return x * lax.rsqrt(jnp.sum(x * x, axis=-1, keepdims=True) + L2_EPS)


def _col_chunks(n, step=MATMUL_N_CHUNK):
    return [(c, min(c + step, n)) for c in range(0, n, step)]


def _ffn_ln_body(*refs, d_ff, alpha, n_in, has_mixer, n_prompt_tiles):
    x_refs, refs = refs[:n_in], refs[n_in:]
    is_prompt = pl.program_id(0) < n_prompt_tiles
    if has_mixer:
        (ogp_ref, ogs_ref, wo_ref, g1_ref, b1_ref), refs = refs[:5], refs[5:]
    wgu_ref, wd_ref, g_ref, b_ref = refs[:4]
    o_refs, h_ref = refs[4:-1], refs[-1]
    x = x_refs[0][...]
    if n_in == 2:
        x = jnp.where(is_prompt, x, x_refs[1][...])
    if has_mixer:
        og = jnp.where(is_prompt, ogp_ref[...], ogs_ref[...])
        x = _layer_norm_rows(alpha * x + jnp.dot(og, wo_ref[...], preferred_element_type=F32),
                             g1_ref[...], b1_ref[...])
    xb = x.astype(BF16)
    for c0, c1 in _col_chunks(d_ff):
        gate = jnp.dot(xb, wgu_ref[:, c0:c1], preferred_element_type=F32)
        up = jnp.dot(xb, wgu_ref[:, d_ff + c0:d_ff + c1], preferred_element_type=F32)
        h_ref[:, c0:c1] = (_silu(gate) * up).astype(BF16)
    y = jnp.dot(h_ref[...], wd_ref[...], preferred_element_type=F32)
    out = _layer_norm_rows(alpha * x + FFN_RES * y, g_ref[...], b_ref[...])
    if len(o_refs) == 1:
        o_refs[0][...] = out
    else:
        o_refs[1][...] = out

        @pl.when(is_prompt)
        def _():
            o_refs[0][...] = out


def _ffn_ln(xs, w_gu, w_d, g, b, *, tm, alpha, n_prompt, mixer=None, split_out=False):
    xs = xs if isinstance(xs, (tuple, list)) else (xs,)
    d = xs[0].shape[1]
    n = sum(x.shape[0] for x in xs)
    d_ff = w_d.shape[0]
    ntp = n_prompt // tm

    def pair(width):
        return [pl.BlockSpec((tm, width), lambda i: (jnp.minimum(i, ntp - 1), 0)),
                pl.BlockSpec((tm, width), lambda i: (jnp.maximum(i - ntp, 0), 0))]

    tile = pl.BlockSpec((tm, d), lambda i: (i, 0))
    vec = _resident((1, d))
    pair_shape = [jax.ShapeDtypeStruct((n_prompt, d), F32), jax.ShapeDtypeStruct((n - n_prompt, d), F32)]
    mixer_specs, mixer_args = [], []
    if mixer is not None:
        og_prompt, og_sample, w_o, g1, b1 = mixer
        mixer_specs = pair(w_o.shape[0]) + [_resident(w_o.shape), vec, vec]
        mixer_args = [og_prompt, og_sample, w_o, g1.reshape(1, d), b1.reshape(1, d)]
    return pl.pallas_call(
        functools.partial(_ffn_ln_body, d_ff=d_ff, alpha=alpha, n_in=len(xs),
                          has_mixer=mixer is not None, n_prompt_tiles=ntp),
        grid=(n // tm,),
        in_specs=(pair(d) if len(xs) == 2 else [tile]) + mixer_specs + [
            _resident(w_gu.shape), _resident(w_d.shape), vec, vec],
        out_specs=pair(d) if split_out else tile,
        out_shape=pair_shape if split_out else jax.ShapeDtypeStruct((n, d), F32),
        scratch_shapes=[pltpu.VMEM((tm, d_ff), BF16)],
        compiler_params=_cparams(1),
        name="ffn_ln",
    )(*xs, *mixer_args, w_gu, w_d, g.reshape(1, d), b.reshape(1, d))


def _gla_log_decay(xb, wa1, wa2, ba):
    low = jnp.dot(xb, wa1, preferred_element_type=F32)
    logits = _dot(low, wa2) + ba
    log_sig = jnp.minimum(logits, 0.0) - jnp.log1p(jnp.exp(-jnp.abs(logits)))
    return log_sig / GLA_GATE_TAU


def _gla_proj_body(x_ref, win_ref, wa1_ref, wa2_ref, ba_ref, p_ref, la_ref):
    xb = x_ref[...].astype(BF16)
    for c0, c1 in _col_chunks(win_ref.shape[1]):
        p_ref[:, c0:c1] = jnp.dot(xb, win_ref[:, c0:c1], preferred_element_type=F32)
    la_ref[...] = _gla_log_decay(xb, wa1_ref[...], wa2_ref[...], ba_ref[...])


def _gla_proj(x, w_in, w_a1, w_a2, b_a, *, tm, first_tile, n_tiles):
    d = x.shape[1]
    n = n_tiles * tm
    n_out = w_in.shape[1]
    qk_w = w_a2.shape[1]
    return pl.pallas_call(
        _gla_proj_body,
        grid=(n_tiles,),
        in_specs=[pl.BlockSpec((tm, d), lambda i: (first_tile + i, 0)), _resident(w_in.shape),
                  _resident(w_a1.shape), _resident(w_a2.shape), _resident((1, qk_w))],
        out_specs=[pl.BlockSpec((tm, n_out), lambda i: (i, 0)),
                   pl.BlockSpec((tm, qk_w), lambda i: (i, 0))],
        out_shape=[jax.ShapeDtypeStruct((n, n_out), F32), jax.ShapeDtypeStruct((n, qk_w), F32)],
        compiler_params=_cparams(1),
        name="gla_proj",
    )(x, w_in, w_a1, w_a2, b_a.reshape(1, qk_w))


def _gdn_gate_values(xb, wab, alog, dtb):
    ab = lax.dot_general(wab, xb, (((1,), (1,)), ((), ())), preferred_element_type=F32)
    kind = _iota2(ab.shape, 0) & (SUBLANES - 1)
    g = -jnp.exp(alog) * _softplus(ab + dtb)
    beta = jax.nn.sigmoid(ab)
    return jnp.where(kind < 2, g, jnp.where(kind < 4, beta, 0.0))


def _gdn_proj_body(x_ref, wmain_ref, wab_ref, alog_ref, dtb_ref, p_ref, gb_ref):
    xb = x_ref[...].astype(BF16)
    for c0, c1 in _col_chunks(wmain_ref.shape[1]):
        p_ref[:, c0:c1] = jnp.dot(xb, wmain_ref[:, c0:c1], preferred_element_type=F32)
    gb_ref[...] = _gdn_gate_values(xb, wab_ref[...], alog_ref[...], dtb_ref[...])


def _gdn_proj(x, w_main, w_ab_t, a_log_rows, dt_bias_rows, *, tm, first_tile, n_tiles):
    d = x.shape[1]
    n = n_tiles * tm
    n_out = w_main.shape[1]
    n_rows = w_ab_t.shape[0]
    return pl.pallas_call(
        _gdn_proj_body,
        grid=(n_tiles,),
        in_specs=[pl.BlockSpec((tm, d), lambda i: (first_tile + i, 0)), _resident(w_main.shape),
                  _resident(w_ab_t.shape), _resident((n_rows, 1)), _resident((n_rows, 1))],
        out_specs=[pl.BlockSpec((tm, n_out), lambda i: (i, 0)),
                   pl.BlockSpec((n_rows, tm), lambda i: (0, i))],
        out_shape=[jax.ShapeDtypeStruct((n, n_out), F32), jax.ShapeDtypeStruct((n_rows, n), F32)],
        compiler_params=_cparams(1),
        name="gdn_proj",
    )(x, w_main, w_ab_t, a_log_rows, dt_bias_rows)


def _iota2(shape, dim):
    return lax.broadcasted_iota(jnp.int32, shape, dim)


def _block_of(idx, block):
    return idx >> (block.bit_length() - 1)


def _block_cumsum(x, block, axis):
    pos = _iota2(x.shape, axis) & (block - 1)
    shift = 1
    while shift < block:
        x = x + jnp.where(pos >= shift, pltpu.roll(x, shift, axis), 0.0)
        shift *= 2
    return x


def _col_form(row):
    return jnp.transpose(jnp.broadcast_to(row, (LANES, LANES)))


def _segments(block):
    return [(r, r + block) for r in range(0, LANES, block)]


def _last_row_of_segments(x, block):
    return jnp.concatenate(
        [jnp.broadcast_to(x[r1 - 1:r1, :], (r1 - r0, x.shape[1])) for r0, r1 in _segments(block)],
        axis=0)


def _rows_only(x, r0, r1):
    rows = _iota2(x.shape, 0)
    return jnp.where((rows >= r0) & (rows < r1), x, 0.0)


def _gla_tiles(heads, ng, block, chained):
    row = _iota2((LANES, LANES), 0)
    col = _iota2((LANES, LANES), 1)
    causal = (_block_of(row, block) == _block_of(col, block)) & (row >= col)
    segs = _segments(block)
    bcums = [_block_cumsum(la, block, 0) for (_, _, _, la, _, _) in heads]
    q_decs = [q * jnp.exp(b) for (q, _, _, _, _, _), b in zip(heads, bcums)]
    atts = [jnp.where(causal, _dot_nt(qd, k * jnp.exp(-b)), 0.0)
            for qd, (_, k, _, _, _, _), b in zip(q_decs, heads, bcums)]
    intras = [_dot(att, v) for att, (_, _, v, _, _, _) in zip(atts, heads)]
    k_end_ts = [jnp.transpose(k * jnp.exp(_last_row_of_segments(b, block) - b))
                for (_, k, _, _, _, _), b in zip(heads, bcums)]
    lane_reps = heads[0][2].shape[1] // LANES
    states = [list(st) for (_, _, _, _, _, st) in heads]
    cur = [st[0] for st in states]
    outs = [[] for _ in heads]
    leaving = [[] for _ in heads]
    for s, (r0, r1) in enumerate(segs):
        if not chained:
            cur = [st[s] for st in states]
        inter = [_dot(qd[r0:r1], c) for qd, c in zip(q_decs, cur)]
        upd = [_dot(kt, _rows_only(v, r0, r1)) for kt, (_, _, v, _, _, _) in zip(k_end_ts, heads)]
        for i, b in enumerate(bcums):
            outs[i].append(intras[i][r0:r1] + inter[i])
            decay = jnp.concatenate([_col_form(jnp.exp(b[r1 - 1:r1, :]))] * lane_reps, axis=1)
            cur[i] = decay * cur[i] + upd[i]
            if not chained:
                leaving[i].append(cur[i])
    results = []
    for i, (_, _, _, _, r, _) in enumerate(heads):
        o = _rms_norm_rows(jnp.concatenate(outs[i], axis=0), ng)
        results.append((o * _silu(r), leaving[i] if not chained else [cur[i]]))
    return results


def _gla_prompt_body(x_ref, wg_ref, wa1_ref, wa2_ref, ba_ref, ng_ref, og_ref, so_ref,
                     p0_ref, p1_ref, la0_ref, la1_ref, xb_ref, s_ref, *,
                     blocks_per_seq, n_heads, dk, dv, q_scale):
    step = pl.program_id(0)
    n_groups, _, group_w = p0_ref.shape
    qk_w = n_heads * dk

    @pl.when(step == 0)
    def _():
        p1_ref[...] = jnp.zeros_like(p1_ref)
        la1_ref[...] = jnp.zeros_like(la1_ref)
        s_ref[...] = jnp.zeros_like(s_ref)

    seq_start = lax.rem(jnp.maximum(step - 1, 0), blocks_per_seq) == 0
    s_ref[...] = jnp.where(seq_start, 0.0, s_ref[...])
    xb_ref[...] = x_ref[...].astype(BF16)

    def cols_of(c0, width):
        g, off = divmod(c0, group_w)
        assert off + width <= group_w
        return g, slice(off, off + width)

    def run(p_cur, la_cur, p_nxt, la_nxt):
        la_nxt[...] = _gla_log_decay(xb_ref[...], wa1_ref[...], wa2_ref[...], ba_ref[...])

        def tile(t, carry):
            p_nxt[t] = jnp.dot(xb_ref[...], wg_ref[t], preferred_element_type=F32)
            rows = pl.ds(pl.multiple_of(t * LANES, LANES), LANES)
            heads = []
            for h in range(n_heads):
                gq, cq = cols_of(h * dk, dk)
                gk, ck = cols_of(qk_w + h * dk, dk)
                gv, cv = cols_of(2 * qk_w + h * dv, dv)
                gr, cr = cols_of(2 * qk_w + n_heads * dv + h * dv, dv)
                heads.append((p_cur[gq, rows, cq] * q_scale, p_cur[gk, rows, ck],
                              p_cur[gv, rows, cv], la_cur[rows, h * dk:(h + 1) * dk],
                              p_cur[gr, rows, cr], [s_ref[h]]))
            for h, (o, leaving) in enumerate(_gla_tiles(heads, ng_ref[...], GLA_CHUNK, True)):
                og_ref[rows, h * dv:(h + 1) * dv] = o.astype(BF16)
                s_ref[h] = leaving[0]
            return carry

        lax.fori_loop(0, n_groups, tile, 0)

    parity = lax.rem(step, 2)

    @pl.when(parity == 0)
    def _():
        run(p1_ref, la1_ref, p0_ref, la0_ref)

    @pl.when(parity == 1)
    def _():
        run(p0_ref, la0_ref, p1_ref, la1_ref)

    so_ref[...] = s_ref[...]


def _gla_sample_body(q_ref, k_ref, v_ref, r_ref, la_ref, ng_ref, s_ref, og_ref, so_ref, *, block,
                     q_scale):
    n_seq = LANES // block
    [(o, leaving)] = _gla_tiles(
        [(q_ref[...] * q_scale, k_ref[...], v_ref[...], la_ref[...], r_ref[...],
          [s_ref[i] for i in range(n_seq)])], ng_ref[...], block, False)
    og_ref[...] = o.astype(BF16)
    for i in range(n_seq):
        so_ref[i] = leaving[i]


def _carried(body, n_inputs, prev):
    if prev is None:
        return body, [], [], {}

    def skipping_carried_input(*refs):
        return body(*refs[:n_inputs], *refs[n_inputs + 1:])

    return skipping_carried_input, [pl.BlockSpec(memory_space=pl.ANY)], [prev], {n_inputs: 1}


def _gla_prompt(x, w_in, w_a1, w_a2, b_a, norm_g, layer, n_layers, prev_states, *, n_prompt, batch,
                heads, dk, dv):
    d = x.shape[1]
    t = n_prompt // batch
    n_out = w_in.shape[1]
    qk_w, v_w = heads * dk, heads * dv
    sb = math.gcd(GLA_SEQ_BLOCK, t)
    n_groups = sb // LANES
    group_w = n_out // n_groups
    assert n_out % n_groups == 0 and group_w % dv == 0
    bps = t // sb
    n_blocks = n_prompt // sb
    w_groups = w_in.reshape(d, n_groups, group_w).transpose(1, 0, 2)

    def recurred(i):
        return jnp.maximum(i - 1, 0)

    body, more_specs, more_args, aliases = _carried(
        functools.partial(_gla_prompt_body, blocks_per_seq=bps, n_heads=heads, dk=dk, dv=dv,
                          q_scale=dk ** -0.5), 6, prev_states)
    return pl.pallas_call(
        body,
        grid=(n_blocks + 1,),
        in_specs=[pl.BlockSpec((sb, d), lambda i: (jnp.minimum(i, n_blocks - 1), 0)),
                  _resident(w_groups.shape), _resident(w_a1.shape), _resident(w_a2.shape),
                  _resident((1, qk_w)), _resident((1, dv))] + more_specs,
        out_specs=[pl.BlockSpec((sb, v_w), lambda i: (recurred(i), 0)),
                   pl.BlockSpec((None, None, heads, dk, dv),
                                lambda i: (recurred(i) // bps, layer, 0, 0, 0))],
        out_shape=[jax.ShapeDtypeStruct((n_prompt, v_w), BF16),
                   jax.ShapeDtypeStruct((batch, n_layers, heads, dk, dv), F32)],
        scratch_shapes=[pltpu.VMEM((n_groups, sb, group_w), F32), pltpu.VMEM((n_groups, sb, group_w), F32),
                        pltpu.VMEM((sb, qk_w), F32), pltpu.VMEM((sb, qk_w), F32),
                        pltpu.VMEM((sb, d), BF16), pltpu.VMEM((heads, dk, dv), F32)],
        input_output_aliases=aliases,
        compiler_params=_cparams(1),
        name="gla_prompt",
    )(x, w_groups, w_a1, w_a2, b_a.reshape(1, qk_w), norm_g.reshape(1, dv), *more_args)


def _gla_sample(p, la, norm_g, state_in, layer, prev_states, *, heads, dk, dv, dec_seq):
    n_sample = p.shape[0]
    qk_w, v_w = heads * dk, heads * dv
    v_blk0 = (2 * qk_w) // dv
    r_blk0 = (2 * qk_w + v_w) // dv
    ng = norm_g.reshape(1, dv)
    seqs = LANES // dec_seq
    off = 0
    state_spec = pl.BlockSpec((seqs, None, None, dk, dv), lambda g, h: (g, layer, h, 0, 0))
    body, more_specs, more_args, aliases = _carried(
        functools.partial(_gla_sample_body, block=dec_seq, q_scale=dk ** -0.5), 7, prev_states)
    og_s, s_s = pl.pallas_call(
        body,
        grid=(n_sample // LANES, heads),
        in_specs=[pl.BlockSpec((LANES, dk), lambda g, h: (off + g, h)),
                  pl.BlockSpec((LANES, dk), lambda g, h: (off + g, heads + h)),
                  pl.BlockSpec((LANES, dv), lambda g, h: (off + g, v_blk0 + h)),
                  pl.BlockSpec((LANES, dv), lambda g, h: (off + g, r_blk0 + h)),
                  pl.BlockSpec((LANES, dk), lambda g, h: (off + g, h)),
                  _resident((1, dv)), state_spec] + more_specs,
        out_specs=[pl.BlockSpec((LANES, dv), lambda g, h: (g, h)), state_spec],
        out_shape=[jax.ShapeDtypeStruct((n_sample, v_w), BF16),
                   jax.ShapeDtypeStruct(state_in.shape, F32)],
        input_output_aliases=aliases,
        compiler_params=_cparams(2),
        name="gla_sample",
    )(p, p, p, p, la, ng, state_in, *more_args)
    return og_s, s_s


def _causal_conv(prev8, x, w):
    n = x.shape[0]
    xs = jnp.concatenate([prev8, x], axis=0)
    acc = xs[SUBLANES:, :] * w[CONV_TAPS - 1:CONV_TAPS, :]
    for j in range(CONV_TAPS - 1):
        shifted = pltpu.roll(xs, CONV_TAPS - 1 - j, 0)
        acc = acc + shifted[SUBLANES:SUBLANES + n, :] * w[j:j + 1, :]
    return acc


def _inverse_level_masks(block):
    idx = np.arange(LANES)
    masks = []
    size = 1
    while size < block:
        pair = (idx[:, None] // (2 * size)) == (idx[None, :] // (2 * size))
        masks.append(pair & ((idx[:, None] // size) != (idx[None, :] // size)))
        size *= 2
    return jnp.asarray(np.stack(masks).astype(np.float32))


def _unit_lower_inverses(mats, masks_ref):
    row = _iota2((LANES, LANES), 0)
    col = _iota2((LANES, LANES), 1)
    eye = jnp.where(row == col, 1.0, 0.0)
    xs = [eye - a * masks_ref[0] for a in mats]
    mats_b = [a.astype(BF16) for a in mats]
    for level in range(1, masks_ref.shape[0]):
        mask = masks_ref[level]
        xbs = [x.astype(BF16) for x in xs]
        ys = [jnp.dot(xb, ab, preferred_element_type=F32) for xb, ab in zip(xbs, mats_b)]
        xs = [x - jnp.dot(y.astype(BF16), xb, preferred_element_type=F32) * mask
              for x, y, xb in zip(xs, ys, xbs)]
    return xs


def _gdn_intra(pairs, block, hd, masks_ref):
    row = _iota2((LANES, LANES), 0)
    col = _iota2((LANES, LANES), 1)
    same = _block_of(row, block) == _block_of(col, block)
    incl = same & (row >= col)
    strict = same & (row > col)
    kks = [_dot_nt(k, k) for (_, k, _, _, _) in pairs]
    qks = [_dot_nt(q, k) for (q, k, _, _, _) in pairs]
    heads = []
    for (q, k, v, z, gb), kk, qk in zip(pairs, kks, qks):
        gcum = _block_cumsum(gb, block, 1)
        for e in range(2):
            g_row = gcum[e:e + 1, :]
            g_col = _col_form(g_row)
            b_col = _col_form(gb[2 + e:3 + e, :])
            decay = jnp.where(incl, jnp.exp(g_col - jnp.broadcast_to(g_row, (LANES, LANES))), 0.0)
            eg = jnp.exp(g_col)
            heads.append(dict(
                q_dec=q * eg, att=qk * decay, g_col=g_col,
                a=jnp.where(strict, b_col * kk * decay, 0.0),
                rhs=jnp.concatenate([k * (b_col * eg), v[:, e * hd:(e + 1) * hd] * b_col], axis=1),
                k_end=k * jnp.exp(_last_row_of_segments(g_col, block) - g_col),
                z=z[:, e * hd:(e + 1) * hd]))
    t_invs = _unit_lower_inverses([h["a"] for h in heads], masks_ref)
    for h, t_inv in zip(heads, t_invs):
        h["wu"] = _dot(t_inv, h["rhs"])
    for h in heads:
        h["k_end_t"] = jnp.transpose(h["k_end"])
    return heads


def _gdn_inter(heads, states, ng, block, hd):
    segs = _segments(block)
    ws_qs = [[_dot(jnp.concatenate([h["wu"][r0:r1, :hd], h["q_dec"][r0:r1]], axis=0), st[s])
              for s, (r0, r1) in enumerate(segs)] for h, st in zip(heads, states)]
    v_news = []
    for h, per_seg in zip(heads, ws_qs):
        parts = [h["wu"][r0:r1, hd:] - x[:r1 - r0] for (r0, r1), x in zip(segs, per_seg)]
        v_news.append(jnp.concatenate(parts, axis=0) if len(parts) > 1 else parts[0])
    o_intra = [_dot(h["att"], v_new) for h, v_new in zip(heads, v_news)]
    updates = [[_dot(h["k_end_t"], v_new if len(segs) == 1 else _rows_only(v_new, r0, r1))
                for (r0, r1) in segs] for h, v_new in zip(heads, v_news)]
    outs, leaving = [], []
    for h, st, per_seg, oi, upd in zip(heads, states, ws_qs, o_intra, updates):
        qs = [x[r1 - r0:] for (r0, r1), x in zip(segs, per_seg)]
        o = oi + (jnp.concatenate(qs, axis=0) if len(qs) > 1 else qs[0])
        outs.append(_rms_norm_rows(o, ng) * _silu(h["z"]))
        leaving.append([jnp.exp(h["g_col"][r1 - 1:r1, :]) * st[s] + upd[s]
                        for s, (r0, r1) in enumerate(segs)])
    return outs, leaving


def _gdn_prompt_body(x_ref, wg_ref, wab_ref, alog_ref, dtb_ref, cw_ref, ng_ref, masks_ref,
                     og_ref, so_ref, tail_ref, p0_ref, p1_ref, gb0_ref, gb1_ref, xb_ref, s_ref, prev_ref,
                     *, blocks_per_seq, n_pairs, q_scale, hd):
    step = pl.program_id(0)
    n_groups, sb, group_w = p0_ref.shape
    n_heads = 2 * n_pairs
    key_w = n_pairs * hd

    @pl.when(step == 0)
    def _():
        p1_ref[...] = jnp.zeros_like(p1_ref)
        gb1_ref[...] = jnp.zeros_like(gb1_ref)
        s_ref[...] = jnp.zeros_like(s_ref)
        prev_ref[...] = jnp.zeros_like(prev_ref)

    seq_start = lax.rem(jnp.maximum(step - 1, 0), blocks_per_seq) == 0
    s_ref[...] = jnp.where(seq_start, 0.0, s_ref[...])
    xb_ref[...] = x_ref[...].astype(BF16)

    def cols_of(c0, width):
        g, off = divmod(c0, group_w)
        assert off + width <= group_w
        return g, slice(off, off + width)

    def run(p_cur, gb_cur, p_nxt, gb_nxt):
        gb_nxt[...] = _gdn_gate_values(xb_ref[...], wab_ref[...], alog_ref[...], dtb_ref[...])

        def tile(t, carry):
            p_nxt[t] = jnp.dot(xb_ref[...], wg_ref[t], preferred_element_type=F32)
            r0 = pl.multiple_of(t * LANES, LANES)
            rows = pl.ds(r0, LANES)
            prev = pl.ds(pl.multiple_of(jnp.maximum(r0 - SUBLANES, 0), SUBLANES), SUBLANES)

            def conv(c0, width):
                g, cols = cols_of(c0, width)
                carried = jnp.where(seq_start, 0.0, prev_ref[:, c0:c0 + width])
                prev8 = jnp.where(t > 0, p_cur[g, prev, cols], carried)
                return _silu(_causal_conv(prev8, p_cur[g, rows, cols], cw_ref[:, c0:c0 + width]))

            pairs = []
            for p in range(n_pairs):
                q = _l2_norm_rows(conv(p * hd, hd)) * q_scale
                k = _l2_norm_rows(conv(key_w + p * hd, hd))
                v = conv(2 * key_w + 2 * p * hd, 2 * hd)
                gz, zc = cols_of(4 * key_w + 2 * p * hd, 2 * hd)
                pairs.append((q, k, v, p_cur[gz, rows, zc],
                              gb_cur[p * SUBLANES:(p + 1) * SUBLANES, rows]))
            heads = _gdn_intra(pairs, LANES, hd, masks_ref)
            outs, leaving = _gdn_inter(heads, [[s_ref[h]] for h in range(n_heads)], ng_ref[...],
                                       LANES, hd)
            for h in range(n_heads):
                og_ref[rows, h * hd:(h + 1) * hd] = outs[h].astype(BF16)
                s_ref[h] = leaving[h][0]
            return carry

        lax.fori_loop(0, n_groups, tile, 0)
        for c0 in range(0, 4 * key_w, hd):
            g, cols = cols_of(c0, hd)
            tail = p_cur[g, sb - SUBLANES:sb, cols]
            tail_ref[:, c0:c0 + hd] = tail
            prev_ref[:, c0:c0 + hd] = tail

    parity = lax.rem(step, 2)

    @pl.when(parity == 0)
    def _():
        run(p1_ref, gb1_ref, p0_ref, gb0_ref)

    @pl.when(parity == 1)
    def _():
        run(p0_ref, gb0_ref, p1_ref, gb1_ref)

    so_ref[...] = s_ref[...]


def _gdn_sample_body(q_ref, k_ref, v_ref, z_ref, gb_ref, wq_ref, wk_ref, wv_ref, ng_ref, masks_ref,
                     hq_ref, hk_ref, hv_ref, s_ref, og_ref, so_ref, *, block, q_scale, hd):
    segs = _segments(block)

    def conv(ref, hist_ref, w_ref):
        x = ref[...]
        w = w_ref[...]
        return _silu(jnp.concatenate(
            [_causal_conv(hist_ref[s], x[r0:r1], w) for s, (r0, r1) in enumerate(segs)], axis=0))

    q = _l2_norm_rows(conv(q_ref, hq_ref, wq_ref)) * q_scale
    k = _l2_norm_rows(conv(k_ref, hk_ref, wk_ref))
    v = conv(v_ref, hv_ref, wv_ref)
    heads = _gdn_intra([(q, k, v, z_ref[...], gb_ref[...])], block, hd, masks_ref)
    states = [[s_ref[s, e] for s in range(len(segs))] for e in range(2)]
    outs, leaving = _gdn_inter(heads, states, ng_ref[...], block, hd)
    og_ref[...] = jnp.concatenate(outs, axis=1).astype(BF16)
    for e in range(2):
        for s in range(len(segs)):
            so_ref[s, e] = leaving[e][s]


def _gdn_prompt(x, w_main, w_ab_t, a_log_rows, dt_bias_rows, conv_w, norm_g, layer, n_layers,
                prev_states, *, n_prompt, batch, hq, hd):
    d = x.shape[1]
    t = n_prompt // batch
    n_main = w_main.shape[1]
    conv_dim = conv_w.shape[1]
    val_w = 2 * hq * hd
    n_rows = w_ab_t.shape[0]
    sb = math.gcd(GDN_SEQ_BLOCK, t)
    n_groups = sb // LANES
    group_w = n_main // n_groups
    assert n_main % n_groups == 0 and group_w % (2 * hd) == 0
    bps = t // sb
    n_blocks = n_prompt // sb
    w_groups = w_main.reshape(d, n_groups, group_w).transpose(1, 0, 2)
    masks = _inverse_level_masks(LANES)

    def recurred(i):
        return jnp.maximum(i - 1, 0)

    body, more_specs, more_args, aliases = _carried(
        functools.partial(_gdn_prompt_body, blocks_per_seq=bps, n_pairs=hq, q_scale=hd ** -0.5, hd=hd),
        8, prev_states)
    return pl.pallas_call(
        body,
        grid=(n_blocks + 1,),
        in_specs=[pl.BlockSpec((sb, d), lambda i: (jnp.minimum(i, n_blocks - 1), 0)),
                  _resident(w_groups.shape), _resident(w_ab_t.shape), _resident((n_rows, 1)),
                  _resident((n_rows, 1)), _resident(conv_w.shape), _resident((1, hd)),
                  _resident(masks.shape)] + more_specs,
        out_specs=[pl.BlockSpec((sb, val_w), lambda i: (recurred(i), 0)),
                   pl.BlockSpec((None, None, 2 * hq, hd, hd),
                                lambda i: (recurred(i) // bps, layer, 0, 0, 0)),
                   pl.BlockSpec((SUBLANES, conv_dim), lambda i: (recurred(i), 0))],
        out_shape=[jax.ShapeDtypeStruct((n_prompt, val_w), BF16),
                   jax.ShapeDtypeStruct((batch, n_layers, 2 * hq, hd, hd), F32),
                   jax.ShapeDtypeStruct((n_blocks * SUBLANES, conv_dim), F32)],
        input_output_aliases=aliases,
        scratch_shapes=[pltpu.VMEM((n_groups, sb, group_w), F32), pltpu.VMEM((n_groups, sb, group_w), F32),
                        pltpu.VMEM((n_rows, sb), F32), pltpu.VMEM((n_rows, sb), F32),
                        pltpu.VMEM((sb, d), BF16), pltpu.VMEM((2 * hq, hd, hd), F32),
                        pltpu.VMEM((SUBLANES, conv_dim), F32)],
        compiler_params=_cparams(1),
        name="gdn_prompt",
    )(x, w_groups, w_ab_t, a_log_rows, dt_bias_rows, conv_w, norm_g.reshape(1, hd), masks,
      *more_args)


def _gdn_sample(p, gb, conv_w, norm_g, hist8, state_in, layer, prev_states, *, hq, hd, dec_seq):
    n_sample = p.shape[0]
    key_w = hq * hd
    val_w = 2 * key_w
    q_scale = hd ** -0.5
    v_blk0 = (2 * key_w) // (2 * hd)
    z_blk0 = (2 * key_w + val_w) // (2 * hd)
    ng = norm_g.reshape(1, hd)
    w_specs = [pl.BlockSpec((CONV_TAPS, hd), lambda a, h: (0, h)),
               pl.BlockSpec((CONV_TAPS, hd), lambda a, h: (0, hq + h)),
               pl.BlockSpec((CONV_TAPS, 2 * hd), lambda a, h: (0, v_blk0 + h))]
    dec_batch = n_sample // dec_seq
    seqs = LANES // dec_seq
    off = 0
    masks_s = _inverse_level_masks(dec_seq)
    state_spec = pl.BlockSpec((seqs, None, 2, hd, hd), lambda g, h: (g, layer, h, 0, 0))
    body, more_specs, more_args, aliases = _carried(
        functools.partial(_gdn_sample_body, block=dec_seq, q_scale=q_scale, hd=hd), 14, prev_states)
    og_s, s_s = pl.pallas_call(
        body,
        grid=(n_sample // LANES, hq),
        in_specs=[pl.BlockSpec((LANES, hd), lambda g, h: (off + g, h)),
                  pl.BlockSpec((LANES, hd), lambda g, h: (off + g, hq + h)),
                  pl.BlockSpec((LANES, 2 * hd), lambda g, h: (off + g, v_blk0 + h)),
                  pl.BlockSpec((LANES, 2 * hd), lambda g, h: (off + g, z_blk0 + h)),
                  pl.BlockSpec((SUBLANES, LANES), lambda g, h: (h, off + g))] + w_specs + [
                  _resident((1, hd)), _resident(masks_s.shape),
                  pl.BlockSpec((seqs, SUBLANES, hd), lambda g, h: (g, 0, h)),
                  pl.BlockSpec((seqs, SUBLANES, hd), lambda g, h: (g, 0, hq + h)),
                  pl.BlockSpec((seqs, SUBLANES, 2 * hd), lambda g, h: (g, 0, v_blk0 + h)),
                  state_spec] + more_specs,
        out_specs=[pl.BlockSpec((LANES, 2 * hd), lambda g, h: (g, h)), state_spec],
        out_shape=[jax.ShapeDtypeStruct((n_sample, val_w), BF16),
                   jax.ShapeDtypeStruct(state_in.shape, F32)],
        input_output_aliases=aliases,
        compiler_params=_cparams(2),
        name="gdn_sample",
    )(p, p, p, p, gb, conv_w, conv_w, conv_w, ng, masks_s, hist8, hist8, hist8, state_in,
      *more_args)
    return og_s, s_s


def _gdn_gate_rows(w_in, a_log, dt_bias, n_main, hq):
    d = w_in.shape[0]
    hv = a_log.shape[0]
    rep = hv // hq
    pad = SUBLANES - 2 * rep
    wa = w_in[:, n_main:n_main + hv].T.reshape(hq, rep, d)
    wb = w_in[:, n_main + hv:n_main + 2 * hv].T.reshape(hq, rep, d)
    w_ab_t = jnp.concatenate([wa, wb, jnp.zeros((hq, pad, d), w_in.dtype)], axis=1)
    zeros = jnp.zeros((hq, SUBLANES - rep), F32)
    alog = jnp.concatenate([a_log.reshape(hq, rep), zeros], axis=1).reshape(hq * SUBLANES, 1)
    dtb = jnp.concatenate([dt_bias.reshape(hq, rep), zeros], axis=1).reshape(hq * SUBLANES, 1)
    return w_ab_t.reshape(hq * SUBLANES, d).astype(BF16), alog, dtb


def kernel(x_prompt, x_sample, state_gla, state_gdn, state_gdn_conv, ln_g, ln_b, ffn1_w_gu, ffn1_w_d, ffn2_w_gu, ffn2_w_d, gla_w_in, gla_w_a1, gla_w_a2, gla_b_a, gla_norm_g, gla_w_o, gdn_w_in, gdn_conv_w, gdn_a_log, gdn_dt_bias, gdn_norm_g, gdn_w_o):
    batch, t, d = x_prompt.shape
    dec_batch, dec_seq, _ = x_sample.shape
    depth = ln_g.shape[0]
    alpha = (2.0 * depth) ** 0.25
    n_prompt, n_sample = batch * t, dec_batch * dec_seq
    tm = math.gcd(TOKEN_TILE, math.gcd(n_prompt, n_sample))
    assert t % LANES == 0 and t % GLA_CHUNK == 0 and n_sample % LANES == 0 and LANES % dec_seq == 0
    assert tm % LANES == 0

    gla_heads, gla_dk, gla_dv = state_gla.shape[2:]
    gla_qk_w = gla_heads * gla_dk
    hv, hd = state_gdn.shape[2:4]
    conv_dim = gdn_conv_w.shape[2]
    val_w = hv * hd
    hq = (conv_dim - val_w) // (2 * hd)
    assert hv == 2 * hq, "the gated-delta kernel pairs two v heads with each q/k head"
    n_main = conv_dim + val_w

    x = (x_prompt.reshape(n_prompt, d), x_sample.reshape(n_sample, d))
    new_gla_p, new_gla_s, new_gdn_p, new_gdn_s = None, None, None, None
    new_conv_p, new_conv_s = [], []
    for i in range(depth):
        x = _ffn_ln(x, ffn1_w_gu[i].astype(BF16), ffn1_w_d[i].astype(BF16), ln_g[i, 0], ln_b[i, 0],
                    tm=tm, alpha=alpha, n_prompt=n_prompt)
        j = i // 2
        if i % 2 == 0:
            rank = gla_w_a1.shape[2]
            w_a1 = jnp.pad(gla_w_a1[j], ((0, 0), (0, LANES - rank))).astype(BF16)
            w_a2 = jnp.pad(gla_w_a2[j], ((0, LANES - rank), (0, 0))).astype(BF16)
            w_in = gla_w_in[j].astype(BF16)
            og_p, new_gla_p = _gla_prompt(
                x, w_in, w_a1, w_a2, gla_b_a[j], gla_norm_g[j], j, state_gla.shape[1], new_gla_p,
                n_prompt=n_prompt, batch=batch, heads=gla_heads, dk=gla_dk, dv=gla_dv)
            p_s, la_s = _gla_proj(x, w_in, w_a1, w_a2, gla_b_a[j], tm=tm,
                                  first_tile=n_prompt // tm, n_tiles=n_sample // tm)
            og_s, new_gla_s = _gla_sample(p_s, la_s, gla_norm_g[j], state_gla, j, new_gla_s,
                                          heads=gla_heads, dk=gla_dk, dv=gla_dv, dec_seq=dec_seq)
            w_o = gla_w_o[j]
        else:
            w_ab_t, alog_rows, dtb_rows = _gdn_gate_rows(gdn_w_in[j], gdn_a_log[j], gdn_dt_bias[j],
                                                         n_main, hq)
            w_main = gdn_w_in[j][:, :n_main].astype(BF16)
            og_p, new_gdn_p, tails = _gdn_prompt(
                x, w_main, w_ab_t, alog_rows, dtb_rows, gdn_conv_w[j], gdn_norm_g[j], j,
                state_gdn.shape[1], new_gdn_p, n_prompt=n_prompt, batch=batch, hq=hq, hd=hd)
            p_s, gb_s = _gdn_proj(x, w_main, w_ab_t, alog_rows, dtb_rows, tm=tm,
                                  first_tile=n_prompt // tm, n_tiles=n_sample // tm)
            hist8 = jnp.pad(state_gdn_conv[:, j],
                            ((0, 0), (SUBLANES - (CONV_TAPS - 1), 0), (0, 0)))
            og_s, new_gdn_s = _gdn_sample(p_s, gb_s, gdn_conv_w[j], gdn_norm_g[j], hist8, state_gdn, j,
                                          new_gdn_s, hq=hq, hd=hd, dec_seq=dec_seq)
            keep = np.arange(-(CONV_TAPS - 1), 0)
            rows_p = ((np.arange(batch) + 1) * (tails.shape[0] // batch))[:, None] + keep
            rows_s = ((np.arange(dec_batch) + 1) * dec_seq)[:, None] + keep
            new_conv_p.append(jnp.take(tails, rows_p.reshape(-1), axis=0)
                              .reshape(batch, CONV_TAPS - 1, conv_dim))
            new_conv_s.append(jnp.take(p_s, rows_s.reshape(-1), axis=0)[:, :conv_dim]
                              .reshape(dec_batch, CONV_TAPS - 1, conv_dim))
            w_o = gdn_w_o[j]
        x = _ffn_ln(x, ffn2_w_gu[i].astype(BF16), ffn2_w_d[i].astype(BF16), ln_g[i, 2], ln_b[i, 2],
                    tm=tm, alpha=alpha, n_prompt=n_prompt, split_out=(i == depth - 1),
                    mixer=(og_p, og_s, w_o.astype(BF16), ln_g[i, 1], ln_b[i, 1]))
    y_prompt, y_sample = x
    return (y_prompt.reshape(batch, t, d), y_sample.reshape(dec_batch, dec_seq, d),
            new_gla_p, new_gdn_p, jnp.stack(new_conv_p, 1),
            new_gla_s, new_gdn_s, jnp.stack(new_conv_s, 1))
```

```python
import functools
import math

import numpy as np

import jax
import jax.numpy as jnp
from jax import lax
from jax.experimental import pallas as pl
from jax.experimental.pallas import tpu as pltpu

F32 = jnp.float32
BF16 = jnp.bfloat16

LN_EPS = 1e-5
RMS_EPS = 1e-6
L2_EPS = 1e-6
FFN_RES = 0.5
GLA_GATE_TAU = 16.0
GLA_CHUNK = 64
CONV_TAPS = 4

LANES = 128
SUBLANES = 8
VMEM_LIMIT_BYTES = 60 * 1024 * 1024
TOKEN_TILE = 512
MATMUL_N_CHUNK = 1024
GDN_SEQ_BLOCK = 512
GLA_SEQ_BLOCK = 512


def _cparams(n_axes):
    return pltpu.CompilerParams(
        dimension_semantics=("arbitrary",) * n_axes, vmem_limit_bytes=VMEM_LIMIT_BYTES)


def _resident(shape):
    zeros = (0,) * len(shape)
    return pl.BlockSpec(shape, lambda *_: zeros, pipeline_mode=pl.Buffered(1))


def _dot(a, b):
    return jnp.dot(a.astype(BF16), b.astype(BF16), preferred_element_type=F32)


def _dot_nt(a, b):
    return lax.dot_general(a.astype(BF16), b.astype(BF16), (((1,), (1,)), ((), ())),
                           preferred_element_type=F32)


def _silu(x):
    return x * jax.nn.sigmoid(x)


def _softplus(x):
    return jnp.maximum(x, 0.0) + jnp.log1p(jnp.exp(-jnp.abs(x)))


def _layer_norm_rows(y, g, b):
    mu = jnp.mean(y, axis=-1, keepdims=True)
    d = y - mu
    var = jnp.mean(d * d, axis=-1, keepdims=True)
    return d * lax.rsqrt(var + LN_EPS) * g + b


def _rms_norm_rows(o, g):
    return o * lax.rsqrt(jnp.mean(o * o, axis=-1, keepdims=True) + RMS_EPS) * g


def _l2_norm_rows(x):
    return x * lax.rsqrt(jnp.sum(x * x, axis=-1, keepdims=True) + L2_EPS)


def _col_chunks(n, step=MATMUL_N_CHUNK):
    return [(c, min(c + step, n)) for c in range(0, n, step)]


def _ffn_ln_body(*refs, d_ff, alpha, n_in, has_mixer, n_prompt_tiles):
    x_refs, refs = refs[:n_in], refs[n_in:]
    is_prompt = pl.program_id(0) < n_prompt_tiles
    if has_mixer:
        (ogp_ref, ogs_ref, wo_ref, g1_ref, b1_ref), refs = refs[:5], refs[5:]
    wgu_ref, wd_ref, g_ref, b_ref = refs[:4]
    o_refs, h_ref = refs[4:-1], refs[-1]
    x = x_refs[0][...]
    if n_in == 2:
        x = jnp.where(is_prompt, x, x_refs[1][...])
    if has_mixer:
        og = jnp.where(is_prompt, ogp_ref[...], ogs_ref[...])
        x = _layer_norm_rows(alpha * x + jnp.dot(og, wo_ref[...], preferred_element_type=F32),
                             g1_ref[...], b1_ref[...])
    xb = x.astype(BF16)
    for c0, c1 in _col_chunks(d_ff):
        gate = jnp.dot(xb, wgu_ref[:, c0:c1], preferred_element_type=F32)
        up = jnp.dot(xb, wgu_ref[:, d_ff + c0:d_ff + c1], preferred_element_type=F32)
        h_ref[:, c0:c1] = (_silu(gate) * up).astype(BF16)
    y = jnp.dot(h_ref[...], wd_ref[...], preferred_element_type=F32)
    out = _layer_norm_rows(alpha * x + FFN_RES * y, g_ref[...], b_ref[...])
    if len(o_refs) == 1:
        o_refs[0][...] = out
    else:
        o_refs[1][...] = out

        @pl.when(is_prompt)
        def _():
            o_refs[0][...] = out


def _ffn_ln(xs, w_gu, w_d, g, b, *, tm, alpha, n_prompt, mixer=None, split_out=False):
    xs = xs if isinstance(xs, (tuple, list)) else (xs,)
    d = xs[0].shape[1]
    n = sum(x.shape[0] for x in xs)
    d_ff = w_d.shape[0]
    ntp = n_prompt // tm

    def pair(width):
        return [pl.BlockSpec((tm, width), lambda i: (jnp.minimum(i, ntp - 1), 0)),
                pl.BlockSpec((tm, width), lambda i: (jnp.maximum(i - ntp, 0), 0))]

    tile = pl.BlockSpec((tm, d), lambda i: (i, 0))
    vec = _resident((1, d))
    pair_shape = [jax.ShapeDtypeStruct((n_prompt, d), F32), jax.ShapeDtypeStruct((n - n_prompt, d), F32)]
    mixer_specs, mixer_args = [], []
    if mixer is not None:
        og_prompt, og_sample, w_o, g1, b1 = mixer
        mixer_specs = pair(w_o.shape[0]) + [_resident(w_o.shape), vec, vec]
        mixer_args = [og_prompt, og_sample, w_o, g1.reshape(1, d), b1.reshape(1, d)]
    return pl.pallas_call(
        functools.partial(_ffn_ln_body, d_ff=d_ff, alpha=alpha, n_in=len(xs),
                          has_mixer=mixer is not None, n_prompt_tiles=ntp),
        grid=(n // tm,),
        in_specs=(pair(d) if len(xs) == 2 else [tile]) + mixer_specs + [
            _resident(w_gu.shape), _resident(w_d.shape), vec, vec],
        out_specs=pair(d) if split_out else tile,
        out_shape=pair_shape if split_out else jax.ShapeDtypeStruct((n, d), F32),
        scratch_shapes=[pltpu.VMEM((tm, d_ff), BF16)],
        compiler_params=_cparams(1),
        name="ffn_ln",
    )(*xs, *mixer_args, w_gu, w_d, g.reshape(1, d), b.reshape(1, d))


def _gla_log_decay(xb, wa1, wa2, ba):
    low = jnp.dot(xb, wa1, preferred_element_type=F32)
    logits = _dot(low, wa2) + ba
    log_sig = jnp.minimum(logits, 0.0) - jnp.log1p(jnp.exp(-jnp.abs(logits)))
    return log_sig / GLA_GATE_TAU


def _gla_proj_body(x_ref, win_ref, wa1_ref, wa2_ref, ba_ref, p_ref, la_ref):
    xb = x_ref[...].astype(BF16)
    for c0, c1 in _col_chunks(win_ref.shape[1]):
        p_ref[:, c0:c1] = jnp.dot(xb, win_ref[:, c0:c1], preferred_element_type=F32)
    la_ref[...] = _gla_log_decay(xb, wa1_ref[...], wa2_ref[...], ba_ref[...])


def _gla_proj(x, w_in, w_a1, w_a2, b_a, *, tm, first_tile, n_tiles):
    d = x.shape[1]
    n = n_tiles * tm
    n_out = w_in.shape[1]
    qk_w = w_a2.shape[1]
    return pl.pallas_call(
        _gla_proj_body,
        grid=(n_tiles,),
        in_specs=[pl.BlockSpec((tm, d), lambda i: (first_tile + i, 0)), _resident(w_in.shape),
                  _resident(w_a1.shape), _resident(w_a2.shape), _resident((1, qk_w))],
        out_specs=[pl.BlockSpec((tm, n_out), lambda i: (i, 0)),
                   pl.BlockSpec((tm, qk_w), lambda i: (i, 0))],
        out_shape=[jax.ShapeDtypeStruct((n, n_out), F32), jax.ShapeDtypeStruct((n, qk_w), F32)],
        compiler_params=_cparams(1),
        name="gla_proj",
    )(x, w_in, w_a1, w_a2, b_a.reshape(1, qk_w))


def _gdn_gate_values(xb, wab, alog, dtb):
    ab = lax.dot_general(wab, xb, (((1,), (1,)), ((), ())), preferred_element_type=F32)
    kind = _iota2(ab.shape, 0) & (SUBLANES - 1)
    g = -jnp.exp(alog) * _softplus(ab + dtb)
    beta = jax.nn.sigmoid(ab)
    return jnp.where(kind < 2, g, jnp.where(kind < 4, beta, 0.0))


def _gdn_proj_body(x_ref, wmain_ref, wab_ref, alog_ref, dtb_ref, p_ref, gb_ref):
    xb = x_ref[...].astype(BF16)
    for c0, c1 in _col_chunks(wmain_ref.shape[1]):
        p_ref[:, c0:c1] = jnp.dot(xb, wmain_ref[:, c0:c1], preferred_element_type=F32)
    gb_ref[...] = _gdn_gate_values(xb, wab_ref[...], alog_ref[...], dtb_ref[...])


def _gdn_proj(x, w_main, w_ab_t, a_log_rows, dt_bias_rows, *, tm, first_tile, n_tiles):
    d = x.shape[1]
    n = n_tiles * tm
    n_out = w_main.shape[1]
    n_rows = w_ab_t.shape[0]
    return pl.pallas_call(
        _gdn_proj_body,
        grid=(n_tiles,),
        in_specs=[pl.BlockSpec((tm, d), lambda i: (first_tile + i, 0)), _resident(w_main.shape),
                  _resident(w_ab_t.shape), _resident((n_rows, 1)), _resident((n_rows, 1))],
        out_specs=[pl.BlockSpec((tm, n_out), lambda i: (i, 0)),
                   pl.BlockSpec((n_rows, tm), lambda i: (0, i))],
        out_shape=[jax.ShapeDtypeStruct((n, n_out), F32), jax.ShapeDtypeStruct((n_rows, n), F32)],
        compiler_params=_cparams(1),
        name="gdn_proj",
    )(x, w_main, w_ab_t, a_log_rows, dt_bias_rows)


def _iota2(shape, dim):
    return lax.broadcasted_iota(jnp.int32, shape, dim)


def _block_of(idx, block):
    return idx >> (block.bit_length() - 1)


def _block_cumsum(x, block, axis):
    pos = _iota2(x.shape, axis) & (block - 1)
    shift = 1
    while shift < block:
        x = x + jnp.where(pos >= shift, pltpu.roll(x, shift, axis), 0.0)
        shift *= 2
    return x


def _col_form(row):
    return jnp.transpose(jnp.broadcast_to(row, (LANES, LANES)))


def _segments(block):
    return [(r, r + block) for r in range(0, LANES, block)]


def _last_row_of_segments(x, block):
    return jnp.concatenate(
        [jnp.broadcast_to(x[r1 - 1:r1, :], (r1 - r0, x.shape[1])) for r0, r1 in _segments(block)],
        axis=0)


def _rows_only(x, r0, r1):
    rows = _iota2(x.shape, 0)
    return jnp.where((rows >= r0) & (rows < r1), x, 0.0)


def _gla_tiles(heads, ng, block, chained):
    row = _iota2((LANES, LANES), 0)
    col = _iota2((LANES, LANES), 1)
    causal = (_block_of(row, block) == _block_of(col, block)) & (row >= col)
    segs = _segments(block)
    bcums = [_block_cumsum(la, block, 0) for (_, _, _, la, _, _) in heads]
    q_decs = [q * jnp.exp(b) for (q, _, _, _, _, _), b in zip(heads, bcums)]
    atts = [jnp.where(causal, _dot_nt(qd, k * jnp.exp(-b)), 0.0)
            for qd, (_, k, _, _, _, _), b in zip(q_decs, heads, bcums)]
    intras = [_dot(att, v) for att, (_, _, v, _, _, _) in zip(atts, heads)]
    k_end_ts = [jnp.transpose(k * jnp.exp(_last_row_of_segments(b, block) - b))
                for (_, k, _, _, _, _), b in zip(heads, bcums)]
    lane_reps = heads[0][2].shape[1] // LANES
    states = [list(st) for (_, _, _, _, _, st) in heads]
    cur = [st[0] for st in states]
    outs = [[] for _ in heads]
    leaving = [[] for _ in heads]
    for s, (r0, r1) in enumerate(segs):
        if not chained:
            cur = [st[s] for st in states]
        inter = [_dot(qd[r0:r1], c) for qd, c in zip(q_decs, cur)]
        upd = [_dot(kt, _rows_only(v, r0, r1)) for kt, (_, _, v, _, _, _) in zip(k_end_ts, heads)]
        for i, b in enumerate(bcums):
            outs[i].append(intras[i][r0:r1] + inter[i])
            decay = jnp.concatenate([_col_form(jnp.exp(b[r1 - 1:r1, :]))] * lane_reps, axis=1)
            cur[i] = decay * cur[i] + upd[i]
            if not chained:
                leaving[i].append(cur[i])
    results = []
    for i, (_, _, _, _, r, _) in enumerate(heads):
        o = _rms_norm_rows(jnp.concatenate(outs[i], axis=0), ng)
        results.append((o * _silu(r), leaving[i] if not chained else [cur[i]]))
    return results


def _gla_prompt_body(x_ref, wg_ref, wa1_ref, wa2_ref, ba_ref, ng_ref, og_ref, so_ref,
                     p0_ref, p1_ref, la0_ref, la1_ref, xb_ref, s_ref, *,
                     blocks_per_seq, n_heads, dk, dv, q_scale):
    step = pl.program_id(0)
    n_groups, _, group_w = p0_ref.shape
    qk_w = n_heads * dk

    @pl.when(step == 0)
    def _():
        p1_ref[...] = jnp.zeros_like(p1_ref)
        la1_ref[...] = jnp.zeros_like(la1_ref)
        s_ref[...] = jnp.zeros_like(s_ref)

    block_in_seq = lax.rem(jnp.maximum(step - 1, 0), blocks_per_seq)
    seq_start = block_in_seq == 0

    @pl.when(seq_start)
    def _():
        s_ref[...] = jnp.zeros_like(s_ref)

    xb_ref[...] = x_ref[...].astype(BF16)

    def cols_of(c0, width):
        g, off = divmod(c0, group_w)
        assert off + width <= group_w
        return g, slice(off, off + width)

    def run(p_cur, la_cur, p_nxt, la_nxt):
        la_nxt[...] = _gla_log_decay(xb_ref[...], wa1_ref[...], wa2_ref[...], ba_ref[...])

        def tile(t, carry):
            rows = pl.ds(pl.multiple_of(t * LANES, LANES), LANES)
            p_nxt[t] = jnp.dot(xb_ref[...], wg_ref[t], preferred_element_type=F32)
            heads = []
            for h in range(n_heads):
                gq, cq = cols_of(h * dk, dk)
                gk, ck = cols_of(qk_w + h * dk, dk)
                gv, cv = cols_of(2 * qk_w + h * dv, dv)
                gr, cr = cols_of(2 * qk_w + n_heads * dv + h * dv, dv)
                heads.append((p_cur[gq, rows, cq] * q_scale, p_cur[gk, rows, ck],
                              p_cur[gv, rows, cv], la_cur[rows, h * dk:(h + 1) * dk],
                              p_cur[gr, rows, cr], [s_ref[h]]))
            for h, (o, leaving) in enumerate(_gla_tiles(heads, ng_ref[...], GLA_CHUNK, True)):
                og_ref[rows, h * dv:(h + 1) * dv] = o.astype(BF16)
                s_ref[h] = leaving[0]
            return carry

        lax.fori_loop(0, n_groups, tile, 0)

    parity = lax.rem(step, 2)

    @pl.when(parity == 0)
    def _():
        run(p1_ref, la1_ref, p0_ref, la0_ref)

    @pl.when(parity == 1)
    def _():
        run(p0_ref, la0_ref, p1_ref, la1_ref)

    @pl.when(block_in_seq == blocks_per_seq - 1)
    def _():
        so_ref[...] = s_ref[...]


def _gla_sample_body(q_ref, k_ref, v_ref, r_ref, la_ref, ng_ref, s_ref, og_ref, so_ref, *, block,
                     q_scale):
    n_seq = LANES // block
    [(o, leaving)] = _gla_tiles(
        [(q_ref[...] * q_scale, k_ref[...], v_ref[...], la_ref[...], r_ref[...],
          [s_ref[i] for i in range(n_seq)])], ng_ref[...], block, False)
    og_ref[...] = o.astype(BF16)
    for i in range(n_seq):
        so_ref[i] = leaving[i]


def _carried(body, n_inputs, prev):
    if prev is None:
        return body, [], [], {}

    def skipping_carried_input(*refs):
        return body(*refs[:n_inputs], *refs[n_inputs + 1:])

    return skipping_carried_input, [pl.BlockSpec(memory_space=pl.ANY)], [prev], {n_inputs: 1}


def _gla_prompt(x, w_in, w_a1, w_a2, b_a, norm_g, layer, n_layers, prev_states, *, n_prompt, batch,
                heads, dk, dv):
    d = x.shape[1]
    t = n_prompt // batch
    n_out = w_in.shape[1]
    qk_w, v_w = heads * dk, heads * dv
    sb = math.gcd(GLA_SEQ_BLOCK, t)
    n_groups = sb // LANES
    group_w = n_out // n_groups
    assert n_out % n_groups == 0 and group_w % dv == 0
    bps = t // sb
    n_blocks = n_prompt // sb
    w_groups = w_in.reshape(d, n_groups, group_w).transpose(1, 0, 2)

    def recurred(i):
        return jnp.maximum(i - 1, 0)

    body, more_specs, more_args, aliases = _carried(
        functools.partial(_gla_prompt_body, blocks_per_seq=bps, n_heads=heads, dk=dk, dv=dv,
                          q_scale=dk ** -0.5), 6, prev_states)
    return pl.pallas_call(
        body,
        grid=(n_blocks + 1,),
        in_specs=[pl.BlockSpec((sb, d), lambda i: (jnp.minimum(i, n_blocks - 1), 0)),
                  _resident(w_groups.shape), _resident(w_a1.shape), _resident(w_a2.shape),
                  _resident((1, qk_w)), _resident((1, dv))] + more_specs,
        out_specs=[pl.BlockSpec((sb, v_w), lambda i: (recurred(i), 0)),
                   pl.BlockSpec((None, None, heads, dk, dv),
                                lambda i: (recurred(i) // bps, layer, 0, 0, 0))],
        out_shape=[jax.ShapeDtypeStruct((n_prompt, v_w), BF16),
                   jax.ShapeDtypeStruct((batch, n_layers, heads, dk, dv), F32)],
        scratch_shapes=[pltpu.VMEM((n_groups, sb, group_w), F32), pltpu.VMEM((n_groups, sb, group_w), F32),
                        pltpu.VMEM((sb, qk_w), F32), pltpu.VMEM((sb, qk_w), F32),
                        pltpu.VMEM((sb, d), BF16), pltpu.VMEM((heads, dk, dv), F32)],
        input_output_aliases=aliases,
        compiler_params=_cparams(1),
        name="gla_prompt",
    )(x, w_groups, w_a1, w_a2, b_a.reshape(1, qk_w), norm_g.reshape(1, dv), *more_args)


def _gla_sample(p, la, norm_g, state_in, layer, prev_states, *, heads, dk, dv, dec_seq):
    n_sample = p.shape[0]
    qk_w, v_w = heads * dk, heads * dv
    v_blk0 = (2 * qk_w) // dv
    r_blk0 = (2 * qk_w + v_w) // dv
    ng = norm_g.reshape(1, dv)
    seqs = LANES // dec_seq
    off = 0
    state_spec = pl.BlockSpec((seqs, None, None, dk, dv), lambda g, h: (g, layer, h, 0, 0))
    body, more_specs, more_args, aliases = _carried(
        functools.partial(_gla_sample_body, block=dec_seq, q_scale=dk ** -0.5), 7, prev_states)
    og_s, s_s = pl.pallas_call(
        body,
        grid=(n_sample // LANES, heads),
        in_specs=[pl.BlockSpec((LANES, dk), lambda g, h: (off + g, h)),
                  pl.BlockSpec((LANES, dk), lambda g, h: (off + g, heads + h)),
                  pl.BlockSpec((LANES, dv), lambda g, h: (off + g, v_blk0 + h)),
                  pl.BlockSpec((LANES, dv), lambda g, h: (off + g, r_blk0 + h)),
                  pl.BlockSpec((LANES, dk), lambda g, h: (off + g, h)),
                  _resident((1, dv)), state_spec] + more_specs,
        out_specs=[pl.BlockSpec((LANES, dv), lambda g, h: (g, h)), state_spec],
        out_shape=[jax.ShapeDtypeStruct((n_sample, v_w), BF16),
                   jax.ShapeDtypeStruct(state_in.shape, F32)],
        input_output_aliases=aliases,
        compiler_params=_cparams(2),
        name="gla_sample",
    )(p, p, p, p, la, ng, state_in, *more_args)
    return og_s, s_s


def _causal_conv(prev8, x, w):
    n = x.shape[0]
    xs = jnp.concatenate([prev8, x], axis=0)
    acc = xs[SUBLANES:, :] * w[CONV_TAPS - 1:CONV_TAPS, :]
    for j in range(CONV_TAPS - 1):
        shifted = pltpu.roll(xs, CONV_TAPS - 1 - j, 0)
        acc = acc + shifted[SUBLANES:SUBLANES + n, :] * w[j:j + 1, :]
    return acc


def _inverse_level_masks(block):
    idx = np.arange(LANES)
    masks = []
    size = 1
    while size < block:
        pair = (idx[:, None] // (2 * size)) == (idx[None, :] // (2 * size))
        masks.append(pair & ((idx[:, None] // size) != (idx[None, :] // size)))
        size *= 2
    return jnp.asarray(np.stack(masks).astype(np.float32))


def _unit_lower_inverses(mats, masks_ref):
    row = _iota2((LANES, LANES), 0)
    col = _iota2((LANES, LANES), 1)
    eye = jnp.where(row == col, 1.0, 0.0)
    xs = [eye - a * masks_ref[0] for a in mats]
    mats_b = [a.astype(BF16) for a in mats]
    for level in range(1, masks_ref.shape[0]):
        mask = masks_ref[level]
        xbs = [x.astype(BF16) for x in xs]
        ys = [jnp.dot(xb, ab, preferred_element_type=F32) for xb, ab in zip(xbs, mats_b)]
        xs = [x - jnp.dot(y.astype(BF16), xb, preferred_element_type=F32) * mask
              for x, y, xb in zip(xs, ys, xbs)]
    return xs


def _gdn_intra(pairs, block, hd, masks_ref):
    row = _iota2((LANES, LANES), 0)
    col = _iota2((LANES, LANES), 1)
    same = _block_of(row, block) == _block_of(col, block)
    incl = same & (row >= col)
    strict = same & (row > col)
    kks = [_dot_nt(k, k) for (_, k, _, _, _) in pairs]
    qks = [_dot_nt(q, k) for (q, k, _, _, _) in pairs]
    heads = []
    for (q, k, v, z, gb), kk, qk in zip(pairs, kks, qks):
        gcum = _block_cumsum(gb, block, 1)
        for e in range(2):
            g_row = gcum[e:e + 1, :]
            g_col = _col_form(g_row)
            b_col = _col_form(gb[2 + e:3 + e, :])
            decay = jnp.where(incl, jnp.exp(g_col - jnp.broadcast_to(g_row, (LANES, LANES))), 0.0)
            eg = jnp.exp(g_col)
            heads.append(dict(
                q_dec=q * eg, att=qk * decay, g_col=g_col,
                a=jnp.where(strict, b_col * kk * decay, 0.0),
                rhs=jnp.concatenate([k * (b_col * eg), v[:, e * hd:(e + 1) * hd] * b_col], axis=1),
                k_end=k * jnp.exp(_last_row_of_segments(g_col, block) - g_col),
                z=z[:, e * hd:(e + 1) * hd]))
    t_invs = _unit_lower_inverses([h["a"] for h in heads], masks_ref)
    for h, t_inv in zip(heads, t_invs):
        h["wu"] = _dot(t_inv, h["rhs"])
    for h in heads:
        h["k_end_t"] = jnp.transpose(h["k_end"])
    return heads


def _gdn_inter(heads, states, ng, block, hd):
    segs = _segments(block)
    ws_qs = [[_dot(jnp.concatenate([h["wu"][r0:r1, :hd], h["q_dec"][r0:r1]], axis=0), st[s])
              for s, (r0, r1) in enumerate(segs)] for h, st in zip(heads, states)]
    v_news = []
    for h, per_seg in zip(heads, ws_qs):
        parts = [h["wu"][r0:r1, hd:] - x[:r1 - r0] for (r0, r1), x in zip(segs, per_seg)]
        v_news.append(jnp.concatenate(parts, axis=0) if len(parts) > 1 else parts[0])
    o_intra = [_dot(h["att"], v_new) for h, v_new in zip(heads, v_news)]
    updates = [[_dot(h["k_end_t"], v_new if len(segs) == 1 else _rows_only(v_new, r0, r1))
                for (r0, r1) in segs] for h, v_new in zip(heads, v_news)]
    outs, leaving = [], []
    for h, st, per_seg, oi, upd in zip(heads, states, ws_qs, o_intra, updates):
        qs = [x[r1 - r0:] for (r0, r1), x in zip(segs, per_seg)]
        o = oi + (jnp.concatenate(qs, axis=0) if len(qs) > 1 else qs[0])
        outs.append(_rms_norm_rows(o, ng) * _silu(h["z"]))
        leaving.append([jnp.exp(h["g_col"][r1 - 1:r1, :]) * st[s] + upd[s]
                        for s, (r0, r1) in enumerate(segs)])
    return outs, leaving


def _gdn_prompt_body(x_ref, wg_ref, wab_ref, alog_ref, dtb_ref, cw_ref, ng_ref, masks_ref,
                     og_ref, so_ref, tail_ref, p0_ref, p1_ref, gb0_ref, gb1_ref, xb_ref, s_ref, prev_ref,
                     *, blocks_per_seq, n_pairs, q_scale, hd):
    step = pl.program_id(0)
    n_groups, sb, group_w = p0_ref.shape
    n_heads = 2 * n_pairs
    key_w = n_pairs * hd

    @pl.when(step == 0)
    def _():
        p1_ref[...] = jnp.zeros_like(p1_ref)
        gb1_ref[...] = jnp.zeros_like(gb1_ref)
        s_ref[...] = jnp.zeros_like(s_ref)
        prev_ref[...] = jnp.zeros_like(prev_ref)

    block_in_seq = lax.rem(jnp.maximum(step - 1, 0), blocks_per_seq)
    seq_start = block_in_seq == 0

    @pl.when(seq_start)
    def _():
        s_ref[...] = jnp.zeros_like(s_ref)

    xb_ref[...] = x_ref[...].astype(BF16)

    def cols_of(c0, width):
        g, off = divmod(c0, group_w)
        assert off + width <= group_w
        return g, slice(off, off + width)

    def run(p_cur, gb_cur, p_nxt, gb_nxt):
        gb_nxt[...] = _gdn_gate_values(xb_ref[...], wab_ref[...], alog_ref[...], dtb_ref[...])

        def tile(t, carry):
            r0 = pl.multiple_of(t * LANES, LANES)
            rows = pl.ds(r0, LANES)
            p_nxt[t] = jnp.dot(xb_ref[...], wg_ref[t], preferred_element_type=F32)
            prev = pl.ds(pl.multiple_of(jnp.maximum(r0 - SUBLANES, 0), SUBLANES), SUBLANES)

            def conv(c0, width):
                g, cols = cols_of(c0, width)
                carried = jnp.where(seq_start, 0.0, prev_ref[:, c0:c0 + width])
                prev8 = jnp.where(t > 0, p_cur[g, prev, cols], carried)
                return _silu(_causal_conv(prev8, p_cur[g, rows, cols], cw_ref[:, c0:c0 + width]))

            pairs = []
            for p in range(n_pairs):
                q = _l2_norm_rows(conv(p * hd, hd)) * q_scale
                k = _l2_norm_rows(conv(key_w + p * hd, hd))
                v = conv(2 * key_w + 2 * p * hd, 2 * hd)
                gz, zc = cols_of(4 * key_w + 2 * p * hd, 2 * hd)
                pairs.append((q, k, v, p_cur[gz, rows, zc],
                              gb_cur[p * SUBLANES:(p + 1) * SUBLANES, rows]))
            heads = _gdn_intra(pairs, LANES, hd, masks_ref)
            outs, leaving = _gdn_inter(heads, [[s_ref[h]] for h in range(n_heads)], ng_ref[...],
                                       LANES, hd)
            for h in range(n_heads):
                og_ref[rows, h * hd:(h + 1) * hd] = outs[h].astype(BF16)
                s_ref[h] = leaving[h][0]
            return carry

        lax.fori_loop(0, n_groups, tile, 0)
        for c0 in range(0, 4 * key_w, hd):
            g, cols = cols_of(c0, hd)
            tail = p_cur[g, sb - SUBLANES:sb, cols]
            tail_ref[:, c0:c0 + hd] = tail
            prev_ref[:, c0:c0 + hd] = tail

    parity = lax.rem(step, 2)

    @pl.when(parity == 0)
    def _():
        run(p1_ref, gb1_ref, p0_ref, gb0_ref)

    @pl.when(parity == 1)
    def _():
        run(p0_ref, gb0_ref, p1_ref, gb1_ref)

    @pl.when(block_in_seq == blocks_per_seq - 1)
    def _():
        so_ref[...] = s_ref[...]


def _gdn_sample_body(q_ref, k_ref, v_ref, z_ref, gb_ref, wq_ref, wk_ref, wv_ref, ng_ref, masks_ref,
                     hq_ref, hk_ref, hv_ref, s_ref, og_ref, so_ref, *, block, q_scale, hd):
    segs = _segments(block)

    def conv(ref, hist_ref, w_ref):
        x = ref[...]
        w = w_ref[...]
        return _silu(jnp.concatenate(
            [_causal_conv(hist_ref[s], x[r0:r1], w) for s, (r0, r1) in enumerate(segs)], axis=0))

    q = _l2_norm_rows(conv(q_ref, hq_ref, wq_ref)) * q_scale
    k = _l2_norm_rows(conv(k_ref, hk_ref, wk_ref))
    v = conv(v_ref, hv_ref, wv_ref)
    heads = _gdn_intra([(q, k, v, z_ref[...], gb_ref[...])], block, hd, masks_ref)
    states = [[s_ref[s, e] for s in range(len(segs))] for e in range(2)]
    outs, leaving = _gdn_inter(heads, states, ng_ref[...], block, hd)
    og_ref[...] = jnp.concatenate(outs, axis=1).astype(BF16)
    for e in range(2):
        for s in range(len(segs)):
            so_ref[s, e] = leaving[e][s]


def _gdn_prompt(x, w_main, w_ab_t, a_log_rows, dt_bias_rows, conv_w, norm_g, layer, n_layers,
                prev_states, *, n_prompt, batch, hq, hd):
    d = x.shape[1]
    t = n_prompt // batch
    n_main = w_main.shape[1]
    conv_dim = conv_w.shape[1]
    val_w = 2 * hq * hd
    n_rows = w_ab_t.shape[0]
    sb = math.gcd(GDN_SEQ_BLOCK, t)
    n_groups = sb // LANES
    group_w = n_main // n_groups
    assert n_main % n_groups == 0 and group_w % (2 * hd) == 0
    bps = t // sb
    n_blocks = n_prompt // sb
    w_groups = w_main.reshape(d, n_groups, group_w).transpose(1, 0, 2)
    masks = _inverse_level_masks(LANES)

    def recurred(i):
        return jnp.maximum(i - 1, 0)

    body, more_specs, more_args, aliases = _carried(
        functools.partial(_gdn_prompt_body, blocks_per_seq=bps, n_pairs=hq, q_scale=hd ** -0.5, hd=hd),
        8, prev_states)
    return pl.pallas_call(
        body,
        grid=(n_blocks + 1,),
        in_specs=[pl.BlockSpec((sb, d), lambda i: (jnp.minimum(i, n_blocks - 1), 0)),
                  _resident(w_groups.shape), _resident(w_ab_t.shape), _resident((n_rows, 1)),
                  _resident((n_rows, 1)), _resident(conv_w.shape), _resident((1, hd)),
                  _resident(masks.shape)] + more_specs,
        out_specs=[pl.BlockSpec((sb, val_w), lambda i: (recurred(i), 0)),
                   pl.BlockSpec((None, None, 2 * hq, hd, hd),
                                lambda i: (recurred(i) // bps, layer, 0, 0, 0)),
                   pl.BlockSpec((SUBLANES, conv_dim), lambda i: (recurred(i), 0))],
        out_shape=[jax.ShapeDtypeStruct((n_prompt, val_w), BF16),
                   jax.ShapeDtypeStruct((batch, n_layers, 2 * hq, hd, hd), F32),
                   jax.ShapeDtypeStruct((n_blocks * SUBLANES, conv_dim), F32)],
        input_output_aliases=aliases,
        scratch_shapes=[pltpu.VMEM((n_groups, sb, group_w), F32), pltpu.VMEM((n_groups, sb, group_w), F32),
                        pltpu.VMEM((n_rows, sb), F32), pltpu.VMEM((n_rows, sb), F32),
                        pltpu.VMEM((sb, d), BF16), pltpu.VMEM((2 * hq, hd, hd), F32),
                        pltpu.VMEM((SUBLANES, conv_dim), F32)],
        compiler_params=_cparams(1),
        name="gdn_prompt",
    )(x, w_groups, w_ab_t, a_log_rows, dt_bias_rows, conv_w, norm_g.reshape(1, hd), masks,
      *more_args)


def _gdn_sample(p, gb, conv_w, norm_g, hist8, state_in, layer, prev_states, *, hq, hd, dec_seq):
    n_sample = p.shape[0]
    key_w = hq * hd
    val_w = 2 * key_w
    q_scale = hd ** -0.5
    v_blk0 = (2 * key_w) // (2 * hd)
    z_blk0 = (2 * key_w + val_w) // (2 * hd)
    ng = norm_g.reshape(1, hd)
    w_specs = [pl.BlockSpec((CONV_TAPS, hd), lambda a, h: (0, h)),
               pl.BlockSpec((CONV_TAPS, hd), lambda a, h: (0, hq + h)),
               pl.BlockSpec((CONV_TAPS, 2 * hd), lambda a, h: (0, v_blk0 + h))]
    dec_batch = n_sample // dec_seq
    seqs = LANES // dec_seq
    off = 0
    masks_s = _inverse_level_masks(dec_seq)
    state_spec = pl.BlockSpec((seqs, None, 2, hd, hd), lambda g, h: (g, layer, h, 0, 0))
    body, more_specs, more_args, aliases = _carried(
        functools.partial(_gdn_sample_body, block=dec_seq, q_scale=q_scale, hd=hd), 14, prev_states)
    og_s, s_s = pl.pallas_call(
        body,
        grid=(n_sample // LANES, hq),
        in_specs=[pl.BlockSpec((LANES, hd), lambda g, h: (off + g, h)),
                  pl.BlockSpec((LANES, hd), lambda g, h: (off + g, hq + h)),
                  pl.BlockSpec((LANES, 2 * hd), lambda g, h: (off + g, v_blk0 + h)),
                  pl.BlockSpec((LANES, 2 * hd), lambda g, h: (off + g, z_blk0 + h)),
                  pl.BlockSpec((SUBLANES, LANES), lambda g, h: (h, off + g))] + w_specs + [
                  _resident((1, hd)), _resident(masks_s.shape),
                  pl.BlockSpec((seqs, SUBLANES, hd), lambda g, h: (g, 0, h)),
                  pl.BlockSpec((seqs, SUBLANES, hd), lambda g, h: (g, 0, hq + h)),
                  pl.BlockSpec((seqs, SUBLANES, 2 * hd), lambda g, h: (g, 0, v_blk0 + h)),
                  state_spec] + more_specs,
        out_specs=[pl.BlockSpec((LANES, 2 * hd), lambda g, h: (g, h)), state_spec],
        out_shape=[jax.ShapeDtypeStruct((n_sample, val_w), BF16),
                   jax.ShapeDtypeStruct(state_in.shape, F32)],
        input_output_aliases=aliases,
        compiler_params=_cparams(2),
        name="gdn_sample",
    )(p, p, p, p, gb, conv_w, conv_w, conv_w, ng, masks_s, hist8, hist8, hist8, state_in,
      *more_args)
    return og_s, s_s


def _gdn_gate_rows(w_in, a_log, dt_bias, n_main, hq):
    d = w_in.shape[0]
    hv = a_log.shape[0]
    rep = hv // hq
    pad = SUBLANES - 2 * rep
    wa = w_in[:, n_main:n_main + hv].T.reshape(hq, rep, d)
    wb = w_in[:, n_main + hv:n_main + 2 * hv].T.reshape(hq, rep, d)
    w_ab_t = jnp.concatenate([wa, wb, jnp.zeros((hq, pad, d), w_in.dtype)], axis=1)
    zeros = jnp.zeros((hq, SUBLANES - rep), F32)
    alog = jnp.concatenate([a_log.reshape(hq, rep), zeros], axis=1).reshape(hq * SUBLANES, 1)
    dtb = jnp.concatenate([dt_bias.reshape(hq, rep), zeros], axis=1).reshape(hq * SUBLANES, 1)
    return w_ab_t.reshape(hq * SUBLANES, d).astype(BF16), alog, dtb


def kernel(x_prompt, x_sample, state_gla, state_gdn, state_gdn_conv, ln_g, ln_b, ffn1_w_gu, ffn1_w_d, ffn2_w_gu, ffn2_w_d, gla_w_in, gla_w_a1, gla_w_a2, gla_b_a, gla_norm_g, gla_w_o, gdn_w_in, gdn_conv_w, gdn_a_log, gdn_dt_bias, gdn_norm_g, gdn_w_o):
    batch, t, d = x_prompt.shape
    dec_batch, dec_seq, _ = x_sample.shape
    depth = ln_g.shape[0]
    alpha = (2.0 * depth) ** 0.25
    n_prompt, n_sample = batch * t, dec_batch * dec_seq
    tm = math.gcd(TOKEN_TILE, math.gcd(n_prompt, n_sample))
    assert t % LANES == 0 and t % GLA_CHUNK == 0 and n_sample % LANES == 0 and LANES % dec_seq == 0
    assert tm % LANES == 0

    gla_heads, gla_dk, gla_dv = state_gla.shape[2:]
    gla_qk_w = gla_heads * gla_dk
    hv, hd = state_gdn.shape[2:4]
    conv_dim = gdn_conv_w.shape[2]
    val_w = hv * hd
    hq = (conv_dim - val_w) // (2 * hd)
    assert hv == 2 * hq, "the gated-delta kernel pairs two v heads with each q/k head"
    n_main = conv_dim + val_w

    x = (x_prompt.reshape(n_prompt, d), x_sample.reshape(n_sample, d))
    new_gla_p, new_gla_s, new_gdn_p, new_gdn_s = None, None, None, None
    new_conv_p, new_conv_s = [], []
    for i in range(depth):
        x = _ffn_ln(x, ffn1_w_gu[i].astype(BF16), ffn1_w_d[i].astype(BF16), ln_g[i, 0], ln_b[i, 0],
                    tm=tm, alpha=alpha, n_prompt=n_prompt)
        j = i // 2
        if i % 2 == 0:
            rank = gla_w_a1.shape[2]
            w_a1 = jnp.pad(gla_w_a1[j], ((0, 0), (0, LANES - rank))).astype(BF16)
            w_a2 = jnp.pad(gla_w_a2[j], ((0, LANES - rank), (0, 0))).astype(BF16)
            w_in = gla_w_in[j].astype(BF16)
            og_p, new_gla_p = _gla_prompt(
                x, w_in, w_a1, w_a2, gla_b_a[j], gla_norm_g[j], j, state_gla.shape[1], new_gla_p,
                n_prompt=n_prompt, batch=batch, heads=gla_heads, dk=gla_dk, dv=gla_dv)
            p_s, la_s = _gla_proj(x, w_in, w_a1, w_a2, gla_b_a[j], tm=tm,
                                  first_tile=n_prompt // tm, n_tiles=n_sample // tm)
            og_s, new_gla_s = _gla_sample(p_s, la_s, gla_norm_g[j], state_gla, j, new_gla_s,
                                          heads=gla_heads, dk=gla_dk, dv=gla_dv, dec_seq=dec_seq)
            w_o = gla_w_o[j]
        else:
            w_ab_t, alog_rows, dtb_rows = _gdn_gate_rows(gdn_w_in[j], gdn_a_log[j], gdn_dt_bias[j],
                                                         n_main, hq)
            w_main = gdn_w_in[j][:, :n_main].astype(BF16)
            og_p, new_gdn_p, tails = _gdn_prompt(
                x, w_main, w_ab_t, alog_rows, dtb_rows, gdn_conv_w[j], gdn_norm_g[j], j,
                state_gdn.shape[1], new_gdn_p, n_prompt=n_prompt, batch=batch, hq=hq, hd=hd)
            p_s, gb_s = _gdn_proj(x, w_main, w_ab_t, alog_rows, dtb_rows, tm=tm,
                                  first_tile=n_prompt // tm, n_tiles=n_sample // tm)
            hist8 = jnp.pad(state_gdn_conv[:, j],
                            ((0, 0), (SUBLANES - (CONV_TAPS - 1), 0), (0, 0)))
            og_s, new_gdn_s = _gdn_sample(p_s, gb_s, gdn_conv_w[j], gdn_norm_g[j], hist8, state_gdn, j,
                                          new_gdn_s, hq=hq, hd=hd, dec_seq=dec_seq)
            keep = np.arange(-(CONV_TAPS - 1), 0)
            rows_p = ((np.arange(batch) + 1) * (tails.shape[0] // batch))[:, None] + keep
            rows_s = ((np.arange(dec_batch) + 1) * dec_seq)[:, None] + keep
            new_conv_p.append(jnp.take(tails, rows_p.reshape(-1), axis=0)
                              .reshape(batch, CONV_TAPS - 1, conv_dim))
            new_conv_s.append(jnp.take(p_s, rows_s.reshape(-1), axis=0)[:, :conv_dim]
                              .reshape(dec_batch, CONV_TAPS - 1, conv_dim))
            w_o = gdn_w_o[j]
        x = _ffn_ln(x, ffn2_w_gu[i].astype(BF16), ffn2_w_d[i].astype(BF16), ln_g[i, 2], ln_b[i, 2],
                    tm=tm, alpha=alpha, n_prompt=n_prompt, split_out=(i == depth - 1),
                    mixer=(og_p, og_s, w_o.astype(BF16), ln_g[i, 1], ln_b[i, 1]))
    y_prompt, y_sample = x
    return (y_prompt.reshape(batch, t, d), y_sample.reshape(dec_batch, dec_seq, d),
            new_gla_p, new_gdn_p, jnp.stack(new_conv_p, 1),
            new_gla_s, new_gdn_s, jnp.stack(new_conv_s, 1))
```

```python
import functools
import math

import numpy as np

import jax
import jax.numpy as jnp
from jax import lax
from jax.experimental import pallas as pl
from jax.experimental.pallas import tpu as pltpu

F32 = jnp.float32
BF16 = jnp.bfloat16

LN_EPS = 1e-5
RMS_EPS = 1e-6
L2_EPS = 1e-6
FFN_RES = 0.5
GLA_GATE_TAU = 16.0
GLA_CHUNK = 64
CONV_TAPS = 4

LANES = 128
SUBLANES = 8
VMEM_LIMIT_BYTES = 60 * 1024 * 1024
TOKEN_TILE = 512
MATMUL_N_CHUNK = 1024
GDN_SEQ_BLOCK = 512
GDN_SAMPLE_PAIRS = 4
GLA_SEQ_BLOCK = 512


def _cparams(n_axes):
    return pltpu.CompilerParams(
        dimension_semantics=("arbitrary",) * n_axes, vmem_limit_bytes=VMEM_LIMIT_BYTES)


def _resident(shape):
    zeros = (0,) * len(shape)
    return pl.BlockSpec(shape, lambda *_: zeros, pipeline_mode=pl.Buffered(1))


def _dot(a, b):
    return jnp.dot(a.astype(BF16), b.astype(BF16), preferred_element_type=F32)


def _dot_nt(a, b):
    return lax.dot_general(a.astype(BF16), b.astype(BF16), (((1,), (1,)), ((), ())),
                           preferred_element_type=F32)


def _silu(x):
    return x * jax.nn.sigmoid(x)


def _softplus(x):
    return jnp.maximum(x, 0.0) + jnp.log1p(jnp.exp(-jnp.abs(x)))


def _layer_norm_rows(y, g, b):
    mu = jnp.mean(y, axis=-1, keepdims=True)
    d = y - mu
    var = jnp.mean(d * d, axis=-1, keepdims=True)
    return d * lax.rsqrt(var + LN_EPS) * g + b


def _rms_norm_rows(o, g):
    return o * lax.rsqrt(jnp.mean(o * o, axis=-1, keepdims=True) + RMS_EPS) * g


def _l2_norm_rows(x):
    return x * lax.rsqrt(jnp.sum(x * x, axis=-1, keepdims=True) + L2_EPS)


def _col_chunks(n, step=MATMUL_N_CHUNK):
    return [(c, min(c + step, n)) for c in range(0, n, step)]


def _ffn_ln_body(*refs, d_ff, alpha, n_in, has_mixer, n_prompt_tiles):
    x_refs, refs = refs[:n_in], refs[n_in:]
    is_prompt = pl.program_id(0) < n_prompt_tiles
    if has_mixer:
        (ogp_ref, ogs_ref, wo_ref, g1_ref, b1_ref), refs = refs[:5], refs[5:]
    wgu_ref, wd_ref, g_ref, b_ref = refs[:4]
    o_refs, h_ref = refs[4:-1], refs[-1]
    x = x_refs[0][...]
    if n_in == 2:
        x = jnp.where(is_prompt, x, x_refs[1][...])
    if has_mixer:
        og = jnp.where(is_prompt, ogp_ref[...], ogs_ref[...])
        x = _layer_norm_rows(alpha * x + jnp.dot(og, wo_ref[...], preferred_element_type=F32),
                             g1_ref[...], b1_ref[...])
    xb = x.astype(BF16)
    for c0, c1 in _col_chunks(d_ff):
        gate = jnp.dot(xb, wgu_ref[:, c0:c1], preferred_element_type=F32)
        up = jnp.dot(xb, wgu_ref[:, d_ff + c0:d_ff + c1], preferred_element_type=F32)
        h_ref[:, c0:c1] = (_silu(gate) * up).astype(BF16)
    y = jnp.dot(h_ref[...], wd_ref[...], preferred_element_type=F32)
    out = _layer_norm_rows(alpha * x + FFN_RES * y, g_ref[...], b_ref[...])
    if len(o_refs) == 1:
        o_refs[0][...] = out
    else:
        o_refs[1][...] = out

        @pl.when(is_prompt)
        def _():
            o_refs[0][...] = out


def _ffn_ln(xs, w_gu, w_d, g, b, *, tm, alpha, n_prompt, mixer=None, split_out=False):
    xs = xs if isinstance(xs, (tuple, list)) else (xs,)
    d = xs[0].shape[1]
    n = sum(x.shape[0] for x in xs)
    d_ff = w_d.shape[0]
    ntp = n_prompt // tm

    def pair(width):
        return [pl.BlockSpec((tm, width), lambda i: (jnp.minimum(i, ntp - 1), 0)),
                pl.BlockSpec((tm, width), lambda i: (jnp.maximum(i - ntp, 0), 0))]

    tile = pl.BlockSpec((tm, d), lambda i: (i, 0))
    vec = _resident((1, d))
    pair_shape = [jax.ShapeDtypeStruct((n_prompt, d), F32), jax.ShapeDtypeStruct((n - n_prompt, d), F32)]
    mixer_specs, mixer_args = [], []
    if mixer is not None:
        og_prompt, og_sample, w_o, g1, b1 = mixer
        mixer_specs = pair(w_o.shape[0]) + [_resident(w_o.shape), vec, vec]
        mixer_args = [og_prompt, og_sample, w_o, g1.reshape(1, d), b1.reshape(1, d)]
    return pl.pallas_call(
        functools.partial(_ffn_ln_body, d_ff=d_ff, alpha=alpha, n_in=len(xs),
                          has_mixer=mixer is not None, n_prompt_tiles=ntp),
        grid=(n // tm,),
        in_specs=(pair(d) if len(xs) == 2 else [tile]) + mixer_specs + [
            _resident(w_gu.shape), _resident(w_d.shape), vec, vec],
        out_specs=pair(d) if split_out else tile,
        out_shape=pair_shape if split_out else jax.ShapeDtypeStruct((n, d), F32),
        scratch_shapes=[pltpu.VMEM((tm, d_ff), BF16)],
        compiler_params=_cparams(1),
        name="ffn_ln",
    )(*xs, *mixer_args, w_gu, w_d, g.reshape(1, d), b.reshape(1, d))


def _gla_log_decay(xb, wa1, wa2, ba):
    low = jnp.dot(xb, wa1, preferred_element_type=F32)
    logits = _dot(low, wa2) + ba
    log_sig = jnp.minimum(logits, 0.0) - jnp.log1p(jnp.exp(-jnp.abs(logits)))
    return log_sig / GLA_GATE_TAU


def _gla_proj_body(x_ref, win_ref, wa1_ref, wa2_ref, ba_ref, p_ref, la_ref):
    xb = x_ref[...].astype(BF16)
    for c0, c1 in _col_chunks(win_ref.shape[1]):
        p_ref[:, c0:c1] = jnp.dot(xb, win_ref[:, c0:c1], preferred_element_type=F32)
    la_ref[...] = _gla_log_decay(xb, wa1_ref[...], wa2_ref[...], ba_ref[...])


def _gla_proj(x, w_in, w_a1, w_a2, b_a, *, tm, first_tile, n_tiles):
    d = x.shape[1]
    n = n_tiles * tm
    n_out = w_in.shape[1]
    qk_w = w_a2.shape[1]
    return pl.pallas_call(
        _gla_proj_body,
        grid=(n_tiles,),
        in_specs=[pl.BlockSpec((tm, d), lambda i: (first_tile + i, 0)), _resident(w_in.shape),
                  _resident(w_a1.shape), _resident(w_a2.shape), _resident((1, qk_w))],
        out_specs=[pl.BlockSpec((tm, n_out), lambda i: (i, 0)),
                   pl.BlockSpec((tm, qk_w), lambda i: (i, 0))],
        out_shape=[jax.ShapeDtypeStruct((n, n_out), F32), jax.ShapeDtypeStruct((n, qk_w), F32)],
        compiler_params=_cparams(1),
        name="gla_proj",
    )(x, w_in, w_a1, w_a2, b_a.reshape(1, qk_w))


def _gdn_gate_values(xb, wab, alog, dtb):
    ab = lax.dot_general(wab, xb, (((1,), (1,)), ((), ())), preferred_element_type=F32)
    kind = _iota2(ab.shape, 0) & (SUBLANES - 1)
    g = -jnp.exp(alog) * _softplus(ab + dtb)
    beta = jax.nn.sigmoid(ab)
    return jnp.where(kind < 2, g, jnp.where(kind < 4, beta, 0.0))


def _gdn_proj_body(x_ref, wmain_ref, wab_ref, alog_ref, dtb_ref, p_ref, gb_ref):
    xb = x_ref[...].astype(BF16)
    for c0, c1 in _col_chunks(wmain_ref.shape[1]):
        p_ref[:, c0:c1] = jnp.dot(xb, wmain_ref[:, c0:c1], preferred_element_type=F32)
    gb_ref[...] = _gdn_gate_values(xb, wab_ref[...], alog_ref[...], dtb_ref[...])


def _gdn_proj(x, w_main, w_ab_t, a_log_rows, dt_bias_rows, *, tm, first_tile, n_tiles):
    d = x.shape[1]
    n = n_tiles * tm
    n_out = w_main.shape[1]
    n_rows = w_ab_t.shape[0]
    return pl.pallas_call(
        _gdn_proj_body,
        grid=(n_tiles,),
        in_specs=[pl.BlockSpec((tm, d), lambda i: (first_tile + i, 0)), _resident(w_main.shape),
                  _resident(w_ab_t.shape), _resident((n_rows, 1)), _resident((n_rows, 1))],
        out_specs=[pl.BlockSpec((tm, n_out), lambda i: (i, 0)),
                   pl.BlockSpec((n_rows, tm), lambda i: (0, i))],
        out_shape=[jax.ShapeDtypeStruct((n, n_out), F32), jax.ShapeDtypeStruct((n_rows, n), F32)],
        compiler_params=_cparams(1),
        name="gdn_proj",
    )(x, w_main, w_ab_t, a_log_rows, dt_bias_rows)


def _iota2(shape, dim):
    return lax.broadcasted_iota(jnp.int32, shape, dim)


def _block_of(idx, block):
    return idx >> (block.bit_length() - 1)


def _block_cumsum(x, block, axis):
    pos = _iota2(x.shape, axis) & (block - 1)
    shift = 1
    while shift < block:
        x = x + jnp.where(pos >= shift, pltpu.roll(x, shift, axis), 0.0)
        shift *= 2
    return x


def _col_form(row):
    return jnp.transpose(jnp.broadcast_to(row, (LANES, LANES)))


def _segments(block):
    return [(r, r + block) for r in range(0, LANES, block)]


def _last_row_of_segments(x, block):
    return jnp.concatenate(
        [jnp.broadcast_to(x[r1 - 1:r1, :], (r1 - r0, x.shape[1])) for r0, r1 in _segments(block)],
        axis=0)


def _rows_only(x, r0, r1):
    rows = _iota2(x.shape, 0)
    return jnp.where((rows >= r0) & (rows < r1), x, 0.0)


def _gla_tiles(heads, ng, block, chained):
    row = _iota2((LANES, LANES), 0)
    col = _iota2((LANES, LANES), 1)
    causal = (_block_of(row, block) == _block_of(col, block)) & (row >= col)
    segs = _segments(block)
    bcums = [_block_cumsum(la, block, 0) for (_, _, _, la, _, _) in heads]
    q_decs = [q * jnp.exp(b) for (q, _, _, _, _, _), b in zip(heads, bcums)]
    atts = [jnp.where(causal, _dot_nt(qd, k * jnp.exp(-b)), 0.0)
            for qd, (_, k, _, _, _, _), b in zip(q_decs, heads, bcums)]
    intras = [_dot(att, v) for att, (_, _, v, _, _, _) in zip(atts, heads)]
    k_end_ts = [jnp.transpose(k * jnp.exp(_last_row_of_segments(b, block) - b))
                for (_, k, _, _, _, _), b in zip(heads, bcums)]
    lane_reps = heads[0][2].shape[1] // LANES
    states = [list(st) for (_, _, _, _, _, st) in heads]
    cur = [st[0] for st in states]
    outs = [[] for _ in heads]
    leaving = [[] for _ in heads]
    for s, (r0, r1) in enumerate(segs):
        if not chained:
            cur = [st[s] for st in states]
        inter = [_dot(qd[r0:r1], c) for qd, c in zip(q_decs, cur)]
        upd = [_dot(kt, _rows_only(v, r0, r1)) for kt, (_, _, v, _, _, _) in zip(k_end_ts, heads)]
        for i, b in enumerate(bcums):
            outs[i].append(intras[i][r0:r1] + inter[i])
            decay = jnp.concatenate([_col_form(jnp.exp(b[r1 - 1:r1, :]))] * lane_reps, axis=1)
            cur[i] = decay * cur[i] + upd[i]
            if not chained:
                leaving[i].append(cur[i])
    results = []
    for i, (_, _, _, _, r, _) in enumerate(heads):
        o = _rms_norm_rows(jnp.concatenate(outs[i], axis=0), ng)
        results.append((o * _silu(r), leaving[i] if not chained else [cur[i]]))
    return results


def _gla_prompt_body(x_ref, wg_ref, wa1_ref, wa2_ref, ba_ref, ng_ref, og_ref, so_ref,
                     p0_ref, p1_ref, la0_ref, la1_ref, xb_ref, s_ref, *,
                     blocks_per_seq, n_heads, dk, dv, q_scale):
    step = pl.program_id(0)
    n_groups, _, group_w = p0_ref.shape
    qk_w = n_heads * dk

    @pl.when(step == 0)
    def _():
        p1_ref[...] = jnp.zeros_like(p1_ref)
        la1_ref[...] = jnp.zeros_like(la1_ref)
        s_ref[...] = jnp.zeros_like(s_ref)

    block_in_seq = lax.rem(jnp.maximum(step - 1, 0), blocks_per_seq)
    seq_start = block_in_seq == 0

    @pl.when(seq_start)
    def _():
        s_ref[...] = jnp.zeros_like(s_ref)

    xb_ref[...] = x_ref[...].astype(BF16)

    def cols_of(c0, width):
        g, off = divmod(c0, group_w)
        assert off + width <= group_w
        return g, slice(off, off + width)

    def run(p_cur, la_cur, p_nxt, la_nxt):
        la_nxt[...] = _gla_log_decay(xb_ref[...], wa1_ref[...], wa2_ref[...], ba_ref[...])

        def tile(t, carry):
            rows = pl.ds(pl.multiple_of(t * LANES, LANES), LANES)
            p_nxt[t] = jnp.dot(xb_ref[...], wg_ref[t], preferred_element_type=F32)
            heads = []
            for h in range(n_heads):
                gq, cq = cols_of(h * dk, dk)
                gk, ck = cols_of(qk_w + h * dk, dk)
                gv, cv = cols_of(2 * qk_w + h * dv, dv)
                gr, cr = cols_of(2 * qk_w + n_heads * dv + h * dv, dv)
                heads.append((p_cur[gq, rows, cq] * q_scale, p_cur[gk, rows, ck],
                              p_cur[gv, rows, cv], la_cur[rows, h * dk:(h + 1) * dk],
                              p_cur[gr, rows, cr], [s_ref[h]]))
            for h, (o, leaving) in enumerate(_gla_tiles(heads, ng_ref[...], GLA_CHUNK, True)):
                og_ref[rows, h * dv:(h + 1) * dv] = o.astype(BF16)
                s_ref[h] = leaving[0]
            return carry

        lax.fori_loop(0, n_groups, tile, 0)

    parity = lax.rem(step, 2)

    @pl.when(parity == 0)
    def _():
        run(p1_ref, la1_ref, p0_ref, la0_ref)

    @pl.when(parity == 1)
    def _():
        run(p0_ref, la0_ref, p1_ref, la1_ref)

    @pl.when(block_in_seq == blocks_per_seq - 1)
    def _():
        so_ref[...] = s_ref[...]


def _gla_sample_body(q_ref, k_ref, v_ref, r_ref, la_ref, ng_ref, s_ref, og_ref, so_ref, *, block,
                     n_heads, dk, dv, q_scale):
    n_seq = LANES // block
    heads = []
    for h in range(n_heads):
        ck = slice(h * dk, (h + 1) * dk)
        cv = slice(h * dv, (h + 1) * dv)
        heads.append((q_ref[:, ck] * q_scale, k_ref[:, ck], v_ref[:, cv], la_ref[:, ck], r_ref[:, cv],
                      [s_ref[i, h] for i in range(n_seq)]))
    for h, (o, leaving) in enumerate(_gla_tiles(heads, ng_ref[...], block, False)):
        og_ref[:, h * dv:(h + 1) * dv] = o.astype(BF16)
        for i in range(n_seq):
            so_ref[i, h] = leaving[i]


def _carried(body, n_inputs, prev):
    def skipping_carried_input(*refs):
        return body(*refs[:n_inputs], *refs[n_inputs + 1:])

    return skipping_carried_input, [pl.BlockSpec(memory_space=pl.ANY)], [prev], {n_inputs: 1}


def _gla_prompt(x, w_in, w_a1, w_a2, b_a, norm_g, layer, n_layers, prev_states, *, n_prompt, batch,
                heads, dk, dv):
    d = x.shape[1]
    t = n_prompt // batch
    n_out = w_in.shape[1]
    qk_w, v_w = heads * dk, heads * dv
    sb = math.gcd(GLA_SEQ_BLOCK, t)
    n_groups = sb // LANES
    group_w = n_out // n_groups
    assert n_out % n_groups == 0 and group_w % dv == 0
    bps = t // sb
    n_blocks = n_prompt // sb
    w_groups = w_in.reshape(d, n_groups, group_w).transpose(1, 0, 2)

    def recurred(i):
        return jnp.maximum(i - 1, 0)

    body, more_specs, more_args, aliases = _carried(
        functools.partial(_gla_prompt_body, blocks_per_seq=bps, n_heads=heads, dk=dk, dv=dv,
                          q_scale=dk ** -0.5), 6, prev_states)
    return pl.pallas_call(
        body,
        grid=(n_blocks + 1,),
        in_specs=[pl.BlockSpec((sb, d), lambda i: (jnp.minimum(i, n_blocks - 1), 0)),
                  _resident(w_groups.shape), _resident(w_a1.shape), _resident(w_a2.shape),
                  _resident((1, qk_w)), _resident((1, dv))] + more_specs,
        out_specs=[pl.BlockSpec((sb, v_w), lambda i: (recurred(i), 0)),
                   pl.BlockSpec((None, None, heads, dk, dv),
                                lambda i: (recurred(i) // bps, layer, 0, 0, 0))],
        out_shape=[jax.ShapeDtypeStruct((n_prompt, v_w), BF16),
                   jax.ShapeDtypeStruct((batch, n_layers, heads, dk, dv), F32)],
        scratch_shapes=[pltpu.VMEM((n_groups, sb, group_w), F32), pltpu.VMEM((n_groups, sb, group_w), F32),
                        pltpu.VMEM((sb, qk_w), F32), pltpu.VMEM((sb, qk_w), F32),
                        pltpu.VMEM((sb, d), BF16), pltpu.VMEM((heads, dk, dv), F32)],
        input_output_aliases=aliases,
        compiler_params=_cparams(1),
        name="gla_prompt",
    )(x, w_groups, w_a1, w_a2, b_a.reshape(1, qk_w), norm_g.reshape(1, dv), *more_args)


def _gla_sample(p, la, norm_g, state_in, layer, prev_states, *, heads, dk, dv, dec_seq):
    n_sample = p.shape[0]
    qk_w, v_w = heads * dk, heads * dv
    ng = norm_g.reshape(1, dv)
    seqs = LANES // dec_seq
    state_spec = pl.BlockSpec((seqs, None, heads, dk, dv), lambda g: (g, layer, 0, 0, 0))
    body, more_specs, more_args, aliases = _carried(
        functools.partial(_gla_sample_body, block=dec_seq, n_heads=heads, dk=dk, dv=dv,
                          q_scale=dk ** -0.5), 7, prev_states)
    og_s, s_s = pl.pallas_call(
        body,
        grid=(n_sample // LANES,),
        in_specs=[pl.BlockSpec((LANES, qk_w), lambda g: (g, 0)),
                  pl.BlockSpec((LANES, qk_w), lambda g: (g, 1)),
                  pl.BlockSpec((LANES, v_w), lambda g: (g, (2 * qk_w) // v_w)),
                  pl.BlockSpec((LANES, v_w), lambda g: (g, (2 * qk_w + v_w) // v_w)),
                  pl.BlockSpec((LANES, qk_w), lambda g: (g, 0)),
                  _resident((1, dv)), state_spec] + more_specs,
        out_specs=[pl.BlockSpec((LANES, v_w), lambda g: (g, 0)), state_spec],
        out_shape=[jax.ShapeDtypeStruct((n_sample, v_w), BF16),
                   jax.ShapeDtypeStruct(state_in.shape, F32)],
        input_output_aliases=aliases,
        compiler_params=_cparams(1),
        name="gla_sample",
    )(p, p, p, p, la, ng, state_in, *more_args)
    return og_s, s_s


def _causal_conv(prev8, x, w):
    n = x.shape[0]
    xs = jnp.concatenate([prev8, x], axis=0)
    acc = xs[SUBLANES:, :] * w[CONV_TAPS - 1:CONV_TAPS, :]
    for j in range(CONV_TAPS - 1):
        shifted = pltpu.roll(xs, CONV_TAPS - 1 - j, 0)
        acc = acc + shifted[SUBLANES:SUBLANES + n, :] * w[j:j + 1, :]
    return acc


def _inverse_level_masks(block):
    idx = np.arange(LANES)
    masks = []
    size = 1
    while size < block:
        pair = (idx[:, None] // (2 * size)) == (idx[None, :] // (2 * size))
        masks.append(pair & ((idx[:, None] // size) != (idx[None, :] // size)))
        size *= 2
    return jnp.asarray(np.stack(masks).astype(np.float32))


def _unit_lower_inverses(mats, masks_ref):
    row = _iota2((LANES, LANES), 0)
    col = _iota2((LANES, LANES), 1)
    eye = jnp.where(row == col, 1.0, 0.0)
    xs = [eye - a * masks_ref[0] for a in mats]
    mats_b = [a.astype(BF16) for a in mats]
    for level in range(1, masks_ref.shape[0]):
        mask = masks_ref[level]
        xbs = [x.astype(BF16) for x in xs]
        ys = [jnp.dot(xb, ab, preferred_element_type=F32) for xb, ab in zip(xbs, mats_b)]
        xs = [x - jnp.dot(y.astype(BF16), xb, preferred_element_type=F32) * mask
              for x, y, xb in zip(xs, ys, xbs)]
    return xs


def _gdn_intra(pairs, block, hd, masks_ref):
    row = _iota2((LANES, LANES), 0)
    col = _iota2((LANES, LANES), 1)
    same = _block_of(row, block) == _block_of(col, block)
    incl = same & (row >= col)
    strict = same & (row > col)
    kk_qks = [_dot_nt(jnp.concatenate([k, q], axis=0), k) for (q, k, _, _, _) in pairs]
    heads = []
    for (q, k, v, z, gb), kk_qk in zip(pairs, kk_qks):
        kk, qk = kk_qk[:LANES], kk_qk[LANES:]
        gcum = _block_cumsum(gb, block, 1)
        for e in range(2):
            g_row = gcum[e:e + 1, :]
            g_col = _col_form(g_row)
            b_col = _col_form(gb[2 + e:3 + e, :])
            decay = jnp.where(incl, jnp.exp(g_col - jnp.broadcast_to(g_row, (LANES, LANES))), 0.0)
            heads.append(dict(
                q=q, k=k, v=v[:, e * hd:(e + 1) * hd], z=z[:, e * hd:(e + 1) * hd], qk=qk,
                decay=decay, g_col=g_col, b_col=b_col,
                a=jnp.where(strict, b_col * kk * decay, 0.0)))
    t_invs = _unit_lower_inverses([h["a"] for h in heads], masks_ref)
    for h, t_inv in zip(heads, t_invs):
        eg = jnp.exp(h["g_col"])
        rhs = jnp.concatenate([h["k"] * (h["b_col"] * eg), h["v"] * h["b_col"]], axis=1)
        h["wu"] = _dot(t_inv, rhs)
        h["q_dec"] = h["q"] * eg
        h["att"] = h["qk"] * h["decay"]
    for h in heads:
        k_end = h["k"] * jnp.exp(_last_row_of_segments(h["g_col"], block) - h["g_col"])
        h["k_end_t"] = jnp.transpose(k_end)
    return heads


def _gdn_inter(heads, states, ng, block, hd):
    segs = _segments(block)
    ws_qs = [[_dot(jnp.concatenate([h["wu"][r0:r1, :hd], h["q_dec"][r0:r1]], axis=0), st[s])
              for s, (r0, r1) in enumerate(segs)] for h, st in zip(heads, states)]
    v_news = []
    for h, per_seg in zip(heads, ws_qs):
        parts = [h["wu"][r0:r1, hd:] - x[:r1 - r0] for (r0, r1), x in zip(segs, per_seg)]
        v_news.append(jnp.concatenate(parts, axis=0) if len(parts) > 1 else parts[0])
    if len(segs) == 1:
        both = [_dot(jnp.concatenate([h["att"], h["k_end_t"]], axis=0), v_new)
                for h, v_new in zip(heads, v_news)]
        o_intra = [x[:LANES] for x in both]
        updates = [[x[LANES:]] for x in both]
    else:
        o_intra = [_dot(h["att"], v_new) for h, v_new in zip(heads, v_news)]
        updates = [[_dot(h["k_end_t"], _rows_only(v_new, r0, r1)) for (r0, r1) in segs]
                   for h, v_new in zip(heads, v_news)]
    outs, leaving = [], []
    for h, st, per_seg, oi, upd in zip(heads, states, ws_qs, o_intra, updates):
        qs = [x[r1 - r0:] for (r0, r1), x in zip(segs, per_seg)]
        o = oi + (jnp.concatenate(qs, axis=0) if len(qs) > 1 else qs[0])
        outs.append(_rms_norm_rows(o, ng) * _silu(h["z"]))
        leaving.append([jnp.exp(h["g_col"][r1 - 1:r1, :]) * st[s] + upd[s]
                        for s, (r0, r1) in enumerate(segs)])
    return outs, leaving


def _gdn_prompt_body(x_ref, wg_ref, wab_ref, alog_ref, dtb_ref, cw_ref, ng_ref, masks_ref,
                     og_ref, so_ref, tail_ref, p0_ref, p1_ref, gb0_ref, gb1_ref, xb_ref, s_ref, prev_ref,
                     *, blocks_per_seq, n_pairs, q_scale, hd):
    step = pl.program_id(0)
    n_groups, sb, group_w = p0_ref.shape
    n_heads = 2 * n_pairs
    key_w = n_pairs * hd

    @pl.when(step == 0)
    def _():
        p1_ref[...] = jnp.zeros_like(p1_ref)
        gb1_ref[...] = jnp.zeros_like(gb1_ref)
        s_ref[...] = jnp.zeros_like(s_ref)
        prev_ref[...] = jnp.zeros_like(prev_ref)

    block_in_seq = lax.rem(jnp.maximum(step - 1, 0), blocks_per_seq)
    seq_start = block_in_seq == 0

    @pl.when(seq_start)
    def _():
        s_ref[...] = jnp.zeros_like(s_ref)

    xb_ref[...] = x_ref[...].astype(BF16)

    def cols_of(c0, width):
        g, off = divmod(c0, group_w)
        assert off + width <= group_w
        return g, slice(off, off + width)

    def run(p_cur, gb_cur, p_nxt, gb_nxt):
        gb_nxt[...] = _gdn_gate_values(xb_ref[...], wab_ref[...], alog_ref[...], dtb_ref[...])

        def tile(t, carry):
            r0 = pl.multiple_of(t * LANES, LANES)
            rows = pl.ds(r0, LANES)
            p_nxt[t] = jnp.dot(xb_ref[...], wg_ref[t], preferred_element_type=F32)
            prev = pl.ds(pl.multiple_of(jnp.maximum(r0 - SUBLANES, 0), SUBLANES), SUBLANES)

            def conv(c0, width):
                g, cols = cols_of(c0, width)
                carried = jnp.where(seq_start, 0.0, prev_ref[:, c0:c0 + width])
                prev8 = jnp.where(t > 0, p_cur[g, prev, cols], carried)
                return _silu(_causal_conv(prev8, p_cur[g, rows, cols], cw_ref[:, c0:c0 + width]))

            pairs = []
            for p in range(n_pairs):
                q = _l2_norm_rows(conv(p * hd, hd)) * q_scale
                k = _l2_norm_rows(conv(key_w + p * hd, hd))
                v = conv(2 * key_w + 2 * p * hd, 2 * hd)
                gz, zc = cols_of(4 * key_w + 2 * p * hd, 2 * hd)
                pairs.append((q, k, v, p_cur[gz, rows, zc],
                              gb_cur[p * SUBLANES:(p + 1) * SUBLANES, rows]))
            heads = _gdn_intra(pairs, LANES, hd, masks_ref)
            outs, leaving = _gdn_inter(heads, [[s_ref[h]] for h in range(n_heads)], ng_ref[...],
                                       LANES, hd)
            for h in range(n_heads):
                og_ref[rows, h * hd:(h + 1) * hd] = outs[h].astype(BF16)
                s_ref[h] = leaving[h][0]
            return carry

        lax.fori_loop(0, n_groups, tile, 0)
        for c0 in range(0, 4 * key_w, hd):
            g, cols = cols_of(c0, hd)
            tail = p_cur[g, sb - SUBLANES:sb, cols]
            tail_ref[:, c0:c0 + hd] = tail
            prev_ref[:, c0:c0 + hd] = tail

    parity = lax.rem(step, 2)

    @pl.when(parity == 0)
    def _():
        run(p1_ref, gb1_ref, p0_ref, gb0_ref)

    @pl.when(parity == 1)
    def _():
        run(p0_ref, gb0_ref, p1_ref, gb1_ref)

    @pl.when(block_in_seq == blocks_per_seq - 1)
    def _():
        so_ref[...] = s_ref[...]


def _gdn_sample_body(q_ref, k_ref, v_ref, z_ref, gb_ref, wq_ref, wk_ref, wv_ref, ng_ref, masks_ref,
                     hq_ref, hk_ref, hv_ref, s_ref, og_ref, so_ref, *, block, n_pairs, q_scale, hd):
    segs = _segments(block)

    def conv(ref, hist_ref, w_ref, cols):
        x = ref[:, cols]
        w = w_ref[:, cols]
        return _silu(jnp.concatenate(
            [_causal_conv(hist_ref[s, :, cols], x[r0:r1], w) for s, (r0, r1) in enumerate(segs)],
            axis=0))

    pairs = []
    for p in range(n_pairs):
        c1 = slice(p * hd, (p + 1) * hd)
        c2 = slice(2 * p * hd, 2 * (p + 1) * hd)
        q = _l2_norm_rows(conv(q_ref, hq_ref, wq_ref, c1)) * q_scale
        k = _l2_norm_rows(conv(k_ref, hk_ref, wk_ref, c1))
        v = conv(v_ref, hv_ref, wv_ref, c2)
        pairs.append((q, k, v, z_ref[:, c2], gb_ref[p * SUBLANES:(p + 1) * SUBLANES, :]))
    heads = _gdn_intra(pairs, block, hd, masks_ref)
    states = [[s_ref[s, h] for s in range(len(segs))] for h in range(2 * n_pairs)]
    outs, leaving = _gdn_inter(heads, states, ng_ref[...], block, hd)
    for h in range(2 * n_pairs):
        og_ref[:, h * hd:(h + 1) * hd] = outs[h].astype(BF16)
        for s in range(len(segs)):
            so_ref[s, h] = leaving[h][s]


def _gdn_prompt(x, w_main, w_ab_t, a_log_rows, dt_bias_rows, conv_w, norm_g, layer, n_layers,
                prev_states, *, n_prompt, batch, hq, hd):
    d = x.shape[1]
    t = n_prompt // batch
    n_main = w_main.shape[1]
    conv_dim = conv_w.shape[1]
    val_w = 2 * hq * hd
    n_rows = w_ab_t.shape[0]
    sb = math.gcd(GDN_SEQ_BLOCK, t)
    n_groups = sb // LANES
    group_w = n_main // n_groups
    assert n_main % n_groups == 0 and group_w % (2 * hd) == 0
    bps = t // sb
    n_blocks = n_prompt // sb
    w_groups = w_main.reshape(d, n_groups, group_w).transpose(1, 0, 2)
    masks = _inverse_level_masks(LANES)

    def recurred(i):
        return jnp.maximum(i - 1, 0)

    body, more_specs, more_args, aliases = _carried(
        functools.partial(_gdn_prompt_body, blocks_per_seq=bps, n_pairs=hq, q_scale=hd ** -0.5, hd=hd),
        8, prev_states)
    return pl.pallas_call(
        body,
        grid=(n_blocks + 1,),
        in_specs=[pl.BlockSpec((sb, d), lambda i: (jnp.minimum(i, n_blocks - 1), 0)),
                  _resident(w_groups.shape), _resident(w_ab_t.shape), _resident((n_rows, 1)),
                  _resident((n_rows, 1)), _resident(conv_w.shape), _resident((1, hd)),
                  _resident(masks.shape)] + more_specs,
        out_specs=[pl.BlockSpec((sb, val_w), lambda i: (recurred(i), 0)),
                   pl.BlockSpec((None, None, 2 * hq, hd, hd),
                                lambda i: (recurred(i) // bps, layer, 0, 0, 0)),
                   pl.BlockSpec((SUBLANES, conv_dim), lambda i: (recurred(i), 0))],
        out_shape=[jax.ShapeDtypeStruct((n_prompt, val_w), BF16),
                   jax.ShapeDtypeStruct((batch, n_layers, 2 * hq, hd, hd), F32),
                   jax.ShapeDtypeStruct((n_blocks * SUBLANES, conv_dim), F32)],
        input_output_aliases=aliases,
        scratch_shapes=[pltpu.VMEM((n_groups, sb, group_w), F32), pltpu.VMEM((n_groups, sb, group_w), F32),
                        pltpu.VMEM((n_rows, sb), F32), pltpu.VMEM((n_rows, sb), F32),
                        pltpu.VMEM((sb, d), BF16), pltpu.VMEM((2 * hq, hd, hd), F32),
                        pltpu.VMEM((SUBLANES, conv_dim), F32)],
        compiler_params=_cparams(1),
        name="gdn_prompt",
    )(x, w_groups, w_ab_t, a_log_rows, dt_bias_rows, conv_w, norm_g.reshape(1, hd), masks,
      *more_args)


def _gdn_sample(p, gb, conv_w, norm_g, hist8, state_in, layer, prev_states, *, hq, hd, dec_seq):
    n_sample = p.shape[0]
    key_w = hq * hd
    val_w = 2 * key_w
    q_scale = hd ** -0.5
    gp = math.gcd(GDN_SAMPLE_PAIRS, hq)
    qw, vw = gp * hd, 2 * gp * hd
    k_blk0 = key_w // qw
    ng = norm_g.reshape(1, hd)
    seqs = LANES // dec_seq
    masks_s = _inverse_level_masks(dec_seq)
    state_spec = pl.BlockSpec((seqs, None, 2 * gp, hd, hd), lambda g, h: (g, layer, h, 0, 0))
    body, more_specs, more_args, aliases = _carried(
        functools.partial(_gdn_sample_body, block=dec_seq, n_pairs=gp, q_scale=q_scale, hd=hd), 14,
        prev_states)
    og_s, s_s = pl.pallas_call(
        body,
        grid=(n_sample // LANES, hq // gp),
        in_specs=[pl.BlockSpec((LANES, qw), lambda g, h: (g, h)),
                  pl.BlockSpec((LANES, qw), lambda g, h: (g, k_blk0 + h)),
                  pl.BlockSpec((LANES, vw), lambda g, h: (g, k_blk0 + h)),
                  pl.BlockSpec((LANES, vw), lambda g, h: (g, 2 * k_blk0 + h)),
                  pl.BlockSpec((gp * SUBLANES, LANES), lambda g, h: (h, g)),
                  pl.BlockSpec((CONV_TAPS, qw), lambda g, h: (0, h)),
                  pl.BlockSpec((CONV_TAPS, qw), lambda g, h: (0, k_blk0 + h)),
                  pl.BlockSpec((CONV_TAPS, vw), lambda g, h: (0, k_blk0 + h)),
                  _resident((1, hd)), _resident(masks_s.shape),
                  pl.BlockSpec((seqs, SUBLANES, qw), lambda g, h: (g, 0, h)),
                  pl.BlockSpec((seqs, SUBLANES, qw), lambda g, h: (g, 0, k_blk0 + h)),
                  pl.BlockSpec((seqs, SUBLANES, vw), lambda g, h: (g, 0, k_blk0 + h)),
                  state_spec] + more_specs,
        out_specs=[pl.BlockSpec((LANES, vw), lambda g, h: (g, h)), state_spec],
        out_shape=[jax.ShapeDtypeStruct((n_sample, val_w), BF16),
                   jax.ShapeDtypeStruct(state_in.shape, F32)],
        input_output_aliases=aliases,
        compiler_params=_cparams(2),
        name="gdn_sample",
    )(p, p, p, p, gb, conv_w, conv_w, conv_w, ng, masks_s, hist8, hist8, hist8, state_in,
      *more_args)
    return og_s, s_s


def _gdn_gate_rows(w_in, a_log, dt_bias, n_main, hq):
    d = w_in.shape[0]
    hv = a_log.shape[0]
    rep = hv // hq
    pad = SUBLANES - 2 * rep
    wa = w_in[:, n_main:n_main + hv].T.reshape(hq, rep, d)
    wb = w_in[:, n_main + hv:n_main + 2 * hv].T.reshape(hq, rep, d)
    w_ab_t = jnp.concatenate([wa, wb, jnp.zeros((hq, pad, d), w_in.dtype)], axis=1)
    zeros = jnp.zeros((hq, SUBLANES - rep), F32)
    alog = jnp.concatenate([a_log.reshape(hq, rep), zeros], axis=1).reshape(hq * SUBLANES, 1)
    dtb = jnp.concatenate([dt_bias.reshape(hq, rep), zeros], axis=1).reshape(hq * SUBLANES, 1)
    return w_ab_t.reshape(hq * SUBLANES, d).astype(BF16), alog, dtb


def kernel(x_prompt, x_sample, state_gla, state_gdn, state_gdn_conv, ln_g, ln_b, ffn1_w_gu, ffn1_w_d, ffn2_w_gu, ffn2_w_d, gla_w_in, gla_w_a1, gla_w_a2, gla_b_a, gla_norm_g, gla_w_o, gdn_w_in, gdn_conv_w, gdn_a_log, gdn_dt_bias, gdn_norm_g, gdn_w_o):
    batch, t, d = x_prompt.shape
    dec_batch, dec_seq, _ = x_sample.shape
    depth = ln_g.shape[0]
    alpha = (2.0 * depth) ** 0.25
    n_prompt, n_sample = batch * t, dec_batch * dec_seq
    tm = math.gcd(TOKEN_TILE, math.gcd(n_prompt, n_sample))
    assert t % LANES == 0 and t % GLA_CHUNK == 0 and n_sample % LANES == 0 and LANES % dec_seq == 0
    assert tm % LANES == 0

    gla_heads, gla_dk, gla_dv = state_gla.shape[2:]
    hv, hd = state_gdn.shape[2:4]
    conv_dim = gdn_conv_w.shape[2]
    val_w = hv * hd
    hq = (conv_dim - val_w) // (2 * hd)
    assert hv == 2 * hq, "the gated-delta kernel pairs two v heads with each q/k head"
    n_main = conv_dim + val_w

    x = (x_prompt.reshape(n_prompt, d), x_sample.reshape(n_sample, d))
    new_gla_p = jnp.zeros((batch,) + state_gla.shape[1:], F32)
    new_gla_s = jnp.zeros(state_gla.shape, F32)
    new_gdn_p = jnp.zeros((batch,) + state_gdn.shape[1:], F32)
    new_gdn_s = jnp.zeros(state_gdn.shape, F32)
    new_conv_p, new_conv_s = [], []
    for i in range(depth):
        x = _ffn_ln(x, ffn1_w_gu[i].astype(BF16), ffn1_w_d[i].astype(BF16), ln_g[i, 0], ln_b[i, 0],
                    tm=tm, alpha=alpha, n_prompt=n_prompt)
        j = i // 2
        if i % 2 == 0:
            rank = gla_w_a1.shape[2]
            w_a1 = jnp.pad(gla_w_a1[j], ((0, 0), (0, LANES - rank))).astype(BF16)
            w_a2 = jnp.pad(gla_w_a2[j], ((0, LANES - rank), (0, 0))).astype(BF16)
            w_in = gla_w_in[j].astype(BF16)
            og_p, new_gla_p = _gla_prompt(
                x, w_in, w_a1, w_a2, gla_b_a[j], gla_norm_g[j], j, state_gla.shape[1], new_gla_p,
                n_prompt=n_prompt, batch=batch, heads=gla_heads, dk=gla_dk, dv=gla_dv)
            p_s, la_s = _gla_proj(x, w_in, w_a1, w_a2, gla_b_a[j], tm=tm,
                                  first_tile=n_prompt // tm, n_tiles=n_sample // tm)
            og_s, new_gla_s = _gla_sample(p_s, la_s, gla_norm_g[j], state_gla, j, new_gla_s,
                                          heads=gla_heads, dk=gla_dk, dv=gla_dv, dec_seq=dec_seq)
            w_o = gla_w_o[j]
        else:
            w_ab_t, alog_rows, dtb_rows = _gdn_gate_rows(gdn_w_in[j], gdn_a_log[j], gdn_dt_bias[j],
                                                         n_main, hq)
            w_main = gdn_w_in[j][:, :n_main].astype(BF16)
            og_p, new_gdn_p, tails = _gdn_prompt(
                x, w_main, w_ab_t, alog_rows, dtb_rows, gdn_conv_w[j], gdn_norm_g[j], j,
                state_gdn.shape[1], new_gdn_p, n_prompt=n_prompt, batch=batch, hq=hq, hd=hd)
            p_s, gb_s = _gdn_proj(x, w_main, w_ab_t, alog_rows, dtb_rows, tm=tm,
                                  first_tile=n_prompt // tm, n_tiles=n_sample // tm)
            hist8 = jnp.pad(state_gdn_conv[:, j],
                            ((0, 0), (SUBLANES - (CONV_TAPS - 1), 0), (0, 0)))
            og_s, new_gdn_s = _gdn_sample(p_s, gb_s, gdn_conv_w[j], gdn_norm_g[j], hist8, state_gdn, j,
                                          new_gdn_s, hq=hq, hd=hd, dec_seq=dec_seq)
            keep = np.arange(-(CONV_TAPS - 1), 0)
            rows_p = ((np.arange(batch) + 1) * (tails.shape[0] // batch))[:, None] + keep
            rows_s = ((np.arange(dec_batch) + 1) * dec_seq)[:, None] + keep
            new_conv_p.append(jnp.take(tails, rows_p.reshape(-1), axis=0)
                              .reshape(batch, CONV_TAPS - 1, conv_dim))
            new_conv_s.append(jnp.take(p_s, rows_s.reshape(-1), axis=0)[:, :conv_dim]
                              .reshape(dec_batch, CONV_TAPS - 1, conv_dim))
            w_o = gdn_w_o[j]
        x = _ffn_ln(x, ffn2_w_gu[i].astype(BF16), ffn2_w_d[i].astype(BF16), ln_g[i, 2], ln_b[i, 2],
                    tm=tm, alpha=alpha, n_prompt=n_prompt, split_out=(i == depth - 1),
                    mixer=(og_p, og_s, w_o.astype(BF16), ln_g[i, 1], ln_b[i, 1]))
    y_prompt, y_sample = x
    return (y_prompt.reshape(batch, t, d), y_sample.reshape(dec_batch, dec_seq, d),
            new_gla_p, new_gdn_p, jnp.stack(new_conv_p, 1),
            new_gla_s, new_gdn_s, jnp.stack(new_conv_s, 1))
```

```python
import functools
import math

import numpy as np

import jax
import jax.numpy as jnp
from jax import lax
from jax.experimental import pallas as pl
from jax.experimental.pallas import tpu as pltpu

F32 = jnp.float32
BF16 = jnp.bfloat16

LN_EPS = 1e-5
RMS_EPS = 1e-6
L2_EPS = 1e-6
FFN_RES = 0.5
GLA_GATE_TAU = 16.0
GLA_CHUNK = 64
CONV_TAPS = 4

LANES = 128
SUBLANES = 8
VMEM_LIMIT_BYTES = 60 * 1024 * 1024
TOKEN_TILE = 512
MATMUL_N_CHUNK = 1024
GDN_SEQ_BLOCK = 512
GDN_SAMPLE_PAIRS = 4
GLA_SEQ_BLOCK = 512


def _cparams(n_axes):
    return pltpu.CompilerParams(
        dimension_semantics=("arbitrary",) * n_axes, vmem_limit_bytes=VMEM_LIMIT_BYTES)


def _resident(shape):
    zeros = (0,) * len(shape)
    return pl.BlockSpec(shape, lambda *_: zeros, pipeline_mode=pl.Buffered(1))


def _dot(a, b):
    return jnp.dot(a.astype(BF16), b.astype(BF16), preferred_element_type=F32)


def _dot_nt(a, b):
    return lax.dot_general(a.astype(BF16), b.astype(BF16), (((1,), (1,)), ((), ())),
                           preferred_element_type=F32)


def _silu(x):
    return x * jax.nn.sigmoid(x)


def _softplus(x):
    return jnp.maximum(x, 0.0) + jnp.log1p(jnp.exp(-jnp.abs(x)))


def _layer_norm_rows(y, g, b):
    mu = jnp.mean(y, axis=-1, keepdims=True)
    d = y - mu
    var = jnp.mean(d * d, axis=-1, keepdims=True)
    return d * lax.rsqrt(var + LN_EPS) * g + b


def _rms_norm_rows(o, g):
    return o * lax.rsqrt(jnp.mean(o * o, axis=-1, keepdims=True) + RMS_EPS) * g


def _l2_norm_rows(x):
    return x * lax.rsqrt(jnp.sum(x * x, axis=-1, keepdims=True) + L2_EPS)


def _col_chunks(n, step=MATMUL_N_CHUNK):
    return [(c, min(c + step, n)) for c in range(0, n, step)]


def _ffn_ln_body(*refs, d_ff, alpha, n_in, has_mixer, n_prompt_tiles):
    x_refs, refs = refs[:n_in], refs[n_in:]
    is_prompt = pl.program_id(0) < n_prompt_tiles
    if has_mixer:
        (ogp_ref, ogs_ref, wo_ref, g1_ref, b1_ref), refs = refs[:5], refs[5:]
    wgu_ref, wd_ref, g_ref, b_ref = refs[:4]
    o_refs, h_ref = refs[4:-1], refs[-1]
    x = x_refs[0][...]
    if n_in == 2:
        x = jnp.where(is_prompt, x, x_refs[1][...])
    if has_mixer:
        og = jnp.where(is_prompt, ogp_ref[...], ogs_ref[...])
        x = _layer_norm_rows(alpha * x + jnp.dot(og, wo_ref[...], preferred_element_type=F32),
                             g1_ref[...], b1_ref[...])
    xb = x.astype(BF16)
    for c0, c1 in _col_chunks(d_ff):
        gate = jnp.dot(xb, wgu_ref[:, c0:c1], preferred_element_type=F32)
        up = jnp.dot(xb, wgu_ref[:, d_ff + c0:d_ff + c1], preferred_element_type=F32)
        h_ref[:, c0:c1] = (_silu(gate) * up).astype(BF16)
    y = jnp.dot(h_ref[...], wd_ref[...], preferred_element_type=F32)
    out = _layer_norm_rows(alpha * x + FFN_RES * y, g_ref[...], b_ref[...])
    if len(o_refs) == 1:
        o_refs[0][...] = out
    else:
        o_refs[1][...] = out

        @pl.when(is_prompt)
        def _():
            o_refs[0][...] = out


def _ffn_ln(xs, w_gu, w_d, g, b, *, tm, alpha, n_prompt, mixer=None, split_out=False):
    xs = xs if isinstance(xs, (tuple, list)) else (xs,)
    d = xs[0].shape[1]
    n = sum(x.shape[0] for x in xs)
    d_ff = w_d.shape[0]
    ntp = n_prompt // tm

    def pair(width):
        return [pl.BlockSpec((tm, width), lambda i: (jnp.minimum(i, ntp - 1), 0)),
                pl.BlockSpec((tm, width), lambda i: (jnp.maximum(i - ntp, 0), 0))]

    tile = pl.BlockSpec((tm, d), lambda i: (i, 0))
    vec = _resident((1, d))
    pair_shape = [jax.ShapeDtypeStruct((n_prompt, d), F32), jax.ShapeDtypeStruct((n - n_prompt, d), F32)]
    mixer_specs, mixer_args = [], []
    if mixer is not None:
        og_prompt, og_sample, w_o, g1, b1 = mixer
        mixer_specs = pair(w_o.shape[0]) + [_resident(w_o.shape), vec, vec]
        mixer_args = [og_prompt, og_sample, w_o, g1.reshape(1, d), b1.reshape(1, d)]
    return pl.pallas_call(
        functools.partial(_ffn_ln_body, d_ff=d_ff, alpha=alpha, n_in=len(xs),
                          has_mixer=mixer is not None, n_prompt_tiles=ntp),
        grid=(n // tm,),
        in_specs=(pair(d) if len(xs) == 2 else [tile]) + mixer_specs + [
            _resident(w_gu.shape), _resident(w_d.shape), vec, vec],
        out_specs=pair(d) if split_out else tile,
        out_shape=pair_shape if split_out else jax.ShapeDtypeStruct((n, d), F32),
        scratch_shapes=[pltpu.VMEM((tm, d_ff), BF16)],
        compiler_params=_cparams(1),
        name="ffn_ln",
    )(*xs, *mixer_args, w_gu, w_d, g.reshape(1, d), b.reshape(1, d))


def _gla_log_decay(xb, wa1, wa2, ba):
    low = jnp.dot(xb, wa1, preferred_element_type=F32)
    logits = _dot(low, wa2) + ba
    log_sig = jnp.minimum(logits, 0.0) - jnp.log1p(jnp.exp(-jnp.abs(logits)))
    return log_sig / GLA_GATE_TAU


def _gla_proj_body(x_ref, win_ref, wa1_ref, wa2_ref, ba_ref, p_ref, la_ref):
    xb = x_ref[...].astype(BF16)
    for c0, c1 in _col_chunks(win_ref.shape[1]):
        p_ref[:, c0:c1] = jnp.dot(xb, win_ref[:, c0:c1], preferred_element_type=F32)
    la_ref[...] = _gla_log_decay(xb, wa1_ref[...], wa2_ref[...], ba_ref[...])


def _gla_proj(x, w_in, w_a1, w_a2, b_a, *, tm, first_tile, n_tiles):
    d = x.shape[1]
    n = n_tiles * tm
    n_out = w_in.shape[1]
    qk_w = w_a2.shape[1]
    return pl.pallas_call(
        _gla_proj_body,
        grid=(n_tiles,),
        in_specs=[pl.BlockSpec((tm, d), lambda i: (first_tile + i, 0)), _resident(w_in.shape),
                  _resident(w_a1.shape), _resident(w_a2.shape), _resident((1, qk_w))],
        out_specs=[pl.BlockSpec((tm, n_out), lambda i: (i, 0)),
                   pl.BlockSpec((tm, qk_w), lambda i: (i, 0))],
        out_shape=[jax.ShapeDtypeStruct((n, n_out), F32), jax.ShapeDtypeStruct((n, qk_w), F32)],
        compiler_params=_cparams(1),
        name="gla_proj",
    )(x, w_in, w_a1, w_a2, b_a.reshape(1, qk_w))


def _gdn_gate_values(xb, wab, alog, dtb):
    ab = lax.dot_general(wab, xb, (((1,), (1,)), ((), ())), preferred_element_type=F32)
    kind = _iota2(ab.shape, 0) & (SUBLANES - 1)
    g = -jnp.exp(alog) * _softplus(ab + dtb)
    beta = jax.nn.sigmoid(ab)
    return jnp.where(kind < 2, g, jnp.where(kind < 4, beta, 0.0))


def _gdn_proj_body(x_ref, wmain_ref, wab_ref, alog_ref, dtb_ref, p_ref, gb_ref):
    xb = x_ref[...].astype(BF16)
    for c0, c1 in _col_chunks(wmain_ref.shape[1]):
        p_ref[:, c0:c1] = jnp.dot(xb, wmain_ref[:, c0:c1], preferred_element_type=F32)
    gb_ref[...] = _gdn_gate_values(xb, wab_ref[...], alog_ref[...], dtb_ref[...])


def _gdn_proj(x, w_main, w_ab_t, a_log_rows, dt_bias_rows, *, tm, first_tile, n_tiles):
    d = x.shape[1]
    n = n_tiles * tm
    n_out = w_main.shape[1]
    n_rows = w_ab_t.shape[0]
    return pl.pallas_call(
        _gdn_proj_body,
        grid=(n_tiles,),
        in_specs=[pl.BlockSpec((tm, d), lambda i: (first_tile + i, 0)), _resident(w_main.shape),
                  _resident(w_ab_t.shape), _resident((n_rows, 1)), _resident((n_rows, 1))],
        out_specs=[pl.BlockSpec((tm, n_out), lambda i: (i, 0)),
                   pl.BlockSpec((n_rows, tm), lambda i: (0, i))],
        out_shape=[jax.ShapeDtypeStruct((n, n_out), F32), jax.ShapeDtypeStruct((n_rows, n), F32)],
        compiler_params=_cparams(1),
        name="gdn_proj",
    )(x, w_main, w_ab_t, a_log_rows, dt_bias_rows)


def _iota2(shape, dim):
    return lax.broadcasted_iota(jnp.int32, shape, dim)


def _block_of(idx, block):
    return idx >> (block.bit_length() - 1)


def _block_cumsum(x, block, axis):
    pos = _iota2(x.shape, axis) & (block - 1)
    shift = 1
    while shift < block:
        x = x + jnp.where(pos >= shift, pltpu.roll(x, shift, axis), 0.0)
        shift *= 2
    return x


def _col_form(row):
    return jnp.transpose(jnp.broadcast_to(row, (LANES, LANES)))


def _segments(block):
    return [(r, r + block) for r in range(0, LANES, block)]


def _last_row_of_segments(x, block):
    return jnp.concatenate(
        [jnp.broadcast_to(x[r1 - 1:r1, :], (r1 - r0, x.shape[1])) for r0, r1 in _segments(block)],
        axis=0)


def _rows_only(x, r0, r1):
    rows = _iota2(x.shape, 0)
    return jnp.where((rows >= r0) & (rows < r1), x, 0.0)


def _gla_tiles(heads, ng, block, chained):
    row = _iota2((LANES, LANES), 0)
    col = _iota2((LANES, LANES), 1)
    causal = (_block_of(row, block) == _block_of(col, block)) & (row >= col)
    segs = _segments(block)
    bcums = [_block_cumsum(la, block, 0) for (_, _, _, la, _, _) in heads]
    q_decs = [q * jnp.exp(b) for (q, _, _, _, _, _), b in zip(heads, bcums)]
    atts = [jnp.where(causal, _dot_nt(qd, k * jnp.exp(-b)), 0.0)
            for qd, (_, k, _, _, _, _), b in zip(q_decs, heads, bcums)]
    intras = [_dot(att, v) for att, (_, _, v, _, _, _) in zip(atts, heads)]
    k_end_ts = [jnp.transpose(k * jnp.exp(_last_row_of_segments(b, block) - b))
                for (_, k, _, _, _, _), b in zip(heads, bcums)]
    lane_reps = heads[0][2].shape[1] // LANES
    states = [list(st) for (_, _, _, _, _, st) in heads]
    cur = [st[0] for st in states]
    outs = [[] for _ in heads]
    leaving = [[] for _ in heads]
    for s, (r0, r1) in enumerate(segs):
        if not chained:
            cur = [st[s] for st in states]
        inter = [_dot(qd[r0:r1], c) for qd, c in zip(q_decs, cur)]
        upd = [_dot(kt, _rows_only(v, r0, r1)) for kt, (_, _, v, _, _, _) in zip(k_end_ts, heads)]
        for i, b in enumerate(bcums):
            outs[i].append(intras[i][r0:r1] + inter[i])
            decay = jnp.concatenate([_col_form(jnp.exp(b[r1 - 1:r1, :]))] * lane_reps, axis=1)
            cur[i] = decay * cur[i] + upd[i]
            if not chained:
                leaving[i].append(cur[i])
    results = []
    for i, (_, _, _, _, r, _) in enumerate(heads):
        o = _rms_norm_rows(jnp.concatenate(outs[i], axis=0), ng)
        results.append((o * _silu(r), leaving[i] if not chained else [cur[i]]))
    return results


def _gla_prompt_body(x_ref, wg_ref, wa1_ref, wa2_ref, ba_ref, ng_ref, og_ref, so_ref,
                     p0_ref, p1_ref, la0_ref, la1_ref, xb_ref, s_ref, *,
                     blocks_per_seq, n_heads, dk, dv, q_scale):
    step = pl.program_id(0)
    n_groups, _, group_w = p0_ref.shape
    qk_w = n_heads * dk

    @pl.when(step == 0)
    def _():
        p1_ref[...] = jnp.zeros_like(p1_ref)
        la1_ref[...] = jnp.zeros_like(la1_ref)
        s_ref[...] = jnp.zeros_like(s_ref)

    block_in_seq = lax.rem(jnp.maximum(step - 1, 0), blocks_per_seq)
    seq_start = block_in_seq == 0

    @pl.when(seq_start)
    def _():
        s_ref[...] = jnp.zeros_like(s_ref)

    xb_ref[...] = x_ref[...].astype(BF16)

    def cols_of(c0, width):
        g, off = divmod(c0, group_w)
        assert off + width <= group_w
        return g, slice(off, off + width)

    def run(p_cur, la_cur, p_nxt, la_nxt):
        la_nxt[...] = _gla_log_decay(xb_ref[...], wa1_ref[...], wa2_ref[...], ba_ref[...])

        def tile(t, carry):
            rows = pl.ds(pl.multiple_of(t * LANES, LANES), LANES)
            p_nxt[t] = jnp.dot(xb_ref[...], wg_ref[t], preferred_element_type=F32)
            heads = []
            for h in range(n_heads):
                gq, cq = cols_of(h * dk, dk)
                gk, ck = cols_of(qk_w + h * dk, dk)
                gv, cv = cols_of(2 * qk_w + h * dv, dv)
                gr, cr = cols_of(2 * qk_w + n_heads * dv + h * dv, dv)
                heads.append((p_cur[gq, rows, cq] * q_scale, p_cur[gk, rows, ck],
                              p_cur[gv, rows, cv], la_cur[rows, h * dk:(h + 1) * dk],
                              p_cur[gr, rows, cr], [s_ref[h]]))
            for h, (o, leaving) in enumerate(_gla_tiles(heads, ng_ref[...], GLA_CHUNK, True)):
                og_ref[rows, h * dv:(h + 1) * dv] = o.astype(BF16)
                s_ref[h] = leaving[0]
            return carry

        lax.fori_loop(0, n_groups, tile, 0)

    parity = lax.rem(step, 2)

    @pl.when(parity == 0)
    def _():
        run(p1_ref, la1_ref, p0_ref, la0_ref)

    @pl.when(parity == 1)
    def _():
        run(p0_ref, la0_ref, p1_ref, la1_ref)

    @pl.when(block_in_seq == blocks_per_seq - 1)
    def _():
        so_ref[...] = s_ref[...]


def _gla_sample_body(q_ref, k_ref, v_ref, r_ref, la_ref, ng_ref, s_ref, og_ref, so_ref, *, block,
                     n_heads, dk, dv, q_scale):
    n_seq = LANES // block
    heads = []
    for h in range(n_heads):
        ck = slice(h * dk, (h + 1) * dk)
        cv = slice(h * dv, (h + 1) * dv)
        heads.append((q_ref[:, ck] * q_scale, k_ref[:, ck], v_ref[:, cv], la_ref[:, ck], r_ref[:, cv],
                      [s_ref[i, h] for i in range(n_seq)]))
    for h, (o, leaving) in enumerate(_gla_tiles(heads, ng_ref[...], block, False)):
        og_ref[:, h * dv:(h + 1) * dv] = o.astype(BF16)
        for i in range(n_seq):
            so_ref[i, h] = leaving[i]


def _carried(body, n_inputs, prev):
    def skipping_carried_input(*refs):
        return body(*refs[:n_inputs], *refs[n_inputs + 1:])

    return skipping_carried_input, [pl.BlockSpec(memory_space=pl.ANY)], [prev], {n_inputs: 1}


def _gla_prompt(x, w_in, w_a1, w_a2, b_a, norm_g, layer, n_layers, prev_states, *, n_prompt, batch,
                heads, dk, dv):
    d = x.shape[1]
    t = n_prompt // batch
    n_out = w_in.shape[1]
    qk_w, v_w = heads * dk, heads * dv
    sb = math.gcd(GLA_SEQ_BLOCK, t)
    n_groups = sb // LANES
    group_w = n_out // n_groups
    assert n_out % n_groups == 0 and group_w % dv == 0
    bps = t // sb
    n_blocks = n_prompt // sb
    w_groups = w_in.reshape(d, n_groups, group_w).transpose(1, 0, 2)

    def recurred(i):
        return jnp.maximum(i - 1, 0)

    body, more_specs, more_args, aliases = _carried(
        functools.partial(_gla_prompt_body, blocks_per_seq=bps, n_heads=heads, dk=dk, dv=dv,
                          q_scale=dk ** -0.5), 6, prev_states)
    return pl.pallas_call(
        body,
        grid=(n_blocks + 1,),
        in_specs=[pl.BlockSpec((sb, d), lambda i: (jnp.minimum(i, n_blocks - 1), 0)),
                  _resident(w_groups.shape), _resident(w_a1.shape), _resident(w_a2.shape),
                  _resident((1, qk_w)), _resident((1, dv))] + more_specs,
        out_specs=[pl.BlockSpec((sb, v_w), lambda i: (recurred(i), 0)),
                   pl.BlockSpec((None, None, heads, dk, dv),
                                lambda i: (recurred(i) // bps, layer, 0, 0, 0))],
        out_shape=[jax.ShapeDtypeStruct((n_prompt, v_w), BF16),
                   jax.ShapeDtypeStruct((batch, n_layers, heads, dk, dv), F32)],
        scratch_shapes=[pltpu.VMEM((n_groups, sb, group_w), F32), pltpu.VMEM((n_groups, sb, group_w), F32),
                        pltpu.VMEM((sb, qk_w), F32), pltpu.VMEM((sb, qk_w), F32),
                        pltpu.VMEM((sb, d), BF16), pltpu.VMEM((heads, dk, dv), F32)],
        input_output_aliases=aliases,
        compiler_params=_cparams(1),
        name="gla_prompt",
    )(x, w_groups, w_a1, w_a2, b_a.reshape(1, qk_w), norm_g.reshape(1, dv), *more_args)


def _gla_sample(p, la, norm_g, state_in, layer, prev_states, *, heads, dk, dv, dec_seq):
    n_sample = p.shape[0]
    qk_w, v_w = heads * dk, heads * dv
    ng = norm_g.reshape(1, dv)
    seqs = LANES // dec_seq
    state_spec = pl.BlockSpec((seqs, None, heads, dk, dv), lambda g: (g, layer, 0, 0, 0))
    body, more_specs, more_args, aliases = _carried(
        functools.partial(_gla_sample_body, block=dec_seq, n_heads=heads, dk=dk, dv=dv,
                          q_scale=dk ** -0.5), 7, prev_states)
    og_s, s_s = pl.pallas_call(
        body,
        grid=(n_sample // LANES,),
        in_specs=[pl.BlockSpec((LANES, qk_w), lambda g: (g, 0)),
                  pl.BlockSpec((LANES, qk_w), lambda g: (g, 1)),
                  pl.BlockSpec((LANES, v_w), lambda g: (g, (2 * qk_w) // v_w)),
                  pl.BlockSpec((LANES, v_w), lambda g: (g, (2 * qk_w + v_w) // v_w)),
                  pl.BlockSpec((LANES, qk_w), lambda g: (g, 0)),
                  _resident((1, dv)), state_spec] + more_specs,
        out_specs=[pl.BlockSpec((LANES, v_w), lambda g: (g, 0)), state_spec],
        out_shape=[jax.ShapeDtypeStruct((n_sample, v_w), BF16),
                   jax.ShapeDtypeStruct(state_in.shape, F32)],
        input_output_aliases=aliases,
        compiler_params=_cparams(1),
        name="gla_sample",
    )(p, p, p, p, la, ng, state_in, *more_args)
    return og_s, s_s


def _causal_conv(prev8, x, w):
    n = x.shape[0]
    xs = jnp.concatenate([prev8, x], axis=0)
    acc = xs[SUBLANES:, :] * w[CONV_TAPS - 1:CONV_TAPS, :]
    for j in range(CONV_TAPS - 1):
        shifted = pltpu.roll(xs, CONV_TAPS - 1 - j, 0)
        acc = acc + shifted[SUBLANES:SUBLANES + n, :] * w[j:j + 1, :]
    return acc


def _inverse_level_masks(block):
    idx = np.arange(LANES)
    masks = []
    size = 1
    while size < block:
        pair = (idx[:, None] // (2 * size)) == (idx[None, :] // (2 * size))
        masks.append(pair & ((idx[:, None] // size) != (idx[None, :] // size)))
        size *= 2
    return jnp.asarray(np.stack(masks).astype(np.float32))


def _unit_lower_inverses(mats, masks_ref):
    row = _iota2((LANES, LANES), 0)
    col = _iota2((LANES, LANES), 1)
    eye = jnp.where(row == col, 1.0, 0.0)
    xs = [eye - a * masks_ref[0] for a in mats]
    mats_b = [a.astype(BF16) for a in mats]
    for level in range(1, masks_ref.shape[0]):
        mask = masks_ref[level]
        xbs = [x.astype(BF16) for x in xs]
        ys = [jnp.dot(xb, ab, preferred_element_type=F32) for xb, ab in zip(xbs, mats_b)]
        xs = [x - jnp.dot(y.astype(BF16), xb, preferred_element_type=F32) * mask
              for x, y, xb in zip(xs, ys, xbs)]
    return xs


def _gdn_intra(pairs, block, hd, masks_ref):
    row = _iota2((LANES, LANES), 0)
    col = _iota2((LANES, LANES), 1)
    same = _block_of(row, block) == _block_of(col, block)
    incl = same & (row >= col)
    strict = same & (row > col)
    kk_qks = [_dot_nt(jnp.concatenate([k, q], axis=0), k) for (q, k, _, _, _) in pairs]
    heads = []
    for (q, k, v, z, gb), kk_qk in zip(pairs, kk_qks):
        kk, qk = kk_qk[:LANES], kk_qk[LANES:]
        gcum = _block_cumsum(gb, block, 1)
        for e in range(2):
            g_row = gcum[e:e + 1, :]
            g_col = _col_form(g_row)
            b_col = _col_form(gb[2 + e:3 + e, :])
            decay = jnp.where(incl, jnp.exp(g_col - jnp.broadcast_to(g_row, (LANES, LANES))), 0.0)
            heads.append(dict(
                q=q, k=k, v=v[:, e * hd:(e + 1) * hd], z=z[:, e * hd:(e + 1) * hd], qk=qk,
                decay=decay, g_col=g_col, b_col=b_col,
                a=jnp.where(strict, b_col * kk * decay, 0.0)))
    t_invs = _unit_lower_inverses([h["a"] for h in heads], masks_ref)
    for h, t_inv in zip(heads, t_invs):
        eg = jnp.exp(h["g_col"])
        rhs = jnp.concatenate([h["k"] * (h["b_col"] * eg), h["v"] * h["b_col"]], axis=1)
        h["wu"] = _dot(t_inv, rhs)
        h["q_dec"] = h["q"] * eg
        h["att"] = h["qk"] * h["decay"]
    for h in heads:
        k_end = h["k"] * jnp.exp(_last_row_of_segments(h["g_col"], block) - h["g_col"])
        h["k_end_t"] = jnp.transpose(k_end)
    return heads


def _gdn_inter(heads, states, ng, block, hd):
    segs = _segments(block)
    ws_qs = [[_dot(jnp.concatenate([h["wu"][r0:r1, :hd], h["q_dec"][r0:r1]], axis=0), st[s])
              for s, (r0, r1) in enumerate(segs)] for h, st in zip(heads, states)]
    v_news = []
    for h, per_seg in zip(heads, ws_qs):
        parts = [h["wu"][r0:r1, hd:] - x[:r1 - r0] for (r0, r1), x in zip(segs, per_seg)]
        v_news.append(jnp.concatenate(parts, axis=0) if len(parts) > 1 else parts[0])
    if len(segs) == 1:
        both = [_dot(jnp.concatenate([h["att"], h["k_end_t"]], axis=0), v_new)
                for h, v_new in zip(heads, v_news)]
        o_intra = [x[:LANES] for x in both]
        updates = [[x[LANES:]] for x in both]
    else:
        o_intra = [_dot(h["att"], v_new) for h, v_new in zip(heads, v_news)]
        updates = [[_dot(h["k_end_t"], _rows_only(v_new, r0, r1)) for (r0, r1) in segs]
                   for h, v_new in zip(heads, v_news)]
    outs, leaving = [], []
    for h, st, per_seg, oi, upd in zip(heads, states, ws_qs, o_intra, updates):
        qs = [x[r1 - r0:] for (r0, r1), x in zip(segs, per_seg)]
        o = oi + (jnp.concatenate(qs, axis=0) if len(qs) > 1 else qs[0])
        outs.append(_rms_norm_rows(o, ng) * _silu(h["z"]))
        leaving.append([jnp.exp(h["g_col"][r1 - 1:r1, :]) * st[s] + upd[s]
                        for s, (r0, r1) in enumerate(segs)])
    return outs, leaving


def _gdn_prompt_body(x_ref, wg_ref, wab_ref, alog_ref, dtb_ref, cw_ref, ng_ref, masks_ref,
                     og_ref, so_ref, tail_ref, p0_ref, p1_ref, gb0_ref, gb1_ref, xb_ref, s_ref, prev_ref,
                     *, blocks_per_seq, n_pairs, q_scale, hd):
    step = pl.program_id(0)
    n_groups, sb, group_w = p0_ref.shape
    n_heads = 2 * n_pairs
    key_w = n_pairs * hd

    @pl.when(step == 0)
    def _():
        p1_ref[...] = jnp.zeros_like(p1_ref)
        gb1_ref[...] = jnp.zeros_like(gb1_ref)
        s_ref[...] = jnp.zeros_like(s_ref)
        prev_ref[...] = jnp.zeros_like(prev_ref)

    block_in_seq = lax.rem(jnp.maximum(step - 1, 0), blocks_per_seq)
    seq_start = block_in_seq == 0

    @pl.when(seq_start)
    def _():
        s_ref[...] = jnp.zeros_like(s_ref)

    xb_ref[...] = x_ref[...].astype(BF16)

    def cols_of(c0, width):
        g, off = divmod(c0, group_w)
        assert off + width <= group_w
        return g, slice(off, off + width)

    def run(p_cur, gb_cur, p_nxt, gb_nxt):
        gb_nxt[...] = _gdn_gate_values(xb_ref[...], wab_ref[...], alog_ref[...], dtb_ref[...])

        def tile(t, carry):
            r0 = pl.multiple_of(t * LANES, LANES)
            rows = pl.ds(r0, LANES)
            p_nxt[t] = jnp.dot(xb_ref[...], wg_ref[t], preferred_element_type=F32)
            prev = pl.ds(pl.multiple_of(jnp.maximum(r0 - SUBLANES, 0), SUBLANES), SUBLANES)

            def conv(c0, width):
                g, cols = cols_of(c0, width)
                carried = jnp.where(seq_start, 0.0, prev_ref[:, c0:c0 + width])
                prev8 = jnp.where(t > 0, p_cur[g, prev, cols], carried)
                return _silu(_causal_conv(prev8, p_cur[g, rows, cols], cw_ref[:, c0:c0 + width]))

            pairs = []
            for p in range(n_pairs):
                q = _l2_norm_rows(conv(p * hd, hd)) * q_scale
                k = _l2_norm_rows(conv(key_w + p * hd, hd))
                v = conv(2 * key_w + 2 * p * hd, 2 * hd)
                gz, zc = cols_of(4 * key_w + 2 * p * hd, 2 * hd)
                pairs.append((q, k, v, p_cur[gz, rows, zc],
                              gb_cur[p * SUBLANES:(p + 1) * SUBLANES, rows]))
            heads = _gdn_intra(pairs, LANES, hd, masks_ref)
            outs, leaving = _gdn_inter(heads, [[s_ref[h]] for h in range(n_heads)], ng_ref[...],
                                       LANES, hd)
            for h in range(n_heads):
                og_ref[rows, h * hd:(h + 1) * hd] = outs[h].astype(BF16)
                s_ref[h] = leaving[h][0]
            return carry

        lax.fori_loop(0, n_groups, tile, 0)
        for c0 in range(0, 4 * key_w, hd):
            g, cols = cols_of(c0, hd)
            tail = p_cur[g, sb - SUBLANES:sb, cols]
            tail_ref[:, c0:c0 + hd] = tail
            prev_ref[:, c0:c0 + hd] = tail

    parity = lax.rem(step, 2)

    @pl.when(parity == 0)
    def _():
        run(p1_ref, gb1_ref, p0_ref, gb0_ref)

    @pl.when(parity == 1)
    def _():
        run(p0_ref, gb0_ref, p1_ref, gb1_ref)

    @pl.when(block_in_seq == blocks_per_seq - 1)
    def _():
        so_ref[...] = s_ref[...]


def _gdn_sample_body(q_ref, k_ref, v_ref, z_ref, gb_ref, wq_ref, wk_ref, wv_ref, ng_ref, masks_ref,
                     hq_ref, hk_ref, hv_ref, s_ref, og_ref, so_ref, *, block, n_pairs, q_scale, hd):
    segs = _segments(block)

    def conv(ref, hist_ref, w_ref, cols):
        x = ref[:, cols]
        w = w_ref[:, cols]
        return _silu(jnp.concatenate(
            [_causal_conv(hist_ref[s, :, cols], x[r0:r1], w) for s, (r0, r1) in enumerate(segs)],
            axis=0))

    pairs = []
    for p in range(n_pairs):
        c1 = slice(p * hd, (p + 1) * hd)
        c2 = slice(2 * p * hd, 2 * (p + 1) * hd)
        q = _l2_norm_rows(conv(q_ref, hq_ref, wq_ref, c1)) * q_scale
        k = _l2_norm_rows(conv(k_ref, hk_ref, wk_ref, c1))
        v = conv(v_ref, hv_ref, wv_ref, c2)
        pairs.append((q, k, v, z_ref[:, c2], gb_ref[p * SUBLANES:(p + 1) * SUBLANES, :]))
    heads = _gdn_intra(pairs, block, hd, masks_ref)
    states = [[s_ref[s, h] for s in range(len(segs))] for h in range(2 * n_pairs)]
    outs, leaving = _gdn_inter(heads, states, ng_ref[...], block, hd)
    for h in range(2 * n_pairs):
        og_ref[:, h * hd:(h + 1) * hd] = outs[h].astype(BF16)
        for s in range(len(segs)):
            so_ref[s, h] = leaving[h][s]


def _gdn_prompt(x, w_main, w_ab_t, a_log_rows, dt_bias_rows, conv_w, norm_g, layer, n_layers,
                prev_states, *, n_prompt, batch, hq, hd):
    d = x.shape[1]
    t = n_prompt // batch
    n_main = w_main.shape[1]
    conv_dim = conv_w.shape[1]
    val_w = 2 * hq * hd
    n_rows = w_ab_t.shape[0]
    sb = math.gcd(GDN_SEQ_BLOCK, t)
    n_groups = sb // LANES
    group_w = n_main // n_groups
    assert n_main % n_groups == 0 and group_w % (2 * hd) == 0
    bps = t // sb
    n_blocks = n_prompt // sb
    w_groups = w_main.reshape(d, n_groups, group_w).transpose(1, 0, 2)
    masks = _inverse_level_masks(LANES)

    def recurred(i):
        return jnp.maximum(i - 1, 0)

    body, more_specs, more_args, aliases = _carried(
        functools.partial(_gdn_prompt_body, blocks_per_seq=bps, n_pairs=hq, q_scale=hd ** -0.5, hd=hd),
        8, prev_states)
    return pl.pallas_call(
        body,
        grid=(n_blocks + 1,),
        in_specs=[pl.BlockSpec((sb, d), lambda i: (jnp.minimum(i, n_blocks - 1), 0)),
                  _resident(w_groups.shape), _resident(w_ab_t.shape), _resident((n_rows, 1)),
                  _resident((n_rows, 1)), _resident(conv_w.shape), _resident((1, hd)),
                  _resident(masks.shape)] + more_specs,
        out_specs=[pl.BlockSpec((sb, val_w), lambda i: (recurred(i), 0)),
                   pl.BlockSpec((None, None, 2 * hq, hd, hd),
                                lambda i: (recurred(i) // bps, layer, 0, 0, 0)),
                   pl.BlockSpec((SUBLANES, conv_dim), lambda i: (recurred(i), 0))],
        out_shape=[jax.ShapeDtypeStruct((n_prompt, val_w), BF16),
                   jax.ShapeDtypeStruct((batch, n_layers, 2 * hq, hd, hd), F32),
                   jax.ShapeDtypeStruct((n_blocks * SUBLANES, conv_dim), F32)],
        input_output_aliases=aliases,
        scratch_shapes=[pltpu.VMEM((n_groups, sb, group_w), F32), pltpu.VMEM((n_groups, sb, group_w), F32),
                        pltpu.VMEM((n_rows, sb), F32), pltpu.VMEM((n_rows, sb), F32),
                        pltpu.VMEM((sb, d), BF16), pltpu.VMEM((2 * hq, hd, hd), F32),
                        pltpu.VMEM((SUBLANES, conv_dim), F32)],
        compiler_params=_cparams(1),
        name="gdn_prompt",
    )(x, w_groups, w_ab_t, a_log_rows, dt_bias_rows, conv_w, norm_g.reshape(1, hd), masks,
      *more_args)


def _gdn_sample(p, gb, conv_w, norm_g, hist8, state_in, layer, prev_states, *, hq, hd, dec_seq):
    n_sample = p.shape[0]
    key_w = hq * hd
    val_w = 2 * key_w
    q_scale = hd ** -0.5
    gp = math.gcd(GDN_SAMPLE_PAIRS, hq)
    qw, vw = gp * hd, 2 * gp * hd
    k_blk0 = key_w // qw
    ng = norm_g.reshape(1, hd)
    seqs = LANES // dec_seq
    masks_s = _inverse_level_masks(dec_seq)
    state_spec = pl.BlockSpec((seqs, None, 2 * gp, hd, hd), lambda g, h: (g, layer, h, 0, 0))
    body, more_specs, more_args, aliases = _carried(
        functools.partial(_gdn_sample_body, block=dec_seq, n_pairs=gp, q_scale=q_scale, hd=hd), 14,
        prev_states)
    og_s, s_s = pl.pallas_call(
        body,
        grid=(n_sample // LANES, hq // gp),
        in_specs=[pl.BlockSpec((LANES, qw), lambda g, h: (g, h)),
                  pl.BlockSpec((LANES, qw), lambda g, h: (g, k_blk0 + h)),
                  pl.BlockSpec((LANES, vw), lambda g, h: (g, k_blk0 + h)),
                  pl.BlockSpec((LANES, vw), lambda g, h: (g, 2 * k_blk0 + h)),
                  pl.BlockSpec((gp * SUBLANES, LANES), lambda g, h: (h, g)),
                  pl.BlockSpec((CONV_TAPS, qw), lambda g, h: (0, h)),
                  pl.BlockSpec((CONV_TAPS, qw), lambda g, h: (0, k_blk0 + h)),
                  pl.BlockSpec((CONV_TAPS, vw), lambda g, h: (0, k_blk0 + h)),
                  _resident((1, hd)), _resident(masks_s.shape),
                  pl.BlockSpec((seqs, SUBLANES, qw), lambda g, h: (g, 0, h)),
                  pl.BlockSpec((seqs, SUBLANES, qw), lambda g, h: (g, 0, k_blk0 + h)),
                  pl.BlockSpec((seqs, SUBLANES, vw), lambda g, h: (g, 0, k_blk0 + h)),
                  state_spec] + more_specs,
        out_specs=[pl.BlockSpec((LANES, vw), lambda g, h: (g, h)), state_spec],
        out_shape=[jax.ShapeDtypeStruct((n_sample, val_w), BF16),
                   jax.ShapeDtypeStruct(state_in.shape, F32)],
        input_output_aliases=aliases,
        compiler_params=_cparams(2),
        name="gdn_sample",
    )(p, p, p, p, gb, conv_w, conv_w, conv_w, ng, masks_s, hist8, hist8, hist8, state_in,
      *more_args)
    return og_s, s_s


def _gdn_gate_rows(w_in, a_log, dt_bias, n_main, hq):
    d = w_in.shape[0]
    hv = a_log.shape[0]
    rep = hv // hq
    pad = SUBLANES - 2 * rep
    wa = w_in[:, n_main:n_main + hv].T.reshape(hq, rep, d)
    wb = w_in[:, n_main + hv:n_main + 2 * hv].T.reshape(hq, rep, d)
    w_ab_t = jnp.concatenate([wa, wb, jnp.zeros((hq, pad, d), w_in.dtype)], axis=1)
    zeros = jnp.zeros((hq, SUBLANES - rep), F32)
    alog = jnp.concatenate([a_log.reshape(hq, rep), zeros], axis=1).reshape(hq * SUBLANES, 1)
    dtb = jnp.concatenate([dt_bias.reshape(hq, rep), zeros], axis=1).reshape(hq * SUBLANES, 1)
    return w_ab_t.reshape(hq * SUBLANES, d).astype(BF16), alog, dtb


def kernel(x_prompt, x_sample, state_gla, state_gdn, state_gdn_conv, ln_g, ln_b, ffn1_w_gu, ffn1_w_d, ffn2_w_gu, ffn2_w_d, gla_w_in, gla_w_a1, gla_w_a2, gla_b_a, gla_norm_g, gla_w_o, gdn_w_in, gdn_conv_w, gdn_a_log, gdn_dt_bias, gdn_norm_g, gdn_w_o):
    batch, t, d = x_prompt.shape
    dec_batch, dec_seq, _ = x_sample.shape
    depth = ln_g.shape[0]
    alpha = (2.0 * depth) ** 0.25
    n_prompt, n_sample = batch * t, dec_batch * dec_seq
    tm = math.gcd(TOKEN_TILE, math.gcd(n_prompt, n_sample))
    assert t % LANES == 0 and t % GLA_CHUNK == 0 and n_sample % LANES == 0 and LANES % dec_seq == 0
    assert tm % LANES == 0

    gla_heads, gla_dk, gla_dv = state_gla.shape[2:]
    hv, hd = state_gdn.shape[2:4]
    conv_dim = gdn_conv_w.shape[2]
    val_w = hv * hd
    hq = (conv_dim - val_w) // (2 * hd)
    assert hv == 2 * hq, "the gated-delta kernel pairs two v heads with each q/k head"
    n_main = conv_dim + val_w

    x = (x_prompt.reshape(n_prompt, d), x_sample.reshape(n_sample, d))
    new_gla_p = jnp.zeros((batch,) + state_gla.shape[1:], F32)
    new_gla_s = jnp.zeros(state_gla.shape, F32)
    new_gdn_p = jnp.zeros((batch,) + state_gdn.shape[1:], F32)
    new_gdn_s = jnp.zeros(state_gdn.shape, F32)
    new_conv_p, new_conv_s = [], []
    for i in range(depth):
        x = _ffn_ln(x, ffn1_w_gu[i].astype(BF16), ffn1_w_d[i].astype(BF16), ln_g[i, 0], ln_b[i, 0],
                    tm=tm, alpha=alpha, n_prompt=n_prompt)
        j = i // 2
        if i % 2 == 0:
            rank = gla_w_a1.shape[2]
            w_a1 = jnp.pad(gla_w_a1[j], ((0, 0), (0, LANES - rank))).astype(BF16)
            w_a2 = jnp.pad(gla_w_a2[j], ((0, LANES - rank), (0, 0))).astype(BF16)
            w_in = gla_w_in[j].astype(BF16)
            og_p, new_gla_p = _gla_prompt(
                x, w_in, w_a1, w_a2, gla_b_a[j], gla_norm_g[j], j, state_gla.shape[1], new_gla_p,
                n_prompt=n_prompt, batch=batch, heads=gla_heads, dk=gla_dk, dv=gla_dv)
            p_s, la_s = _gla_proj(x, w_in, w_a1, w_a2, gla_b_a[j], tm=tm,
                                  first_tile=n_prompt // tm, n_tiles=n_sample // tm)
            og_s, new_gla_s = _gla_sample(p_s, la_s, gla_norm_g[j], state_gla, j, new_gla_s,
                                          heads=gla_heads, dk=gla_dk, dv=gla_dv, dec_seq=dec_seq)
            w_o = gla_w_o[j]
        else:
            w_ab_t, alog_rows, dtb_rows = _gdn_gate_rows(gdn_w_in[j], gdn_a_log[j], gdn_dt_bias[j],
                                                         n_main, hq)
            w_main = gdn_w_in[j][:, :n_main].astype(BF16)
            og_p, new_gdn_p, tails = _gdn_prompt(
                x, w_main, w_ab_t, alog_rows, dtb_rows, gdn_conv_w[j], gdn_norm_g[j], j,
                state_gdn.shape[1], new_gdn_p, n_prompt=n_prompt, batch=batch, hq=hq, hd=hd)
            p_s, gb_s = _gdn_proj(x, w_main, w_ab_t, alog_rows, dtb_rows, tm=tm,
                                  first_tile=n_prompt // tm, n_tiles=n_sample // tm)
            hist8 = jnp.pad(state_gdn_conv[:, j],
                            ((0, 0), (SUBLANES - (CONV_TAPS - 1), 0), (0, 0)))
            og_s, new_gdn_s = _gdn_sample(p_s, gb_s, gdn_conv_w[j], gdn_norm_g[j], hist8, state_gdn, j,
                                          new_gdn_s, hq=hq, hd=hd, dec_seq=dec_seq)
            keep = CONV_TAPS - 1
            new_conv_p.append(tails.reshape(batch, -1, conv_dim)[:, -keep:, :])
            new_conv_s.append(p_s.reshape(dec_batch, dec_seq, n_main)[:, -keep:, :conv_dim])
            w_o = gdn_w_o[j]
        x = _ffn_ln(x, ffn2_w_gu[i].astype(BF16), ffn2_w_d[i].astype(BF16), ln_g[i, 2], ln_b[i, 2],
                    tm=tm, alpha=alpha, n_prompt=n_prompt, split_out=(i == depth - 1),
                    mixer=(og_p, og_s, w_o.astype(BF16), ln_g[i, 1], ln_b[i, 1]))
    y_prompt, y_sample = x
    return (y_prompt.reshape(batch, t, d), y_sample.reshape(dec_batch, dec_seq, d),
            new_gla_p, new_gdn_p, jnp.stack(new_conv_p, 1),
            new_gla_s, new_gdn_s, jnp.stack(new_conv_s, 1))
```

```python
import functools
import math

import numpy as np

import jax
import jax.numpy as jnp
from jax import lax
from jax.experimental import pallas as pl
from jax.experimental.pallas import tpu as pltpu

F32 = jnp.float32
BF16 = jnp.bfloat16

LN_EPS = 1e-5
RMS_EPS = 1e-6
L2_EPS = 1e-6
FFN_RES = 0.5
GLA_GATE_TAU = 16.0
GLA_CHUNK = 64
CONV_TAPS = 4

LANES = 128
SUBLANES = 8
VMEM_LIMIT_BYTES = 60 * 1024 * 1024
TOKEN_TILE = 512
MATMUL_N_CHUNK = 1024
GDN_SEQ_BLOCK = 512
GDN_SAMPLE_PAIRS = 4
GLA_SEQ_BLOCK = 512


def _cparams(n_axes):
    return pltpu.CompilerParams(
        dimension_semantics=("arbitrary",) * n_axes, vmem_limit_bytes=VMEM_LIMIT_BYTES)


def _resident(shape):
    zeros = (0,) * len(shape)
    return pl.BlockSpec(shape, lambda *_: zeros, pipeline_mode=pl.Buffered(1))


def _dot(a, b):
    return jnp.dot(a.astype(BF16), b.astype(BF16), preferred_element_type=F32)


def _dot_nt(a, b):
    return lax.dot_general(a.astype(BF16), b.astype(BF16), (((1,), (1,)), ((), ())),
                           preferred_element_type=F32)


def _silu(x):
    return x * jax.nn.sigmoid(x)


def _softplus(x):
    return jnp.maximum(x, 0.0) + jnp.log1p(jnp.exp(-jnp.abs(x)))


def _layer_norm_rows(y, g, b):
    mu = jnp.mean(y, axis=-1, keepdims=True)
    d = y - mu
    var = jnp.mean(d * d, axis=-1, keepdims=True)
    return d * lax.rsqrt(var + LN_EPS) * g + b


def _rms_norm_rows(o, g):
    return o * lax.rsqrt(jnp.mean(o * o, axis=-1, keepdims=True) + RMS_EPS) * g


def _l2_norm_rows(x):
    return x * lax.rsqrt(jnp.sum(x * x, axis=-1, keepdims=True) + L2_EPS)


def _col_chunks(n, step=MATMUL_N_CHUNK):
    return [(c, min(c + step, n)) for c in range(0, n, step)]


def _ffn_ln_body(*refs, d_ff, alpha, n_in, has_mixer, n_prompt_tiles):
    x_refs, refs = refs[:n_in], refs[n_in:]
    is_prompt = pl.program_id(0) < n_prompt_tiles
    if has_mixer:
        (ogp_ref, ogs_ref, wo_ref, g1_ref, b1_ref), refs = refs[:5], refs[5:]
    wgu_ref, wd_ref, g_ref, b_ref = refs[:4]
    o_refs, h_ref = refs[4:-1], refs[-1]
    x = x_refs[0][...]
    if n_in == 2:
        x = jnp.where(is_prompt, x, x_refs[1][...])
    if has_mixer:
        og = jnp.where(is_prompt, ogp_ref[...], ogs_ref[...])
        x = _layer_norm_rows(alpha * x + jnp.dot(og, wo_ref[...], preferred_element_type=F32),
                             g1_ref[...], b1_ref[...])
    xb = x.astype(BF16)
    for c0, c1 in _col_chunks(d_ff):
        gate = jnp.dot(xb, wgu_ref[:, c0:c1], preferred_element_type=F32)
        up = jnp.dot(xb, wgu_ref[:, d_ff + c0:d_ff + c1], preferred_element_type=F32)
        h_ref[:, c0:c1] = (_silu(gate) * up).astype(BF16)
    y = jnp.dot(h_ref[...], wd_ref[...], preferred_element_type=F32)
    out = _layer_norm_rows(alpha * x + FFN_RES * y, g_ref[...], b_ref[...])
    if len(o_refs) == 1:
        o_refs[0][...] = out
    else:
        o_refs[1][...] = out

        @pl.when(is_prompt)
        def _():
            o_refs[0][...] = out


def _ffn_ln(xs, w_gu, w_d, g, b, *, tm, alpha, n_prompt, mixer=None, split_out=False):
    xs = xs if isinstance(xs, (tuple, list)) else (xs,)
    d = xs[0].shape[1]
    n = sum(x.shape[0] for x in xs)
    d_ff = w_d.shape[0]
    ntp = n_prompt // tm

    def pair(width):
        return [pl.BlockSpec((tm, width), lambda i: (jnp.minimum(i, ntp - 1), 0)),
                pl.BlockSpec((tm, width), lambda i: (jnp.maximum(i - ntp, 0), 0))]

    tile = pl.BlockSpec((tm, d), lambda i: (i, 0))
    vec = _resident((1, d))
    pair_shape = [jax.ShapeDtypeStruct((n_prompt, d), F32), jax.ShapeDtypeStruct((n - n_prompt, d), F32)]
    mixer_specs, mixer_args = [], []
    if mixer is not None:
        og_prompt, og_sample, w_o, g1, b1 = mixer
        mixer_specs = pair(w_o.shape[0]) + [_resident(w_o.shape), vec, vec]
        mixer_args = [og_prompt, og_sample, w_o, g1.reshape(1, d), b1.reshape(1, d)]
    return pl.pallas_call(
        functools.partial(_ffn_ln_body, d_ff=d_ff, alpha=alpha, n_in=len(xs),
                          has_mixer=mixer is not None, n_prompt_tiles=ntp),
        grid=(n // tm,),
        in_specs=(pair(d) if len(xs) == 2 else [tile]) + mixer_specs + [
            _resident(w_gu.shape), _resident(w_d.shape), vec, vec],
        out_specs=pair(d) if split_out else tile,
        out_shape=pair_shape if split_out else jax.ShapeDtypeStruct((n, d), F32),
        scratch_shapes=[pltpu.VMEM((tm, d_ff), BF16)],
        compiler_params=_cparams(1),
        name="ffn_ln",
    )(*xs, *mixer_args, w_gu, w_d, g.reshape(1, d), b.reshape(1, d))


def _gla_log_decay(xb, wa1, wa2, ba):
    low = jnp.dot(xb, wa1, preferred_element_type=F32)
    logits = _dot(low, wa2) + ba
    log_sig = jnp.minimum(logits, 0.0) - jnp.log1p(jnp.exp(-jnp.abs(logits)))
    return log_sig / GLA_GATE_TAU


def _project_groups(xb, wg_ref, p_ref):
    n_groups, _, group_w = wg_ref.shape
    for g in range(n_groups):
        p_ref[:, g * group_w:(g + 1) * group_w] = jnp.dot(xb, wg_ref[g], preferred_element_type=F32)


def _gla_proj_body(x_ref, wg_ref, wa1_ref, wa2_ref, ba_ref, p_ref, la_ref):
    xb = x_ref[...].astype(BF16)
    _project_groups(xb, wg_ref, p_ref)
    la_ref[...] = _gla_log_decay(xb, wa1_ref[...], wa2_ref[...], ba_ref[...])


def _gla_proj(x, w_in, w_a1, w_a2, b_a, *, tm, first_tile, n_tiles):
    d = x.shape[1]
    n = n_tiles * tm
    n_out = w_in.shape[0] * w_in.shape[2]
    qk_w = w_a2.shape[1]
    return pl.pallas_call(
        _gla_proj_body,
        grid=(n_tiles,),
        in_specs=[pl.BlockSpec((tm, d), lambda i: (first_tile + i, 0)), _resident(w_in.shape),
                  _resident(w_a1.shape), _resident(w_a2.shape), _resident((1, qk_w))],
        out_specs=[pl.BlockSpec((tm, n_out), lambda i: (i, 0)),
                   pl.BlockSpec((tm, qk_w), lambda i: (i, 0))],
        out_shape=[jax.ShapeDtypeStruct((n, n_out), F32), jax.ShapeDtypeStruct((n, qk_w), F32)],
        compiler_params=_cparams(1),
        name="gla_proj",
    )(x, w_in, w_a1, w_a2, b_a.reshape(1, qk_w))


def _gdn_gate_values(xb, wab, alog, dtb):
    ab = lax.dot_general(wab, xb, (((1,), (1,)), ((), ())), preferred_element_type=F32)
    kind = _iota2(ab.shape, 0) & (SUBLANES - 1)
    g = -jnp.exp(alog) * _softplus(ab + dtb)
    beta = jax.nn.sigmoid(ab)
    return jnp.where(kind < 2, g, jnp.where(kind < 4, beta, 0.0))


def _gdn_proj_body(x_ref, wg_ref, wab_ref, alog_ref, dtb_ref, p_ref, gb_ref):
    xb = x_ref[...].astype(BF16)
    _project_groups(xb, wg_ref, p_ref)
    gb_ref[...] = _gdn_gate_values(xb, wab_ref[...], alog_ref[...], dtb_ref[...])


def _gdn_proj(x, w_main, w_ab_t, a_log_rows, dt_bias_rows, *, tm, first_tile, n_tiles):
    d = x.shape[1]
    n = n_tiles * tm
    n_out = w_main.shape[0] * w_main.shape[2]
    n_rows = w_ab_t.shape[0]
    return pl.pallas_call(
        _gdn_proj_body,
        grid=(n_tiles,),
        in_specs=[pl.BlockSpec((tm, d), lambda i: (first_tile + i, 0)), _resident(w_main.shape),
                  _resident(w_ab_t.shape), _resident((n_rows, 1)), _resident((n_rows, 1))],
        out_specs=[pl.BlockSpec((tm, n_out), lambda i: (i, 0)),
                   pl.BlockSpec((n_rows, tm), lambda i: (0, i))],
        out_shape=[jax.ShapeDtypeStruct((n, n_out), F32), jax.ShapeDtypeStruct((n_rows, n), F32)],
        compiler_params=_cparams(1),
        name="gdn_proj",
    )(x, w_main, w_ab_t, a_log_rows, dt_bias_rows)


def _iota2(shape, dim):
    return lax.broadcasted_iota(jnp.int32, shape, dim)


def _block_of(idx, block):
    return idx >> (block.bit_length() - 1)


def _block_cumsum(x, block, axis):
    pos = _iota2(x.shape, axis) & (block - 1)
    shift = 1
    while shift < block:
        x = x + jnp.where(pos >= shift, pltpu.roll(x, shift, axis), 0.0)
        shift *= 2
    return x


def _col_form(row):
    return jnp.transpose(jnp.broadcast_to(row, (LANES, LANES)))


def _segments(block):
    return [(r, r + block) for r in range(0, LANES, block)]


def _last_row_of_segments(x, block):
    return jnp.concatenate(
        [jnp.broadcast_to(x[r1 - 1:r1, :], (r1 - r0, x.shape[1])) for r0, r1 in _segments(block)],
        axis=0)


def _rows_only(x, r0, r1):
    rows = _iota2(x.shape, 0)
    return jnp.where((rows >= r0) & (rows < r1), x, 0.0)


def _gla_tiles(heads, ng, block, chained):
    row = _iota2((LANES, LANES), 0)
    col = _iota2((LANES, LANES), 1)
    causal = (_block_of(row, block) == _block_of(col, block)) & (row >= col)
    segs = _segments(block)
    bcums = [_block_cumsum(la, block, 0) for (_, _, _, la, _, _) in heads]
    q_decs = [q * jnp.exp(b) for (q, _, _, _, _, _), b in zip(heads, bcums)]
    atts = [jnp.where(causal, _dot_nt(qd, k * jnp.exp(-b)), 0.0)
            for qd, (_, k, _, _, _, _), b in zip(q_decs, heads, bcums)]
    intras = [_dot(att, v) for att, (_, _, v, _, _, _) in zip(atts, heads)]
    k_end_ts = [jnp.transpose(k * jnp.exp(_last_row_of_segments(b, block) - b))
                for (_, k, _, _, _, _), b in zip(heads, bcums)]
    lane_reps = heads[0][2].shape[1] // LANES
    states = [list(st) for (_, _, _, _, _, st) in heads]
    cur = [st[0] for st in states]
    outs = [[] for _ in heads]
    leaving = [[] for _ in heads]
    for s, (r0, r1) in enumerate(segs):
        if not chained:
            cur = [st[s] for st in states]
        inter = [_dot(qd[r0:r1], c) for qd, c in zip(q_decs, cur)]
        upd = [_dot(kt, _rows_only(v, r0, r1)) for kt, (_, _, v, _, _, _) in zip(k_end_ts, heads)]
        for i, b in enumerate(bcums):
            outs[i].append(intras[i][r0:r1] + inter[i])
            decay = jnp.concatenate([_col_form(jnp.exp(b[r1 - 1:r1, :]))] * lane_reps, axis=1)
            cur[i] = decay * cur[i] + upd[i]
            if not chained:
                leaving[i].append(cur[i])
    results = []
    for i, (_, _, _, _, r, _) in enumerate(heads):
        o = _rms_norm_rows(jnp.concatenate(outs[i], axis=0), ng)
        results.append((o * _silu(r), leaving[i] if not chained else [cur[i]]))
    return results


def _gla_prompt_body(x_ref, wg_ref, wa1_ref, wa2_ref, ba_ref, ng_ref, og_ref, so_ref,
                     p0_ref, p1_ref, la0_ref, la1_ref, xb_ref, s_ref, *,
                     blocks_per_seq, n_heads, dk, dv, q_scale):
    step = pl.program_id(0)
    n_groups, _, group_w = p0_ref.shape
    qk_w = n_heads * dk

    @pl.when(step == 0)
    def _():
        p1_ref[...] = jnp.zeros_like(p1_ref)
        la1_ref[...] = jnp.zeros_like(la1_ref)
        s_ref[...] = jnp.zeros_like(s_ref)

    block_in_seq = lax.rem(jnp.maximum(step - 1, 0), blocks_per_seq)
    seq_start = block_in_seq == 0

    @pl.when(seq_start)
    def _():
        s_ref[...] = jnp.zeros_like(s_ref)

    xb_ref[...] = x_ref[...].astype(BF16)

    def cols_of(c0, width):
        g, off = divmod(c0, group_w)
        assert off + width <= group_w
        return g, slice(off, off + width)

    def run(p_cur, la_cur, p_nxt, la_nxt):
        la_nxt[...] = _gla_log_decay(xb_ref[...], wa1_ref[...], wa2_ref[...], ba_ref[...])

        def tile(t, carry):
            rows = pl.ds(pl.multiple_of(t * LANES, LANES), LANES)
            p_nxt[t] = jnp.dot(xb_ref[...], wg_ref[t], preferred_element_type=F32)
            heads = []
            for h in range(n_heads):
                gq, cq = cols_of(h * dk, dk)
                gk, ck = cols_of(qk_w + h * dk, dk)
                gv, cv = cols_of(2 * qk_w + h * dv, dv)
                gr, cr = cols_of(2 * qk_w + n_heads * dv + h * dv, dv)
                heads.append((p_cur[gq, rows, cq] * q_scale, p_cur[gk, rows, ck],
                              p_cur[gv, rows, cv], la_cur[rows, h * dk:(h + 1) * dk],
                              p_cur[gr, rows, cr], [s_ref[h]]))
            for h, (o, leaving) in enumerate(_gla_tiles(heads, ng_ref[...], GLA_CHUNK, True)):
                og_ref[rows, h * dv:(h + 1) * dv] = o.astype(BF16)
                s_ref[h] = leaving[0]
            return carry

        lax.fori_loop(0, n_groups, tile, 0)

    parity = lax.rem(step, 2)

    @pl.when(parity == 0)
    def _():
        run(p1_ref, la1_ref, p0_ref, la0_ref)

    @pl.when(parity == 1)
    def _():
        run(p0_ref, la0_ref, p1_ref, la1_ref)

    @pl.when(block_in_seq == blocks_per_seq - 1)
    def _():
        so_ref[...] = s_ref[...]


def _gla_sample_body(q_ref, k_ref, v_ref, r_ref, la_ref, ng_ref, s_ref, og_ref, so_ref, *, block,
                     n_heads, dk, dv, q_scale):
    n_seq = LANES // block
    heads = []
    for h in range(n_heads):
        ck = slice(h * dk, (h + 1) * dk)
        cv = slice(h * dv, (h + 1) * dv)
        heads.append((q_ref[:, ck] * q_scale, k_ref[:, ck], v_ref[:, cv], la_ref[:, ck], r_ref[:, cv],
                      [s_ref[i, h] for i in range(n_seq)]))
    for h, (o, leaving) in enumerate(_gla_tiles(heads, ng_ref[...], block, False)):
        og_ref[:, h * dv:(h + 1) * dv] = o.astype(BF16)
        for i in range(n_seq):
            so_ref[i, h] = leaving[i]


def _carried(body, n_inputs, prev):
    def skipping_carried_input(*refs):
        return body(*refs[:n_inputs], *refs[n_inputs + 1:])

    return skipping_carried_input, [pl.BlockSpec(memory_space=pl.ANY)], [prev], {n_inputs: 1}


def _gla_prompt(x, w_groups, w_a1, w_a2, b_a, norm_g, layer, n_layers, prev_states, *, n_prompt,
                batch, heads, dk, dv):
    d = x.shape[1]
    t = n_prompt // batch
    qk_w, v_w = heads * dk, heads * dv
    sb = math.gcd(GLA_SEQ_BLOCK, t)
    n_groups, _, group_w = w_groups.shape
    assert n_groups == sb // LANES and group_w % dv == 0
    bps = t // sb
    n_blocks = n_prompt // sb

    def recurred(i):
        return jnp.maximum(i - 1, 0)

    body, more_specs, more_args, aliases = _carried(
        functools.partial(_gla_prompt_body, blocks_per_seq=bps, n_heads=heads, dk=dk, dv=dv,
                          q_scale=dk ** -0.5), 6, prev_states)
    return pl.pallas_call(
        body,
        grid=(n_blocks + 1,),
        in_specs=[pl.BlockSpec((sb, d), lambda i: (jnp.minimum(i, n_blocks - 1), 0)),
                  _resident(w_groups.shape), _resident(w_a1.shape), _resident(w_a2.shape),
                  _resident((1, qk_w)), _resident((1, dv))] + more_specs,
        out_specs=[pl.BlockSpec((sb, v_w), lambda i: (recurred(i), 0)),
                   pl.BlockSpec((None, None, heads, dk, dv),
                                lambda i: (recurred(i) // bps, layer, 0, 0, 0))],
        out_shape=[jax.ShapeDtypeStruct((n_prompt, v_w), BF16),
                   jax.ShapeDtypeStruct((batch, n_layers, heads, dk, dv), F32)],
        scratch_shapes=[pltpu.VMEM((n_groups, sb, group_w), F32), pltpu.VMEM((n_groups, sb, group_w), F32),
                        pltpu.VMEM((sb, qk_w), F32), pltpu.VMEM((sb, qk_w), F32),
                        pltpu.VMEM((sb, d), BF16), pltpu.VMEM((heads, dk, dv), F32)],
        input_output_aliases=aliases,
        compiler_params=_cparams(1),
        name="gla_prompt",
    )(x, w_groups, w_a1, w_a2, b_a.reshape(1, qk_w), norm_g.reshape(1, dv), *more_args)


def _gla_sample(p, la, norm_g, state_in, layer, prev_states, *, heads, dk, dv, dec_seq):
    n_sample = p.shape[0]
    qk_w, v_w = heads * dk, heads * dv
    ng = norm_g.reshape(1, dv)
    seqs = LANES // dec_seq
    state_spec = pl.BlockSpec((seqs, None, heads, dk, dv), lambda g: (g, layer, 0, 0, 0))
    body, more_specs, more_args, aliases = _carried(
        functools.partial(_gla_sample_body, block=dec_seq, n_heads=heads, dk=dk, dv=dv,
                          q_scale=dk ** -0.5), 7, prev_states)
    og_s, s_s = pl.pallas_call(
        body,
        grid=(n_sample // LANES,),
        in_specs=[pl.BlockSpec((LANES, qk_w), lambda g: (g, 0)),
                  pl.BlockSpec((LANES, qk_w), lambda g: (g, 1)),
                  pl.BlockSpec((LANES, v_w), lambda g: (g, (2 * qk_w) // v_w)),
                  pl.BlockSpec((LANES, v_w), lambda g: (g, (2 * qk_w + v_w) // v_w)),
                  pl.BlockSpec((LANES, qk_w), lambda g: (g, 0)),
                  _resident((1, dv)), state_spec] + more_specs,
        out_specs=[pl.BlockSpec((LANES, v_w), lambda g: (g, 0)), state_spec],
        out_shape=[jax.ShapeDtypeStruct((n_sample, v_w), BF16),
                   jax.ShapeDtypeStruct(state_in.shape, F32)],
        input_output_aliases=aliases,
        compiler_params=_cparams(1),
        name="gla_sample",
    )(p, p, p, p, la, ng, state_in, *more_args)
    return og_s, s_s


def _causal_conv(prev8, x, w):
    n = x.shape[0]
    xs = jnp.concatenate([prev8, x], axis=0)
    acc = xs[SUBLANES:, :] * w[CONV_TAPS - 1:CONV_TAPS, :]
    for j in range(CONV_TAPS - 1):
        shifted = pltpu.roll(xs, CONV_TAPS - 1 - j, 0)
        acc = acc + shifted[SUBLANES:SUBLANES + n, :] * w[j:j + 1, :]
    return acc


def _inverse_level_masks(block):
    idx = np.arange(LANES)
    masks = []
    size = 1
    while size < block:
        pair = (idx[:, None] // (2 * size)) == (idx[None, :] // (2 * size))
        masks.append(pair & ((idx[:, None] // size) != (idx[None, :] // size)))
        size *= 2
    return jnp.asarray(np.stack(masks).astype(np.float32))


def _unit_lower_inverses(mats, masks_ref):
    row = _iota2((LANES, LANES), 0)
    col = _iota2((LANES, LANES), 1)
    eye = jnp.where(row == col, 1.0, 0.0)
    xs = [eye - a * masks_ref[0] for a in mats]
    mats_b = [a.astype(BF16) for a in mats]
    for level in range(1, masks_ref.shape[0]):
        mask = masks_ref[level]
        xbs = [x.astype(BF16) for x in xs]
        ys = [jnp.dot(xb, ab, preferred_element_type=F32) for xb, ab in zip(xbs, mats_b)]
        xs = [x - jnp.dot(y.astype(BF16), xb, preferred_element_type=F32) * mask
              for x, y, xb in zip(xs, ys, xbs)]
    return xs


def _gdn_intra(pairs, block, hd, masks_ref):
    row = _iota2((LANES, LANES), 0)
    col = _iota2((LANES, LANES), 1)
    same = _block_of(row, block) == _block_of(col, block)
    incl = same & (row >= col)
    strict = same & (row > col)
    kk_qks = [_dot_nt(jnp.concatenate([k, q], axis=0), k) for (q, k, _, _, _) in pairs]
    heads = []
    for (q, k, v, z, gb), kk_qk in zip(pairs, kk_qks):
        kk, qk = kk_qk[:LANES], kk_qk[LANES:]
        gcum = _block_cumsum(gb, block, 1)
        for e in range(2):
            g_row = gcum[e:e + 1, :]
            g_col = _col_form(g_row)
            b_col = _col_form(gb[2 + e:3 + e, :])
            decay = jnp.where(incl, jnp.exp(g_col - jnp.broadcast_to(g_row, (LANES, LANES))), 0.0)
            heads.append(dict(
                q=q, k=k, v=v[:, e * hd:(e + 1) * hd], z=z[:, e * hd:(e + 1) * hd], qk=qk,
                decay=decay, g_col=g_col, b_col=b_col,
                a=jnp.where(strict, b_col * kk * decay, 0.0)))
    t_invs = _unit_lower_inverses([h["a"] for h in heads], masks_ref)
    for h, t_inv in zip(heads, t_invs):
        eg = jnp.exp(h["g_col"])
        rhs = jnp.concatenate([h["k"] * (h["b_col"] * eg), h["v"] * h["b_col"]], axis=1)
        h["wu"] = _dot(t_inv, rhs)
        h["q_dec"] = h["q"] * eg
        h["att"] = h["qk"] * h["decay"]
    for h in heads:
        k_end = h["k"] * jnp.exp(_last_row_of_segments(h["g_col"], block) - h["g_col"])
        h["k_end_t"] = jnp.transpose(k_end)
    return heads


def _gdn_inter(heads, states, ng, block, hd):
    segs = _segments(block)
    ws_qs = [[_dot(jnp.concatenate([h["wu"][r0:r1, :hd], h["q_dec"][r0:r1]], axis=0), st[s])
              for s, (r0, r1) in enumerate(segs)] for h, st in zip(heads, states)]
    v_news = []
    for h, per_seg in zip(heads, ws_qs):
        parts = [h["wu"][r0:r1, hd:] - x[:r1 - r0] for (r0, r1), x in zip(segs, per_seg)]
        v_news.append(jnp.concatenate(parts, axis=0) if len(parts) > 1 else parts[0])
    if len(segs) == 1:
        both = [_dot(jnp.concatenate([h["att"], h["k_end_t"]], axis=0), v_new)
                for h, v_new in zip(heads, v_news)]
        o_intra = [x[:LANES] for x in both]
        updates = [[x[LANES:]] for x in both]
    else:
        o_intra = [_dot(h["att"], v_new) for h, v_new in zip(heads, v_news)]
        updates = [[_dot(h["k_end_t"], _rows_only(v_new, r0, r1)) for (r0, r1) in segs]
                   for h, v_new in zip(heads, v_news)]
    outs, leaving = [], []
    for h, st, per_seg, oi, upd in zip(heads, states, ws_qs, o_intra, updates):
        qs = [x[r1 - r0:] for (r0, r1), x in zip(segs, per_seg)]
        o = oi + (jnp.concatenate(qs, axis=0) if len(qs) > 1 else qs[0])
        outs.append(_rms_norm_rows(o, ng) * _silu(h["z"]))
        leaving.append([jnp.exp(h["g_col"][r1 - 1:r1, :]) * st[s] + upd[s]
                        for s, (r0, r1) in enumerate(segs)])
    return outs, leaving


def _gdn_prompt_body(x_ref, wg_ref, wab_ref, alog_ref, dtb_ref, cw_ref, ng_ref, masks_ref,
                     og_ref, so_ref, tail_ref, p0_ref, p1_ref, gb0_ref, gb1_ref, xb_ref, s_ref, prev_ref,
                     *, blocks_per_seq, n_pairs, q_scale, hd):
    step = pl.program_id(0)
    n_groups, sb, group_w = p0_ref.shape
    n_heads = 2 * n_pairs
    key_w = n_pairs * hd

    @pl.when(step == 0)
    def _():
        p1_ref[...] = jnp.zeros_like(p1_ref)
        gb1_ref[...] = jnp.zeros_like(gb1_ref)
        s_ref[...] = jnp.zeros_like(s_ref)
        prev_ref[...] = jnp.zeros_like(prev_ref)

    block_in_seq = lax.rem(jnp.maximum(step - 1, 0), blocks_per_seq)
    seq_start = block_in_seq == 0

    @pl.when(seq_start)
    def _():
        s_ref[...] = jnp.zeros_like(s_ref)

    xb_ref[...] = x_ref[...].astype(BF16)

    def cols_of(c0, width):
        g, off = divmod(c0, group_w)
        assert off + width <= group_w
        return g, slice(off, off + width)

    def run(p_cur, gb_cur, p_nxt, gb_nxt):
        gb_nxt[...] = _gdn_gate_values(xb_ref[...], wab_ref[...], alog_ref[...], dtb_ref[...])

        def tile(t, carry):
            r0 = pl.multiple_of(t * LANES, LANES)
            rows = pl.ds(r0, LANES)
            p_nxt[t] = jnp.dot(xb_ref[...], wg_ref[t], preferred_element_type=F32)
            prev = pl.ds(pl.multiple_of(jnp.maximum(r0 - SUBLANES, 0), SUBLANES), SUBLANES)

            def conv(c0, width):
                g, cols = cols_of(c0, width)
                carried = jnp.where(seq_start, 0.0, prev_ref[:, c0:c0 + width])
                prev8 = jnp.where(t > 0, p_cur[g, prev, cols], carried)
                return _silu(_causal_conv(prev8, p_cur[g, rows, cols], cw_ref[:, c0:c0 + width]))

            pairs = []
            for p in range(n_pairs):
                q = _l2_norm_rows(conv(p * hd, hd)) * q_scale
                k = _l2_norm_rows(conv(key_w + p * hd, hd))
                v = conv(2 * key_w + 2 * p * hd, 2 * hd)
                gz, zc = cols_of(4 * key_w + 2 * p * hd, 2 * hd)
                pairs.append((q, k, v, p_cur[gz, rows, zc],
                              gb_cur[p * SUBLANES:(p + 1) * SUBLANES, rows]))
            heads = _gdn_intra(pairs, LANES, hd, masks_ref)
            outs, leaving = _gdn_inter(heads, [[s_ref[h]] for h in range(n_heads)], ng_ref[...],
                                       LANES, hd)
            for h in range(n_heads):
                og_ref[rows, h * hd:(h + 1) * hd] = outs[h].astype(BF16)
                s_ref[h] = leaving[h][0]
            return carry

        lax.fori_loop(0, n_groups, tile, 0)
        for c0 in range(0, 4 * key_w, hd):
            g, cols = cols_of(c0, hd)
            tail = p_cur[g, sb - SUBLANES:sb, cols]
            tail_ref[:, c0:c0 + hd] = tail
            prev_ref[:, c0:c0 + hd] = tail

    parity = lax.rem(step, 2)

    @pl.when(parity == 0)
    def _():
        run(p1_ref, gb1_ref, p0_ref, gb0_ref)

    @pl.when(parity == 1)
    def _():
        run(p0_ref, gb0_ref, p1_ref, gb1_ref)

    @pl.when(block_in_seq == blocks_per_seq - 1)
    def _():
        so_ref[...] = s_ref[...]


def _gdn_sample_body(q_ref, k_ref, v_ref, z_ref, gb_ref, wq_ref, wk_ref, wv_ref, ng_ref, masks_ref,
                     hq_ref, hk_ref, hv_ref, s_ref, og_ref, so_ref, *, block, n_pairs, q_scale, hd):
    segs = _segments(block)

    def conv(ref, hist_ref, w_ref, cols):
        x = ref[:, cols]
        w = w_ref[:, cols]
        return _silu(jnp.concatenate(
            [_causal_conv(hist_ref[s, :, cols], x[r0:r1], w) for s, (r0, r1) in enumerate(segs)],
            axis=0))

    pairs = []
    for p in range(n_pairs):
        c1 = slice(p * hd, (p + 1) * hd)
        c2 = slice(2 * p * hd, 2 * (p + 1) * hd)
        q = _l2_norm_rows(conv(q_ref, hq_ref, wq_ref, c1)) * q_scale
        k = _l2_norm_rows(conv(k_ref, hk_ref, wk_ref, c1))
        v = conv(v_ref, hv_ref, wv_ref, c2)
        pairs.append((q, k, v, z_ref[:, c2], gb_ref[p * SUBLANES:(p + 1) * SUBLANES, :]))
    heads = _gdn_intra(pairs, block, hd, masks_ref)
    states = [[s_ref[s, h] for s in range(len(segs))] for h in range(2 * n_pairs)]
    outs, leaving = _gdn_inter(heads, states, ng_ref[...], block, hd)
    for h in range(2 * n_pairs):
        og_ref[:, h * hd:(h + 1) * hd] = outs[h].astype(BF16)
        for s in range(len(segs)):
            so_ref[s, h] = leaving[h][s]


def _gdn_prompt(x, w_groups, w_ab_t, a_log_rows, dt_bias_rows, conv_w, norm_g, layer, n_layers,
                prev_states, *, n_prompt, batch, hq, hd):
    d = x.shape[1]
    t = n_prompt // batch
    conv_dim = conv_w.shape[1]
    val_w = 2 * hq * hd
    n_rows = w_ab_t.shape[0]
    sb = math.gcd(GDN_SEQ_BLOCK, t)
    n_groups, _, group_w = w_groups.shape
    assert n_groups == sb // LANES and group_w % (2 * hd) == 0
    bps = t // sb
    n_blocks = n_prompt // sb
    masks = _inverse_level_masks(LANES)

    def recurred(i):
        return jnp.maximum(i - 1, 0)

    body, more_specs, more_args, aliases = _carried(
        functools.partial(_gdn_prompt_body, blocks_per_seq=bps, n_pairs=hq, q_scale=hd ** -0.5, hd=hd),
        8, prev_states)
    return pl.pallas_call(
        body,
        grid=(n_blocks + 1,),
        in_specs=[pl.BlockSpec((sb, d), lambda i: (jnp.minimum(i, n_blocks - 1), 0)),
                  _resident(w_groups.shape), _resident(w_ab_t.shape), _resident((n_rows, 1)),
                  _resident((n_rows, 1)), _resident(conv_w.shape), _resident((1, hd)),
                  _resident(masks.shape)] + more_specs,
        out_specs=[pl.BlockSpec((sb, val_w), lambda i: (recurred(i), 0)),
                   pl.BlockSpec((None, None, 2 * hq, hd, hd),
                                lambda i: (recurred(i) // bps, layer, 0, 0, 0)),
                   pl.BlockSpec((SUBLANES, conv_dim), lambda i: (recurred(i), 0))],
        out_shape=[jax.ShapeDtypeStruct((n_prompt, val_w), BF16),
                   jax.ShapeDtypeStruct((batch, n_layers, 2 * hq, hd, hd), F32),
                   jax.ShapeDtypeStruct((n_blocks * SUBLANES, conv_dim), F32)],
        input_output_aliases=aliases,
        scratch_shapes=[pltpu.VMEM((n_groups, sb, group_w), F32), pltpu.VMEM((n_groups, sb, group_w), F32),
                        pltpu.VMEM((n_rows, sb), F32), pltpu.VMEM((n_rows, sb), F32),
                        pltpu.VMEM((sb, d), BF16), pltpu.VMEM((2 * hq, hd, hd), F32),
                        pltpu.VMEM((SUBLANES, conv_dim), F32)],
        compiler_params=_cparams(1),
        name="gdn_prompt",
    )(x, w_groups, w_ab_t, a_log_rows, dt_bias_rows, conv_w, norm_g.reshape(1, hd), masks,
      *more_args)


def _gdn_sample(p, gb, conv_w, norm_g, hist8, state_in, layer, prev_states, *, hq, hd, dec_seq):
    n_sample = p.shape[0]
    key_w = hq * hd
    val_w = 2 * key_w
    q_scale = hd ** -0.5
    gp = math.gcd(GDN_SAMPLE_PAIRS, hq)
    qw, vw = gp * hd, 2 * gp * hd
    k_blk0 = key_w // qw
    ng = norm_g.reshape(1, hd)
    seqs = LANES // dec_seq
    masks_s = _inverse_level_masks(dec_seq)
    state_spec = pl.BlockSpec((seqs, None, 2 * gp, hd, hd), lambda g, h: (g, layer, h, 0, 0))
    body, more_specs, more_args, aliases = _carried(
        functools.partial(_gdn_sample_body, block=dec_seq, n_pairs=gp, q_scale=q_scale, hd=hd), 14,
        prev_states)
    og_s, s_s = pl.pallas_call(
        body,
        grid=(n_sample // LANES, hq // gp),
        in_specs=[pl.BlockSpec((LANES, qw), lambda g, h: (g, h)),
                  pl.BlockSpec((LANES, qw), lambda g, h: (g, k_blk0 + h)),
                  pl.BlockSpec((LANES, vw), lambda g, h: (g, k_blk0 + h)),
                  pl.BlockSpec((LANES, vw), lambda g, h: (g, 2 * k_blk0 + h)),
                  pl.BlockSpec((gp * SUBLANES, LANES), lambda g, h: (h, g)),
                  pl.BlockSpec((CONV_TAPS, qw), lambda g, h: (0, h)),
                  pl.BlockSpec((CONV_TAPS, qw), lambda g, h: (0, k_blk0 + h)),
                  pl.BlockSpec((CONV_TAPS, vw), lambda g, h: (0, k_blk0 + h)),
                  _resident((1, hd)), _resident(masks_s.shape),
                  pl.BlockSpec((seqs, SUBLANES, qw), lambda g, h: (g, 0, h)),
                  pl.BlockSpec((seqs, SUBLANES, qw), lambda g, h: (g, 0, k_blk0 + h)),
                  pl.BlockSpec((seqs, SUBLANES, vw), lambda g, h: (g, 0, k_blk0 + h)),
                  state_spec] + more_specs,
        out_specs=[pl.BlockSpec((LANES, vw), lambda g, h: (g, h)), state_spec],
        out_shape=[jax.ShapeDtypeStruct((n_sample, val_w), BF16),
                   jax.ShapeDtypeStruct(state_in.shape, F32)],
        input_output_aliases=aliases,
        compiler_params=_cparams(2),
        name="gdn_sample",
    )(p, p, p, p, gb, conv_w, conv_w, conv_w, ng, masks_s, hist8, hist8, hist8, state_in,
      *more_args)
    return og_s, s_s


def _column_groups(w, n_groups):
    d, n = w.shape
    assert n % n_groups == 0
    return w.reshape(d, n_groups, n // n_groups).transpose(1, 0, 2).astype(BF16)


def _gdn_gate_rows(w_in, a_log, dt_bias, n_main, hq):
    d = w_in.shape[0]
    hv = a_log.shape[0]
    rep = hv // hq
    pad = SUBLANES - 2 * rep
    wa = w_in[:, n_main:n_main + hv].T.reshape(hq, rep, d)
    wb = w_in[:, n_main + hv:n_main + 2 * hv].T.reshape(hq, rep, d)
    w_ab_t = jnp.concatenate([wa, wb, jnp.zeros((hq, pad, d), w_in.dtype)], axis=1)
    zeros = jnp.zeros((hq, SUBLANES - rep), F32)
    alog = jnp.concatenate([a_log.reshape(hq, rep), zeros], axis=1).reshape(hq * SUBLANES, 1)
    dtb = jnp.concatenate([dt_bias.reshape(hq, rep), zeros], axis=1).reshape(hq * SUBLANES, 1)
    return w_ab_t.reshape(hq * SUBLANES, d).astype(BF16), alog, dtb


def kernel(x_prompt, x_sample, state_gla, state_gdn, state_gdn_conv, ln_g, ln_b, ffn1_w_gu, ffn1_w_d, ffn2_w_gu, ffn2_w_d, gla_w_in, gla_w_a1, gla_w_a2, gla_b_a, gla_norm_g, gla_w_o, gdn_w_in, gdn_conv_w, gdn_a_log, gdn_dt_bias, gdn_norm_g, gdn_w_o):
    batch, t, d = x_prompt.shape
    dec_batch, dec_seq, _ = x_sample.shape
    depth = ln_g.shape[0]
    alpha = (2.0 * depth) ** 0.25
    n_prompt, n_sample = batch * t, dec_batch * dec_seq
    tm = math.gcd(TOKEN_TILE, math.gcd(n_prompt, n_sample))
    assert t % LANES == 0 and t % GLA_CHUNK == 0 and n_sample % LANES == 0 and LANES % dec_seq == 0
    assert tm % LANES == 0

    gla_heads, gla_dk, gla_dv = state_gla.shape[2:]
    hv, hd = state_gdn.shape[2:4]
    conv_dim = gdn_conv_w.shape[2]
    val_w = hv * hd
    hq = (conv_dim - val_w) // (2 * hd)
    assert hv == 2 * hq, "the gated-delta kernel pairs two v heads with each q/k head"
    n_main = conv_dim + val_w

    x = (x_prompt.reshape(n_prompt, d), x_sample.reshape(n_sample, d))
    new_gla_p = jnp.zeros((batch,) + state_gla.shape[1:], F32)
    new_gla_s = jnp.zeros(state_gla.shape, F32)
    new_gdn_p = jnp.zeros((batch,) + state_gdn.shape[1:], F32)
    new_gdn_s = jnp.zeros(state_gdn.shape, F32)
    new_conv_p, new_conv_s = [], []
    for i in range(depth):
        x = _ffn_ln(x, ffn1_w_gu[i].astype(BF16), ffn1_w_d[i].astype(BF16), ln_g[i, 0], ln_b[i, 0],
                    tm=tm, alpha=alpha, n_prompt=n_prompt)
        j = i // 2
        if i % 2 == 0:
            rank = gla_w_a1.shape[2]
            w_a1 = jnp.pad(gla_w_a1[j], ((0, 0), (0, LANES - rank))).astype(BF16)
            w_a2 = jnp.pad(gla_w_a2[j], ((0, LANES - rank), (0, 0))).astype(BF16)
            w_in = _column_groups(gla_w_in[j], math.gcd(GLA_SEQ_BLOCK, t) // LANES)
            og_p, new_gla_p = _gla_prompt(
                x, w_in, w_a1, w_a2, gla_b_a[j], gla_norm_g[j], j, state_gla.shape[1], new_gla_p,
                n_prompt=n_prompt, batch=batch, heads=gla_heads, dk=gla_dk, dv=gla_dv)
            p_s, la_s = _gla_proj(x, w_in, w_a1, w_a2, gla_b_a[j], tm=tm,
                                  first_tile=n_prompt // tm, n_tiles=n_sample // tm)
            og_s, new_gla_s = _gla_sample(p_s, la_s, gla_norm_g[j], state_gla, j, new_gla_s,
                                          heads=gla_heads, dk=gla_dk, dv=gla_dv, dec_seq=dec_seq)
            w_o = gla_w_o[j]
        else:
            w_ab_t, alog_rows, dtb_rows = _gdn_gate_rows(gdn_w_in[j], gdn_a_log[j], gdn_dt_bias[j],
                                                         n_main, hq)
            w_main = _column_groups(gdn_w_in[j][:, :n_main], math.gcd(GDN_SEQ_BLOCK, t) // LANES)
            og_p, new_gdn_p, tails = _gdn_prompt(
                x, w_main, w_ab_t, alog_rows, dtb_rows, gdn_conv_w[j], gdn_norm_g[j], j,
                state_gdn.shape[1], new_gdn_p, n_prompt=n_prompt, batch=batch, hq=hq, hd=hd)
            p_s, gb_s = _gdn_proj(x, w_main, w_ab_t, alog_rows, dtb_rows, tm=tm,
                                  first_tile=n_prompt // tm, n_tiles=n_sample // tm)
            hist8 = jnp.pad(state_gdn_conv[:, j],
                            ((0, 0), (SUBLANES - (CONV_TAPS - 1), 0), (0, 0)))
            og_s, new_gdn_s = _gdn_sample(p_s, gb_s, gdn_conv_w[j], gdn_norm_g[j], hist8, state_gdn, j,
                                          new_gdn_s, hq=hq, hd=hd, dec_seq=dec_seq)
            keep = CONV_TAPS - 1
            new_conv_p.append(tails.reshape(batch, -1, conv_dim)[:, -keep:, :])
            new_conv_s.append(p_s.reshape(dec_batch, dec_seq, n_main)[:, -keep:, :conv_dim])
            w_o = gdn_w_o[j]
        x = _ffn_ln(x, ffn2_w_gu[i].astype(BF16), ffn2_w_d[i].astype(BF16), ln_g[i, 2], ln_b[i, 2],
                    tm=tm, alpha=alpha, n_prompt=n_prompt, split_out=(i == depth - 1),
                    mixer=(og_p, og_s, w_o.astype(BF16), ln_g[i, 1], ln_b[i, 1]))
    y_prompt, y_sample = x
    return (y_prompt.reshape(batch, t, d), y_sample.reshape(dec_batch, dec_seq, d),
            new_gla_p, new_gdn_p, jnp.stack(new_conv_p, 1),
            new_gla_s, new_gdn_s, jnp.stack(new_conv_s, 1))
```

```python
import functools
import math

import numpy as np

import jax
import jax.numpy as jnp
from jax import lax
from jax.experimental import pallas as pl
from jax.experimental.pallas import tpu as pltpu

F32 = jnp.float32
BF16 = jnp.bfloat16

LN_EPS = 1e-5
RMS_EPS = 1e-6
L2_EPS = 1e-6
FFN_RES = 0.5
GLA_GATE_TAU = 16.0
GLA_CHUNK = 64
CONV_TAPS = 4

LANES = 128
SUBLANES = 8
VMEM_LIMIT_BYTES = 60 * 1024 * 1024
TOKEN_TILE = 512
MATMUL_N_CHUNK = 1024
GDN_SEQ_BLOCK = 512
GDN_SAMPLE_PAIRS = 4
GLA_SEQ_BLOCK = 512


def _cparams(n_axes):
    return pltpu.CompilerParams(
        dimension_semantics=("arbitrary",) * n_axes, vmem_limit_bytes=VMEM_LIMIT_BYTES)


def _resident(shape):
    zeros = (0,) * len(shape)
    return pl.BlockSpec(shape, lambda *_: zeros, pipeline_mode=pl.Buffered(1))


def _dot(a, b):
    return jnp.dot(a.astype(BF16), b.astype(BF16), preferred_element_type=F32)


def _dot_nt(a, b):
    return lax.dot_general(a.astype(BF16), b.astype(BF16), (((1,), (1,)), ((), ())),
                           preferred_element_type=F32)


def _silu(x):
    return x * jax.nn.sigmoid(x)


def _softplus(x):
    return jnp.maximum(x, 0.0) + jnp.log1p(jnp.exp(-jnp.abs(x)))


def _layer_norm_rows(y, g, b):
    mu = jnp.mean(y, axis=-1, keepdims=True)
    d = y - mu
    var = jnp.mean(d * d, axis=-1, keepdims=True)
    return d * lax.rsqrt(var + LN_EPS) * g + b


def _rms_norm_rows(o, g):
    return o * lax.rsqrt(jnp.mean(o * o, axis=-1, keepdims=True) + RMS_EPS) * g


def _l2_norm_rows(x):
    return x * lax.rsqrt(jnp.sum(x * x, axis=-1, keepdims=True) + L2_EPS)


def _col_chunks(n, step=MATMUL_N_CHUNK):
    return [(c, min(c + step, n)) for c in range(0, n, step)]


def _ffn_ln_body(*refs, d_ff, alpha, n_in, has_mixer, n_prompt_tiles):
    x_refs, refs = refs[:n_in], refs[n_in:]
    is_prompt = pl.program_id(0) < n_prompt_tiles
    if has_mixer:
        (ogp_ref, ogs_ref, wo_ref, g1_ref, b1_ref), refs = refs[:5], refs[5:]
    wgu_ref, wd_ref, g_ref, b_ref = refs[:4]
    o_refs, h_ref = refs[4:-1], refs[-1]
    x = x_refs[0][...]
    if n_in == 2:
        x = jnp.where(is_prompt, x, x_refs[1][...])
    if has_mixer:
        og = jnp.where(is_prompt, ogp_ref[...], ogs_ref[...])
        x = _layer_norm_rows(alpha * x + jnp.dot(og, wo_ref[...], preferred_element_type=F32),
                             g1_ref[...], b1_ref[...])
    xb = x.astype(BF16)
    for c0, c1 in _col_chunks(d_ff):
        gate = jnp.dot(xb, wgu_ref[:, c0:c1], preferred_element_type=F32)
        up = jnp.dot(xb, wgu_ref[:, d_ff + c0:d_ff + c1], preferred_element_type=F32)
        h_ref[:, c0:c1] = (_silu(gate) * up).astype(BF16)
    if len(o_refs) == 1:
        half = x.shape[0] // 2
        for r in (0, half):
            y = jnp.dot(h_ref[r:r + half, :], wd_ref[...], preferred_element_type=F32)
            o_refs[0][r:r + half, :] = _layer_norm_rows(
                alpha * x[r:r + half] + FFN_RES * y, g_ref[...], b_ref[...])
    else:
        y = jnp.dot(h_ref[...], wd_ref[...], preferred_element_type=F32)
        out = _layer_norm_rows(alpha * x + FFN_RES * y, g_ref[...], b_ref[...])
        o_refs[1][...] = out

        @pl.when(is_prompt)
        def _():
            o_refs[0][...] = out


def _ffn_ln(xs, w_gu, w_d, g, b, *, tm, alpha, n_prompt, mixer=None, split_out=False):
    xs = xs if isinstance(xs, (tuple, list)) else (xs,)
    d = xs[0].shape[1]
    n = sum(x.shape[0] for x in xs)
    d_ff = w_d.shape[0]
    ntp = n_prompt // tm

    def pair(width):
        return [pl.BlockSpec((tm, width), lambda i: (jnp.minimum(i, ntp - 1), 0)),
                pl.BlockSpec((tm, width), lambda i: (jnp.maximum(i - ntp, 0), 0))]

    tile = pl.BlockSpec((tm, d), lambda i: (i, 0))
    vec = _resident((1, d))
    pair_shape = [jax.ShapeDtypeStruct((n_prompt, d), F32), jax.ShapeDtypeStruct((n - n_prompt, d), F32)]
    mixer_specs, mixer_args = [], []
    if mixer is not None:
        og_prompt, og_sample, w_o, g1, b1 = mixer
        mixer_specs = pair(w_o.shape[0]) + [_resident(w_o.shape), vec, vec]
        mixer_args = [og_prompt, og_sample, w_o, g1.reshape(1, d), b1.reshape(1, d)]
    return pl.pallas_call(
        functools.partial(_ffn_ln_body, d_ff=d_ff, alpha=alpha, n_in=len(xs),
                          has_mixer=mixer is not None, n_prompt_tiles=ntp),
        grid=(n // tm,),
        in_specs=(pair(d) if len(xs) == 2 else [tile]) + mixer_specs + [
            _resident(w_gu.shape), _resident(w_d.shape), vec, vec],
        out_specs=pair(d) if split_out else tile,
        out_shape=pair_shape if split_out else jax.ShapeDtypeStruct((n, d), F32),
        scratch_shapes=[pltpu.VMEM((tm, d_ff), BF16)],
        compiler_params=_cparams(1),
        name="ffn_ln",
    )(*xs, *mixer_args, w_gu, w_d, g.reshape(1, d), b.reshape(1, d))


def _gla_log_decay(xb, wa1, wa2, ba):
    low = jnp.dot(xb, wa1, preferred_element_type=F32)
    logits = _dot(low, wa2) + ba
    log_sig = jnp.minimum(logits, 0.0) - jnp.log1p(jnp.exp(-jnp.abs(logits)))
    return log_sig / GLA_GATE_TAU


def _project_groups(xb, wg_ref, p_ref):
    n_groups, _, group_w = wg_ref.shape
    for g in range(n_groups):
        p_ref[:, g * group_w:(g + 1) * group_w] = jnp.dot(xb, wg_ref[g], preferred_element_type=F32)


def _gla_proj_body(x_ref, wg_ref, wa1_ref, wa2_ref, ba_ref, p_ref, la_ref):
    xb = x_ref[...].astype(BF16)
    _project_groups(xb, wg_ref, p_ref)
    la_ref[...] = _gla_log_decay(xb, wa1_ref[...], wa2_ref[...], ba_ref[...])


def _gla_proj(x, w_in, w_a1, w_a2, b_a, *, tm, first_tile, n_tiles):
    d = x.shape[1]
    n = n_tiles * tm
    n_out = w_in.shape[0] * w_in.shape[2]
    qk_w = w_a2.shape[1]
    return pl.pallas_call(
        _gla_proj_body,
        grid=(n_tiles,),
        in_specs=[pl.BlockSpec((tm, d), lambda i: (first_tile + i, 0)), _resident(w_in.shape),
                  _resident(w_a1.shape), _resident(w_a2.shape), _resident((1, qk_w))],
        out_specs=[pl.BlockSpec((tm, n_out), lambda i: (i, 0)),
                   pl.BlockSpec((tm, qk_w), lambda i: (i, 0))],
        out_shape=[jax.ShapeDtypeStruct((n, n_out), F32), jax.ShapeDtypeStruct((n, qk_w), F32)],
        compiler_params=_cparams(1),
        name="gla_proj",
    )(x, w_in, w_a1, w_a2, b_a.reshape(1, qk_w))


def _gdn_gate_values(xb, wab, alog, dtb):
    ab = lax.dot_general(wab, xb, (((1,), (1,)), ((), ())), preferred_element_type=F32)
    kind = _iota2(ab.shape, 0) & (SUBLANES - 1)
    g = -jnp.exp(alog) * _softplus(ab + dtb)
    beta = jax.nn.sigmoid(ab)
    return jnp.where(kind < 2, g, jnp.where(kind < 4, beta, 0.0))


def _gdn_proj_body(x_ref, wg_ref, wab_ref, alog_ref, dtb_ref, p_ref, gb_ref):
    xb = x_ref[...].astype(BF16)
    _project_groups(xb, wg_ref, p_ref)
    gb_ref[...] = _gdn_gate_values(xb, wab_ref[...], alog_ref[...], dtb_ref[...])


def _gdn_proj(x, w_main, w_ab_t, a_log_rows, dt_bias_rows, *, tm, first_tile, n_tiles):
    d = x.shape[1]
    n = n_tiles * tm
    n_out = w_main.shape[0] * w_main.shape[2]
    n_rows = w_ab_t.shape[0]
    return pl.pallas_call(
        _gdn_proj_body,
        grid=(n_tiles,),
        in_specs=[pl.BlockSpec((tm, d), lambda i: (first_tile + i, 0)), _resident(w_main.shape),
                  _resident(w_ab_t.shape), _resident((n_rows, 1)), _resident((n_rows, 1))],
        out_specs=[pl.BlockSpec((tm, n_out), lambda i: (i, 0)),
                   pl.BlockSpec((n_rows, tm), lambda i: (0, i))],
        out_shape=[jax.ShapeDtypeStruct((n, n_out), F32), jax.ShapeDtypeStruct((n_rows, n), F32)],
        compiler_params=_cparams(1),
        name="gdn_proj",
    )(x, w_main, w_ab_t, a_log_rows, dt_bias_rows)


def _iota2(shape, dim):
    return lax.broadcasted_iota(jnp.int32, shape, dim)


def _block_of(idx, block):
    return idx >> (block.bit_length() - 1)


def _block_cumsum(x, block, axis):
    pos = _iota2(x.shape, axis) & (block - 1)
    shift = 1
    while shift < block:
        x = x + jnp.where(pos >= shift, pltpu.roll(x, shift, axis), 0.0)
        shift *= 2
    return x


def _col_form(row):
    return jnp.transpose(jnp.broadcast_to(row, (LANES, LANES)))


def _segments(block):
    return [(r, r + block) for r in range(0, LANES, block)]


def _last_row_of_segments(x, block):
    return jnp.concatenate(
        [jnp.broadcast_to(x[r1 - 1:r1, :], (r1 - r0, x.shape[1])) for r0, r1 in _segments(block)],
        axis=0)


def _rows_only(x, r0, r1):
    rows = _iota2(x.shape, 0)
    return jnp.where((rows >= r0) & (rows < r1), x, 0.0)


def _gla_tiles(heads, ng, block, chained):
    row = _iota2((LANES, LANES), 0)
    col = _iota2((LANES, LANES), 1)
    causal = (_block_of(row, block) == _block_of(col, block)) & (row >= col)
    segs = _segments(block)
    bcums = [_block_cumsum(la, block, 0) for (_, _, _, la, _, _) in heads]
    q_decs = [q * jnp.exp(b) for (q, _, _, _, _, _), b in zip(heads, bcums)]
    atts = [jnp.where(causal, _dot_nt(qd, k * jnp.exp(-b)), 0.0)
            for qd, (_, k, _, _, _, _), b in zip(q_decs, heads, bcums)]
    intras = [_dot(att, v) for att, (_, _, v, _, _, _) in zip(atts, heads)]
    k_end_ts = [jnp.transpose(k * jnp.exp(_last_row_of_segments(b, block) - b))
                for (_, k, _, _, _, _), b in zip(heads, bcums)]
    lane_reps = heads[0][2].shape[1] // LANES
    states = [list(st) for (_, _, _, _, _, st) in heads]
    cur = [st[0] for st in states]
    outs = [[] for _ in heads]
    leaving = [[] for _ in heads]
    for s, (r0, r1) in enumerate(segs):
        if not chained:
            cur = [st[s] for st in states]
        inter = [_dot(qd[r0:r1], c) for qd, c in zip(q_decs, cur)]
        upd = [_dot(kt, _rows_only(v, r0, r1)) for kt, (_, _, v, _, _, _) in zip(k_end_ts, heads)]
        for i, b in enumerate(bcums):
            outs[i].append(intras[i][r0:r1] + inter[i])
            decay = jnp.concatenate([_col_form(jnp.exp(b[r1 - 1:r1, :]))] * lane_reps, axis=1)
            cur[i] = decay * cur[i] + upd[i]
            if not chained:
                leaving[i].append(cur[i])
    results = []
    for i, (_, _, _, _, r, _) in enumerate(heads):
        o = _rms_norm_rows(jnp.concatenate(outs[i], axis=0), ng)
        results.append((o * _silu(r), leaving[i] if not chained else [cur[i]]))
    return results


def _gla_prompt_body(x_ref, wg_ref, wa1_ref, wa2_ref, ba_ref, ng_ref, og_ref, so_ref,
                     p0_ref, p1_ref, la0_ref, la1_ref, xb_ref, s_ref, *,
                     blocks_per_seq, n_heads, dk, dv, q_scale):
    step = pl.program_id(0)
    n_groups, _, group_w = p0_ref.shape
    qk_w = n_heads * dk

    @pl.when(step == 0)
    def _():
        p1_ref[...] = jnp.zeros_like(p1_ref)
        la1_ref[...] = jnp.zeros_like(la1_ref)
        s_ref[...] = jnp.zeros_like(s_ref)

    block_in_seq = lax.rem(jnp.maximum(step - 1, 0), blocks_per_seq)
    seq_start = block_in_seq == 0

    @pl.when(seq_start)
    def _():
        s_ref[...] = jnp.zeros_like(s_ref)

    xb_ref[...] = x_ref[...].astype(BF16)

    def cols_of(c0, width):
        g, off = divmod(c0, group_w)
        assert off + width <= group_w
        return g, slice(off, off + width)

    def run(p_cur, la_cur, p_nxt, la_nxt):
        la_nxt[...] = _gla_log_decay(xb_ref[...], wa1_ref[...], wa2_ref[...], ba_ref[...])

        def tile(t, carry):
            rows = pl.ds(pl.multiple_of(t * LANES, LANES), LANES)
            p_nxt[t] = jnp.dot(xb_ref[...], wg_ref[t], preferred_element_type=F32)
            heads = []
            for h in range(n_heads):
                gq, cq = cols_of(h * dk, dk)
                gk, ck = cols_of(qk_w + h * dk, dk)
                gv, cv = cols_of(2 * qk_w + h * dv, dv)
                gr, cr = cols_of(2 * qk_w + n_heads * dv + h * dv, dv)
                heads.append((p_cur[gq, rows, cq] * q_scale, p_cur[gk, rows, ck],
                              p_cur[gv, rows, cv], la_cur[rows, h * dk:(h + 1) * dk],
                              p_cur[gr, rows, cr], [s_ref[h]]))
            for h, (o, leaving) in enumerate(_gla_tiles(heads, ng_ref[...], GLA_CHUNK, True)):
                og_ref[rows, h * dv:(h + 1) * dv] = o.astype(BF16)
                s_ref[h] = leaving[0]
            return carry

        lax.fori_loop(0, n_groups, tile, 0)

    parity = lax.rem(step, 2)

    @pl.when(parity == 0)
    def _():
        run(p1_ref, la1_ref, p0_ref, la0_ref)

    @pl.when(parity == 1)
    def _():
        run(p0_ref, la0_ref, p1_ref, la1_ref)

    @pl.when(block_in_seq == blocks_per_seq - 1)
    def _():
        so_ref[...] = s_ref[...]


def _gla_sample_body(q_ref, k_ref, v_ref, r_ref, la_ref, ng_ref, s_ref, og_ref, so_ref, *, block,
                     n_heads, dk, dv, q_scale):
    n_seq = LANES // block
    heads = []
    for h in range(n_heads):
        ck = slice(h * dk, (h + 1) * dk)
        cv = slice(h * dv, (h + 1) * dv)
        heads.append((q_ref[:, ck] * q_scale, k_ref[:, ck], v_ref[:, cv], la_ref[:, ck], r_ref[:, cv],
                      [s_ref[i, h] for i in range(n_seq)]))
    for h, (o, leaving) in enumerate(_gla_tiles(heads, ng_ref[...], block, False)):
        og_ref[:, h * dv:(h + 1) * dv] = o.astype(BF16)
        for i in range(n_seq):
            so_ref[i, h] = leaving[i]


def _carried(body, n_inputs, prev):
    def skipping_carried_input(*refs):
        return body(*refs[:n_inputs], *refs[n_inputs + 1:])

    return skipping_carried_input, [pl.BlockSpec(memory_space=pl.ANY)], [prev], {n_inputs: 1}


def _gla_prompt(x, w_groups, w_a1, w_a2, b_a, norm_g, layer, n_layers, prev_states, *, n_prompt,
                batch, heads, dk, dv):
    d = x.shape[1]
    t = n_prompt // batch
    qk_w, v_w = heads * dk, heads * dv
    sb = math.gcd(GLA_SEQ_BLOCK, t)
    n_groups, _, group_w = w_groups.shape
    assert n_groups == sb // LANES and group_w % dv == 0
    bps = t // sb
    n_blocks = n_prompt // sb

    def recurred(i):
        return jnp.maximum(i - 1, 0)

    body, more_specs, more_args, aliases = _carried(
        functools.partial(_gla_prompt_body, blocks_per_seq=bps, n_heads=heads, dk=dk, dv=dv,
                          q_scale=dk ** -0.5), 6, prev_states)
    return pl.pallas_call(
        body,
        grid=(n_blocks + 1,),
        in_specs=[pl.BlockSpec((sb, d), lambda i: (jnp.minimum(i, n_blocks - 1), 0)),
                  _resident(w_groups.shape), _resident(w_a1.shape), _resident(w_a2.shape),
                  _resident((1, qk_w)), _resident((1, dv))] + more_specs,
        out_specs=[pl.BlockSpec((sb, v_w), lambda i: (recurred(i), 0)),
                   pl.BlockSpec((None, None, heads, dk, dv),
                                lambda i: (recurred(i) // bps, layer, 0, 0, 0))],
        out_shape=[jax.ShapeDtypeStruct((n_prompt, v_w), BF16),
                   jax.ShapeDtypeStruct((batch, n_layers, heads, dk, dv), F32)],
        scratch_shapes=[pltpu.VMEM((n_groups, sb, group_w), F32), pltpu.VMEM((n_groups, sb, group_w), F32),
                        pltpu.VMEM((sb, qk_w), F32), pltpu.VMEM((sb, qk_w), F32),
                        pltpu.VMEM((sb, d), BF16), pltpu.VMEM((heads, dk, dv), F32)],
        input_output_aliases=aliases,
        compiler_params=_cparams(1),
        name="gla_prompt",
    )(x, w_groups, w_a1, w_a2, b_a.reshape(1, qk_w), norm_g.reshape(1, dv), *more_args)


def _gla_sample(p, la, norm_g, state_in, layer, prev_states, *, heads, dk, dv, dec_seq):
    n_sample = p.shape[0]
    qk_w, v_w = heads * dk, heads * dv
    ng = norm_g.reshape(1, dv)
    seqs = LANES // dec_seq
    state_spec = pl.BlockSpec((seqs, None, heads, dk, dv), lambda g: (g, layer, 0, 0, 0))
    body, more_specs, more_args, aliases = _carried(
        functools.partial(_gla_sample_body, block=dec_seq, n_heads=heads, dk=dk, dv=dv,
                          q_scale=dk ** -0.5), 7, prev_states)
    og_s, s_s = pl.pallas_call(
        body,
        grid=(n_sample // LANES,),
        in_specs=[pl.BlockSpec((LANES, qk_w), lambda g: (g, 0)),
                  pl.BlockSpec((LANES, qk_w), lambda g: (g, 1)),
                  pl.BlockSpec((LANES, v_w), lambda g: (g, (2 * qk_w) // v_w)),
                  pl.BlockSpec((LANES, v_w), lambda g: (g, (2 * qk_w + v_w) // v_w)),
                  pl.BlockSpec((LANES, qk_w), lambda g: (g, 0)),
                  _resident((1, dv)), state_spec] + more_specs,
        out_specs=[pl.BlockSpec((LANES, v_w), lambda g: (g, 0)), state_spec],
        out_shape=[jax.ShapeDtypeStruct((n_sample, v_w), BF16),
                   jax.ShapeDtypeStruct(state_in.shape, F32)],
        input_output_aliases=aliases,
        compiler_params=_cparams(1),
        name="gla_sample",
    )(p, p, p, p, la, ng, state_in, *more_args)
    return og_s, s_s


def _causal_conv(prev8, x, w):
    n = x.shape[0]
    xs = jnp.concatenate([prev8, x], axis=0)
    acc = xs[SUBLANES:, :] * w[CONV_TAPS - 1:CONV_TAPS, :]
    for j in range(CONV_TAPS - 1):
        shifted = pltpu.roll(xs, CONV_TAPS - 1 - j, 0)
        acc = acc + shifted[SUBLANES:SUBLANES + n, :] * w[j:j + 1, :]
    return acc


def _inverse_level_masks(block):
    idx = np.arange(LANES)
    masks = []
    size = 1
    while size < block:
        pair = (idx[:, None] // (2 * size)) == (idx[None, :] // (2 * size))
        masks.append(pair & ((idx[:, None] // size) != (idx[None, :] // size)))
        size *= 2
    return jnp.asarray(np.stack(masks).astype(np.float32))


def _unit_lower_inverses(mats, masks_ref):
    row = _iota2((LANES, LANES), 0)
    col = _iota2((LANES, LANES), 1)
    eye = jnp.where(row == col, 1.0, 0.0)
    xs = [eye - a * masks_ref[0] for a in mats]
    mats_b = [a.astype(BF16) for a in mats]
    for level in range(1, masks_ref.shape[0]):
        mask = masks_ref[level]
        xbs = [x.astype(BF16) for x in xs]
        ys = [jnp.dot(xb, ab, preferred_element_type=F32) for xb, ab in zip(xbs, mats_b)]
        xs = [x - jnp.dot(y.astype(BF16), xb, preferred_element_type=F32) * mask
              for x, y, xb in zip(xs, ys, xbs)]
    return xs


def _gdn_intra(pairs, block, hd, masks_ref):
    row = _iota2((LANES, LANES), 0)
    col = _iota2((LANES, LANES), 1)
    same = _block_of(row, block) == _block_of(col, block)
    incl = same & (row >= col)
    strict = same & (row > col)
    kk_qks = [_dot_nt(jnp.concatenate([k, q], axis=0), k) for (q, k, _, _, _) in pairs]
    heads = []
    for (q, k, v, z, gb), kk_qk in zip(pairs, kk_qks):
        kk, qk = kk_qk[:LANES], kk_qk[LANES:]
        gcum = _block_cumsum(gb, block, 1)
        for e in range(2):
            g_row = gcum[e:e + 1, :]
            g_col = _col_form(g_row)
            b_col = _col_form(gb[2 + e:3 + e, :])
            decay = jnp.where(incl, jnp.exp(g_col - jnp.broadcast_to(g_row, (LANES, LANES))), 0.0)
            heads.append(dict(
                q=q, k=k, v=v[:, e * hd:(e + 1) * hd], z=z[:, e * hd:(e + 1) * hd], qk=qk,
                decay=decay, g_col=g_col, b_col=b_col,
                a=jnp.where(strict, b_col * kk * decay, 0.0)))
    t_invs = _unit_lower_inverses([h["a"] for h in heads], masks_ref)
    for h, t_inv in zip(heads, t_invs):
        eg = jnp.exp(h["g_col"])
        rhs = jnp.concatenate([h["k"] * (h["b_col"] * eg), h["v"] * h["b_col"]], axis=1)
        h["wu"] = _dot(t_inv, rhs)
        h["q_dec"] = h["q"] * eg
        h["att"] = h["qk"] * h["decay"]
    for h in heads:
        k_end = h["k"] * jnp.exp(_last_row_of_segments(h["g_col"], block) - h["g_col"])
        h["k_end_t"] = jnp.transpose(k_end)
    return heads


def _gdn_inter(heads, states, ng, block, hd):
    segs = _segments(block)
    ws_qs = [[_dot(jnp.concatenate([h["wu"][r0:r1, :hd], h["q_dec"][r0:r1]], axis=0), st[s])
              for s, (r0, r1) in enumerate(segs)] for h, st in zip(heads, states)]
    v_news = []
    for h, per_seg in zip(heads, ws_qs):
        parts = [h["wu"][r0:r1, hd:] - x[:r1 - r0] for (r0, r1), x in zip(segs, per_seg)]
        v_news.append(jnp.concatenate(parts, axis=0) if len(parts) > 1 else parts[0])
    if len(segs) == 1:
        both = [_dot(jnp.concatenate([h["att"], h["k_end_t"]], axis=0), v_new)
                for h, v_new in zip(heads, v_news)]
        o_intra = [x[:LANES] for x in both]
        updates = [[x[LANES:]] for x in both]
    else:
        o_intra = [_dot(h["att"], v_new) for h, v_new in zip(heads, v_news)]
        updates = [[_dot(h["k_end_t"], _rows_only(v_new, r0, r1)) for (r0, r1) in segs]
                   for h, v_new in zip(heads, v_news)]
    outs, leaving = [], []
    for h, st, per_seg, oi, upd in zip(heads, states, ws_qs, o_intra, updates):
        qs = [x[r1 - r0:] for (r0, r1), x in zip(segs, per_seg)]
        o = oi + (jnp.concatenate(qs, axis=0) if len(qs) > 1 else qs[0])
        outs.append(_rms_norm_rows(o, ng) * _silu(h["z"]))
        leaving.append([jnp.exp(h["g_col"][r1 - 1:r1, :]) * st[s] + upd[s]
                        for s, (r0, r1) in enumerate(segs)])
    return outs, leaving


def _gdn_prompt_body(x_ref, wg_ref, wab_ref, alog_ref, dtb_ref, cw_ref, ng_ref, masks_ref,
                     og_ref, so_ref, tail_ref, p0_ref, p1_ref, gb0_ref, gb1_ref, xb_ref, s_ref, prev_ref,
                     *, blocks_per_seq, n_pairs, q_scale, hd):
    step = pl.program_id(0)
    n_groups, sb, group_w = p0_ref.shape
    n_heads = 2 * n_pairs
    key_w = n_pairs * hd

    @pl.when(step == 0)
    def _():
        p1_ref[...] = jnp.zeros_like(p1_ref)
        gb1_ref[...] = jnp.zeros_like(gb1_ref)
        s_ref[...] = jnp.zeros_like(s_ref)
        prev_ref[...] = jnp.zeros_like(prev_ref)

    block_in_seq = lax.rem(jnp.maximum(step - 1, 0), blocks_per_seq)
    seq_start = block_in_seq == 0

    @pl.when(seq_start)
    def _():
        s_ref[...] = jnp.zeros_like(s_ref)

    xb_ref[...] = x_ref[...].astype(BF16)

    def cols_of(c0, width):
        g, off = divmod(c0, group_w)
        assert off + width <= group_w
        return g, slice(off, off + width)

    def run(p_cur, gb_cur, p_nxt, gb_nxt):
        gb_nxt[...] = _gdn_gate_values(xb_ref[...], wab_ref[...], alog_ref[...], dtb_ref[...])

        def tile(t, carry):
            r0 = pl.multiple_of(t * LANES, LANES)
            rows = pl.ds(r0, LANES)
            p_nxt[t] = jnp.dot(xb_ref[...], wg_ref[t], preferred_element_type=F32)
            prev = pl.ds(pl.multiple_of(jnp.maximum(r0 - SUBLANES, 0), SUBLANES), SUBLANES)

            def conv(c0, width):
                g, cols = cols_of(c0, width)
                carried = jnp.where(seq_start, 0.0, prev_ref[:, c0:c0 + width])
                prev8 = jnp.where(t > 0, p_cur[g, prev, cols], carried)
                return _silu(_causal_conv(prev8, p_cur[g, rows, cols], cw_ref[:, c0:c0 + width]))

            pairs = []
            for p in range(n_pairs):
                q = _l2_norm_rows(conv(p * hd, hd)) * q_scale
                k = _l2_norm_rows(conv(key_w + p * hd, hd))
                v = conv(2 * key_w + 2 * p * hd, 2 * hd)
                gz, zc = cols_of(4 * key_w + 2 * p * hd, 2 * hd)
                pairs.append((q, k, v, p_cur[gz, rows, zc],
                              gb_cur[p * SUBLANES:(p + 1) * SUBLANES, rows]))
            heads = _gdn_intra(pairs, LANES, hd, masks_ref)
            outs, leaving = _gdn_inter(heads, [[s_ref[h]] for h in range(n_heads)], ng_ref[...],
                                       LANES, hd)
            for h in range(n_heads):
                og_ref[rows, h * hd:(h + 1) * hd] = outs[h].astype(BF16)
                s_ref[h] = leaving[h][0]
            return carry

        lax.fori_loop(0, n_groups, tile, 0)
        for c0 in range(0, 4 * key_w, hd):
            g, cols = cols_of(c0, hd)
            tail = p_cur[g, sb - SUBLANES:sb, cols]
            tail_ref[:, c0:c0 + hd] = tail
            prev_ref[:, c0:c0 + hd] = tail

    parity = lax.rem(step, 2)

    @pl.when(parity == 0)
    def _():
        run(p1_ref, gb1_ref, p0_ref, gb0_ref)

    @pl.when(parity == 1)
    def _():
        run(p0_ref, gb0_ref, p1_ref, gb1_ref)

    @pl.when(block_in_seq == blocks_per_seq - 1)
    def _():
        so_ref[...] = s_ref[...]


def _gdn_sample_body(q_ref, k_ref, v_ref, z_ref, gb_ref, wq_ref, wk_ref, wv_ref, ng_ref, masks_ref,
                     hq_ref, hk_ref, hv_ref, s_ref, og_ref, so_ref, *, block, n_pairs, q_scale, hd):
    segs = _segments(block)

    def conv(ref, hist_ref, w_ref, cols):
        x = ref[:, cols]
        w = w_ref[:, cols]
        return _silu(jnp.concatenate(
            [_causal_conv(hist_ref[s, :, cols], x[r0:r1], w) for s, (r0, r1) in enumerate(segs)],
            axis=0))

    pairs = []
    for p in range(n_pairs):
        c1 = slice(p * hd, (p + 1) * hd)
        c2 = slice(2 * p * hd, 2 * (p + 1) * hd)
        q = _l2_norm_rows(conv(q_ref, hq_ref, wq_ref, c1)) * q_scale
        k = _l2_norm_rows(conv(k_ref, hk_ref, wk_ref, c1))
        v = conv(v_ref, hv_ref, wv_ref, c2)
        pairs.append((q, k, v, z_ref[:, c2], gb_ref[p * SUBLANES:(p + 1) * SUBLANES, :]))
    heads = _gdn_intra(pairs, block, hd, masks_ref)
    states = [[s_ref[s, h] for s in range(len(segs))] for h in range(2 * n_pairs)]
    outs, leaving = _gdn_inter(heads, states, ng_ref[...], block, hd)
    for h in range(2 * n_pairs):
        og_ref[:, h * hd:(h + 1) * hd] = outs[h].astype(BF16)
        for s in range(len(segs)):
            so_ref[s, h] = leaving[h][s]


def _gdn_prompt(x, w_groups, w_ab_t, a_log_rows, dt_bias_rows, conv_w, norm_g, layer, n_layers,
                prev_states, *, n_prompt, batch, hq, hd):
    d = x.shape[1]
    t = n_prompt // batch
    conv_dim = conv_w.shape[1]
    val_w = 2 * hq * hd
    n_rows = w_ab_t.shape[0]
    sb = math.gcd(GDN_SEQ_BLOCK, t)
    n_groups, _, group_w = w_groups.shape
    assert n_groups == sb // LANES and group_w % (2 * hd) == 0
    bps = t // sb
    n_blocks = n_prompt // sb
    masks = _inverse_level_masks(LANES)

    def recurred(i):
        return jnp.maximum(i - 1, 0)

    body, more_specs, more_args, aliases = _carried(
        functools.partial(_gdn_prompt_body, blocks_per_seq=bps, n_pairs=hq, q_scale=hd ** -0.5, hd=hd),
        8, prev_states)
    return pl.pallas_call(
        body,
        grid=(n_blocks + 1,),
        in_specs=[pl.BlockSpec((sb, d), lambda i: (jnp.minimum(i, n_blocks - 1), 0)),
                  _resident(w_groups.shape), _resident(w_ab_t.shape), _resident((n_rows, 1)),
                  _resident((n_rows, 1)), _resident(conv_w.shape), _resident((1, hd)),
                  _resident(masks.shape)] + more_specs,
        out_specs=[pl.BlockSpec((sb, val_w), lambda i: (recurred(i), 0)),
                   pl.BlockSpec((None, None, 2 * hq, hd, hd),
                                lambda i: (recurred(i) // bps, layer, 0, 0, 0)),
                   pl.BlockSpec((SUBLANES, conv_dim), lambda i: (recurred(i), 0))],
        out_shape=[jax.ShapeDtypeStruct((n_prompt, val_w), BF16),
                   jax.ShapeDtypeStruct((batch, n_layers, 2 * hq, hd, hd), F32),
                   jax.ShapeDtypeStruct((n_blocks * SUBLANES, conv_dim), F32)],
        input_output_aliases=aliases,
        scratch_shapes=[pltpu.VMEM((n_groups, sb, group_w), F32), pltpu.VMEM((n_groups, sb, group_w), F32),
                        pltpu.VMEM((n_rows, sb), F32), pltpu.VMEM((n_rows, sb), F32),
                        pltpu.VMEM((sb, d), BF16), pltpu.VMEM((2 * hq, hd, hd), F32),
                        pltpu.VMEM((SUBLANES, conv_dim), F32)],
        compiler_params=_cparams(1),
        name="gdn_prompt",
    )(x, w_groups, w_ab_t, a_log_rows, dt_bias_rows, conv_w, norm_g.reshape(1, hd), masks,
      *more_args)


def _gdn_sample(p, gb, conv_w, norm_g, hist8, state_in, layer, prev_states, *, hq, hd, dec_seq):
    n_sample = p.shape[0]
    key_w = hq * hd
    val_w = 2 * key_w
    q_scale = hd ** -0.5
    gp = math.gcd(GDN_SAMPLE_PAIRS, hq)
    qw, vw = gp * hd, 2 * gp * hd
    k_blk0 = key_w // qw
    ng = norm_g.reshape(1, hd)
    seqs = LANES // dec_seq
    masks_s = _inverse_level_masks(dec_seq)
    state_spec = pl.BlockSpec((seqs, None, 2 * gp, hd, hd), lambda g, h: (g, layer, h, 0, 0))
    body, more_specs, more_args, aliases = _carried(
        functools.partial(_gdn_sample_body, block=dec_seq, n_pairs=gp, q_scale=q_scale, hd=hd), 14,
        prev_states)
    og_s, s_s = pl.pallas_call(
        body,
        grid=(n_sample // LANES, hq // gp),
        in_specs=[pl.BlockSpec((LANES, qw), lambda g, h: (g, h)),
                  pl.BlockSpec((LANES, qw), lambda g, h: (g, k_blk0 + h)),
                  pl.BlockSpec((LANES, vw), lambda g, h: (g, k_blk0 + h)),
                  pl.BlockSpec((LANES, vw), lambda g, h: (g, 2 * k_blk0 + h)),
                  pl.BlockSpec((gp * SUBLANES, LANES), lambda g, h: (h, g)),
                  pl.BlockSpec((CONV_TAPS, qw), lambda g, h: (0, h)),
                  pl.BlockSpec((CONV_TAPS, qw), lambda g, h: (0, k_blk0 + h)),
                  pl.BlockSpec((CONV_TAPS, vw), lambda g, h: (0, k_blk0 + h)),
                  _resident((1, hd)), _resident(masks_s.shape),
                  pl.BlockSpec((seqs, SUBLANES, qw), lambda g, h: (g, 0, h)),
                  pl.BlockSpec((seqs, SUBLANES, qw), lambda g, h: (g, 0, k_blk0 + h)),
                  pl.BlockSpec((seqs, SUBLANES, vw), lambda g, h: (g, 0, k_blk0 + h)),
                  state_spec] + more_specs,
        out_specs=[pl.BlockSpec((LANES, vw), lambda g, h: (g, h)), state_spec],
        out_shape=[jax.ShapeDtypeStruct((n_sample, val_w), BF16),
                   jax.ShapeDtypeStruct(state_in.shape, F32)],
        input_output_aliases=aliases,
        compiler_params=_cparams(2),
        name="gdn_sample",
    )(p, p, p, p, gb, conv_w, conv_w, conv_w, ng, masks_s, hist8, hist8, hist8, state_in,
      *more_args)
    return og_s, s_s


def _column_groups(w, n_groups):
    d, n = w.shape
    assert n % n_groups == 0
    return w.reshape(d, n_groups, n // n_groups).transpose(1, 0, 2).astype(BF16)


def _gdn_gate_rows(w_in, a_log, dt_bias, n_main, hq):
    d = w_in.shape[0]
    hv = a_log.shape[0]
    rep = hv // hq
    pad = SUBLANES - 2 * rep
    wa = w_in[:, n_main:n_main + hv].T.reshape(hq, rep, d)
    wb = w_in[:, n_main + hv:n_main + 2 * hv].T.reshape(hq, rep, d)
    w_ab_t = jnp.concatenate([wa, wb, jnp.zeros((hq, pad, d), w_in.dtype)], axis=1)
    zeros = jnp.zeros((hq, SUBLANES - rep), F32)
    alog = jnp.concatenate([a_log.reshape(hq, rep), zeros], axis=1).reshape(hq * SUBLANES, 1)
    dtb = jnp.concatenate([dt_bias.reshape(hq, rep), zeros], axis=1).reshape(hq * SUBLANES, 1)
    return w_ab_t.reshape(hq * SUBLANES, d).astype(BF16), alog, dtb


def kernel(x_prompt, x_sample, state_gla, state_gdn, state_gdn_conv, ln_g, ln_b, ffn1_w_gu, ffn1_w_d, ffn2_w_gu, ffn2_w_d, gla_w_in, gla_w_a1, gla_w_a2, gla_b_a, gla_norm_g, gla_w_o, gdn_w_in, gdn_conv_w, gdn_a_log, gdn_dt_bias, gdn_norm_g, gdn_w_o):
    batch, t, d = x_prompt.shape
    dec_batch, dec_seq, _ = x_sample.shape
    depth = ln_g.shape[0]
    alpha = (2.0 * depth) ** 0.25
    n_prompt, n_sample = batch * t, dec_batch * dec_seq
    tm = math.gcd(TOKEN_TILE, math.gcd(n_prompt, n_sample))
    assert t % LANES == 0 and t % GLA_CHUNK == 0 and n_sample % LANES == 0 and LANES % dec_seq == 0
    assert tm % LANES == 0

    gla_heads, gla_dk, gla_dv = state_gla.shape[2:]
    hv, hd = state_gdn.shape[2:4]
    conv_dim = gdn_conv_w.shape[2]
    val_w = hv * hd
    hq = (conv_dim - val_w) // (2 * hd)
    assert hv == 2 * hq, "the gated-delta kernel pairs two v heads with each q/k head"
    n_main = conv_dim + val_w

    x = (x_prompt.reshape(n_prompt, d), x_sample.reshape(n_sample, d))
    new_gla_p = jnp.zeros((batch,) + state_gla.shape[1:], F32)
    new_gla_s = jnp.zeros(state_gla.shape, F32)
    new_gdn_p = jnp.zeros((batch,) + state_gdn.shape[1:], F32)
    new_gdn_s = jnp.zeros(state_gdn.shape, F32)
    new_conv_p, new_conv_s = [], []
    for i in range(depth):
        x = _ffn_ln(x, ffn1_w_gu[i].astype(BF16), ffn1_w_d[i].astype(BF16), ln_g[i, 0], ln_b[i, 0],
                    tm=tm, alpha=alpha, n_prompt=n_prompt)
        j = i // 2
        if i % 2 == 0:
            rank = gla_w_a1.shape[2]
            w_a1 = jnp.pad(gla_w_a1[j], ((0, 0), (0, LANES - rank))).astype(BF16)
            w_a2 = jnp.pad(gla_w_a2[j], ((0, LANES - rank), (0, 0))).astype(BF16)
            w_in = _column_groups(gla_w_in[j], math.gcd(GLA_SEQ_BLOCK, t) // LANES)
            og_p, new_gla_p = _gla_prompt(
                x, w_in, w_a1, w_a2, gla_b_a[j], gla_norm_g[j], j, state_gla.shape[1], new_gla_p,
                n_prompt=n_prompt, batch=batch, heads=gla_heads, dk=gla_dk, dv=gla_dv)
            p_s, la_s = _gla_proj(x, w_in, w_a1, w_a2, gla_b_a[j], tm=tm,
                                  first_tile=n_prompt // tm, n_tiles=n_sample // tm)
            og_s, new_gla_s = _gla_sample(p_s, la_s, gla_norm_g[j], state_gla, j, new_gla_s,
                                          heads=gla_heads, dk=gla_dk, dv=gla_dv, dec_seq=dec_seq)
            w_o = gla_w_o[j]
        else:
            w_ab_t, alog_rows, dtb_rows = _gdn_gate_rows(gdn_w_in[j], gdn_a_log[j], gdn_dt_bias[j],
                                                         n_main, hq)
            w_main = _column_groups(gdn_w_in[j][:, :n_main], math.gcd(GDN_SEQ_BLOCK, t) // LANES)
            og_p, new_gdn_p, tails = _gdn_prompt(
                x, w_main, w_ab_t, alog_rows, dtb_rows, gdn_conv_w[j], gdn_norm_g[j], j,
                state_gdn.shape[1], new_gdn_p, n_prompt=n_prompt, batch=batch, hq=hq, hd=hd)
            p_s, gb_s = _gdn_proj(x, w_main, w_ab_t, alog_rows, dtb_rows, tm=tm,
                                  first_tile=n_prompt // tm, n_tiles=n_sample // tm)
            hist8 = jnp.pad(state_gdn_conv[:, j],
                            ((0, 0), (SUBLANES - (CONV_TAPS - 1), 0), (0, 0)))
            og_s, new_gdn_s = _gdn_sample(p_s, gb_s, gdn_conv_w[j], gdn_norm_g[j], hist8, state_gdn, j,
                                          new_gdn_s, hq=hq, hd=hd, dec_seq=dec_seq)
            keep = CONV_TAPS - 1
            new_conv_p.append(tails.reshape(batch, -1, conv_dim)[:, -keep:, :])
            new_conv_s.append(p_s.reshape(dec_batch, dec_seq, n_main)[:, -keep:, :conv_dim])
            w_o = gdn_w_o[j]
        x = _ffn_ln(x, ffn2_w_gu[i].astype(BF16), ffn2_w_d[i].astype(BF16), ln_g[i, 2], ln_b[i, 2],
                    tm=tm, alpha=alpha, n_prompt=n_prompt, split_out=(i == depth - 1),
                    mixer=(og_p, og_s, w_o.astype(BF16), ln_g[i, 1], ln_b[i, 1]))
    y_prompt, y_sample = x
    return (y_prompt.reshape(batch, t, d), y_sample.reshape(dec_batch, dec_seq, d),
            new_gla_p, new_gdn_p, jnp.stack(new_conv_p, 1),
            new_gla_s, new_gdn_s, jnp.stack(new_conv_s, 1))
```
